```python
import jax, jax.numpy as jnp
from jax import lax
import numpy as np

D_MODEL = 2048
BATCH = 32
SEQ = 256
DEPTH = 1
DEC_BATCH = 4
DEC_SEQ = 1024
PAST_LEN = 256

GRID_W = 64
NA_HEADS = 8
HEAD_DIM = 128
NA_WIDTH = NA_HEADS * HEAD_DIM
FOURIER_GROUPS = 4
FOURIER_GROUP_DIM = 256
FOURIER_WIDTH = FOURIER_GROUPS * FOURIER_GROUP_DIM
WIN_ROWS_MAX = 8
WIN_COLS = 16
N_GROUPS = 4
EXPERTS_PER_GROUP = 4
N_EXPERTS = N_GROUPS * EXPERTS_PER_GROUP
TOP_K_FINE = 2
EXPERT_HIDDEN = 512
N_MOD = 6
IN_WIDTH = FOURIER_WIDTH + 3 * NA_WIDTH + 2 * D_MODEL
Q_BLOCK = 128
EPS = 1e-6
NEG_INF = -1e30

kernel_name = "hybrid_fnet_natten_hmoe_diffusion_step"


def _rmsnorm(x, g):
    xf = x.astype(jnp.float32)
    y = xf * lax.rsqrt(jnp.mean(xf * xf, axis=-1, keepdims=True) + EPS)
    return (y * g.astype(jnp.float32)).astype(x.dtype)


def _adaln(cvec, w_ada, b_ada):
    mod = jax.nn.silu(cvec) @ w_ada + b_ada
    return jnp.split(mod, N_MOD, axis=-1)


def _modulate(x, g, shift, scale):
    return _rmsnorm(x, g) * (1 + scale) + shift


def _project(h, w_in, b_in):
    b, n, _ = h.shape
    p = h @ w_in + b_in
    o = 0
    u_f = p[..., o:o + FOURIER_WIDTH]; o += FOURIER_WIDTH
    q = p[..., o:o + NA_WIDTH].reshape(b, n, NA_HEADS, HEAD_DIM); o += NA_WIDTH
    k = p[..., o:o + NA_WIDTH].reshape(b, n, NA_HEADS, HEAD_DIM); o += NA_WIDTH
    v = p[..., o:o + NA_WIDTH].reshape(b, n, NA_HEADS, HEAD_DIM); o += NA_WIDTH
    g_a = p[..., o:o + D_MODEL]; o += D_MODEL
    g_b = p[..., o:o + D_MODEL]
    return u_f, q, k, v, g_a, g_b


def _fourier_mix(u):
    b, n, _ = u.shape
    ug = u.reshape(b, n, FOURIER_GROUPS, FOURIER_GROUP_DIM).astype(jnp.float32)
    f = jnp.fft.fft2(ug, axes=(1, 3), norm="ortho").real
    return f.reshape(b, n, FOURIER_WIDTH).astype(u.dtype)


def _ctx_self_attention(q, k, v):
    b, l, h, d = q.shape
    nb = l // Q_BLOCK
    scale = HEAD_DIM ** -0.5
    qb = q.reshape(b, nb, Q_BLOCK, h, d).transpose(1, 0, 2, 3, 4)

    def blk(qi):
        s = jnp.einsum('bqhd,bkhd->bhqk', qi, k).astype(jnp.float32) * scale
        p = jax.nn.softmax(s, axis=-1).astype(v.dtype)
        return jnp.einsum('bhqk,bkhd->bqhd', p, v)

    o = lax.map(blk, qb)
    return o.transpose(1, 0, 2, 3, 4).reshape(b, l, h * d)


def _latent_na(q, k, v, k_ctx, v_ctx, rpb):
    b, n, h, d = q.shape
    rows = n // GRID_W
    kh = min(WIN_ROWS_MAX, rows)
    scale = HEAD_DIM ** -0.5
    qg = q.reshape(b, rows, GRID_W, h, d)
    kg = k.reshape(b, rows, GRID_W, h, d)
    vg = v.reshape(b, rows, GRID_W, h, d)
    col = np.arange(GRID_W)
    col_start = np.clip(col - WIN_COLS // 2, 0, GRID_W - WIN_COLS)
    col_ok = (col[None, :] >= col_start[:, None]) & (col[None, :] < col_start[:, None] + WIN_COLS)
    dc_idx = np.clip(col[None, :] - col[:, None] + WIN_COLS - 1, 0, 2 * WIN_COLS - 2)
    mask = jnp.asarray(np.broadcast_to(col_ok[:, None, :], (GRID_W, kh, GRID_W)).reshape(GRID_W, kh * GRID_W))

    def row_block(args):
        r, q_r = args
        start = jnp.clip(r - kh // 2, 0, rows - kh)
        k_r = lax.dynamic_slice_in_dim(kg, start, kh, axis=1).reshape(b, kh * GRID_W, h, d)
        v_r = lax.dynamic_slice_in_dim(vg, start, kh, axis=1).reshape(b, kh * GRID_W, h, d)
        dr_idx = start + jnp.arange(kh) - r + WIN_ROWS_MAX - 1
        bias = rpb[:, dr_idx][:, :, dc_idx]
        bias = bias.transpose(0, 2, 1, 3).reshape(h, GRID_W, kh * GRID_W).astype(jnp.float32)
        s_nb = jnp.einsum('bqhd,bkhd->bhqk', q_r, k_r).astype(jnp.float32) * scale + bias
        s_nb = jnp.where(mask, s_nb, NEG_INF)
        s_cx = jnp.einsum('bqhd,bchd->bhqc', q_r, k_ctx).astype(jnp.float32) * scale
        p = jax.nn.softmax(jnp.concatenate([s_nb, s_cx], axis=-1), axis=-1).astype(v.dtype)
        nk = kh * GRID_W
        return (jnp.einsum('bhqk,bkhd->bqhd', p[..., :nk], v_r)
                + jnp.einsum('bhqc,bchd->bqhd', p[..., nk:], v_ctx))

    o = lax.map(row_block, (jnp.arange(rows), qg.transpose(1, 0, 2, 3, 4)))
    return o.transpose(1, 0, 2, 3, 4).reshape(b, n, h * d)


def _merge(u_f, na_out, g_a, g_b, w_fourier, w_na_o, w_out):
    a = _fourier_mix(u_f) @ w_fourier
    bb = na_out @ w_na_o
    merged = jax.nn.sigmoid(g_a) * a + jax.nn.sigmoid(g_b) * bb
    return merged @ w_out


def _hier_moe(h, w_rg, b_rg, w_re, b_re, w_g, w_u, w_d):
    shp = h.shape
    t = h.reshape(-1, D_MODEL)
    lg = (t @ w_rg + b_rg).astype(jnp.float32)
    pg = jax.nn.softmax(lg, axis=-1)
    gsel = jnp.argmax(lg, axis=-1)
    p_sel = jnp.take_along_axis(pg, gsel[:, None], axis=1)
    lf = (t @ w_re + b_re).astype(jnp.float32).reshape(-1, N_GROUPS, EXPERTS_PER_GROUP)
    lf_sel = jnp.take_along_axis(lf, gsel[:, None, None], axis=1)[:, 0]
    tv, ti = lax.top_k(lf_sel, TOP_K_FINE)
    wf = jax.nn.softmax(tv, axis=-1) * p_sel
    eid = gsel[:, None] * EXPERTS_PER_GROUP + ti
    comb = jnp.sum(jax.nn.one_hot(eid, N_EXPERTS, dtype=jnp.float32) * wf[..., None], axis=1).astype(h.dtype)
    y = jnp.zeros_like(t)
    for e in range(N_EXPERTS):
        hid = jax.nn.silu(t @ w_g[e]) * (t @ w_u[e])
        y = y + comb[:, e:e + 1] * (hid @ w_d[e])
    return y.reshape(shp)


def setup_inputs(seed: int = 0) -> dict:
    key = jax.random.key(seed)
    ks = jax.random.split(key, 24)
    f32 = jnp.float32

    def nrm(k, shape, scale):
        return jax.random.normal(k, shape, f32) * scale

    cache_shape = (DEC_BATCH, DEPTH, PAST_LEN, NA_HEADS, HEAD_DIM)
    return {
        "x_prompt": nrm(ks[0], (BATCH, SEQ, D_MODEL), 1.0),
        "x_sample": nrm(ks[1], (DEC_BATCH, DEC_SEQ, D_MODEL), 1.0),
        "cache_k_ctx": nrm(ks[2], cache_shape, 1.0),
        "cache_v_ctx": nrm(ks[3], cache_shape, 1.0),
        "c": nrm(ks[4], (DEC_BATCH, D_MODEL), 1.0),
        "c_ctx": nrm(ks[5], (D_MODEL,), 1.0),
        "norm1_g": 1.0 + nrm(ks[6], (DEPTH, D_MODEL), 0.02),
        "norm2_g": 1.0 + nrm(ks[7], (DEPTH, D_MODEL), 0.02),
        "w_ada": nrm(ks[8], (DEPTH, D_MODEL, N_MOD * D_MODEL), 0.5 * D_MODEL ** -0.5),
        "b_ada": nrm(ks[9], (DEPTH, N_MOD * D_MODEL), 0.01),
        "w_in": nrm(ks[10], (DEPTH, D_MODEL, IN_WIDTH), D_MODEL ** -0.5),
        "b_in": nrm(ks[11], (DEPTH, IN_WIDTH), 0.01),
        "w_fourier": nrm(ks[12], (DEPTH, FOURIER_WIDTH, D_MODEL), FOURIER_WIDTH ** -0.5),
        "w_na_o": nrm(ks[13], (DEPTH, NA_WIDTH, D_MODEL), NA_WIDTH ** -0.5),
        "rpb": nrm(ks[14], (DEPTH, NA_HEADS, 2 * WIN_ROWS_MAX - 1, 2 * WIN_COLS - 1), 0.02),
        "w_out": nrm(ks[15], (DEPTH, D_MODEL, D_MODEL), D_MODEL ** -0.5),
        "w_router_group": nrm(ks[16], (DEPTH, D_MODEL, N_GROUPS), D_MODEL ** -0.5),
        "b_router_group": nrm(ks[17], (DEPTH, N_GROUPS), 0.01),
        "w_router_expert": nrm(ks[18], (DEPTH, D_MODEL, N_EXPERTS), D_MODEL ** -0.5),
        "b_router_expert": nrm(ks[19], (DEPTH, N_EXPERTS), 0.01),
        "w_exp_gate": nrm(ks[20], (DEPTH, N_EXPERTS, D_MODEL, EXPERT_HIDDEN), D_MODEL ** -0.5),
        "w_exp_up": nrm(ks[21], (DEPTH, N_EXPERTS, D_MODEL, EXPERT_HIDDEN), D_MODEL ** -0.5),
        "w_exp_down": nrm(ks[22], (DEPTH, N_EXPERTS, EXPERT_HIDDEN, D_MODEL), EXPERT_HIDDEN ** -0.5),
        "final_norm_g": 1.0 + nrm(ks[23], (D_MODEL,), 0.02),
    }


def reference(x_prompt, x_sample, cache_k_ctx, cache_v_ctx, c, c_ctx,
              norm1_g, norm2_g, w_ada, b_ada, w_in, b_in, w_fourier, w_na_o, rpb, w_out,
              w_router_group, b_router_group, w_router_expert, b_router_expert,
              w_exp_gate, w_exp_up, w_exp_down, final_norm_g):
    xp = x_prompt
    xs = x_sample
    k_list = []
    v_list = []
    for l in range(DEPTH):
        sh1c, sc1c, ga1c, sh2c, sc2c, ga2c = _adaln(c_ctx, w_ada[l], b_ada[l])
        mods = _adaln(c, w_ada[l], b_ada[l])
        sh1s, sc1s, ga1s, sh2s, sc2s, ga2s = [m[:, None, :] for m in mods]

        h = _modulate(xp, norm1_g[l], sh1c, sc1c)
        u_f, q, k, v, g_a, g_b = _project(h, w_in[l], b_in[l])
        k_list.append(k)
        v_list.append(v)
        na = _ctx_self_attention(q, k, v)
        xp = xp + ga1c * _merge(u_f, na, g_a, g_b, w_fourier[l], w_na_o[l], w_out[l])
        h2 = _modulate(xp, norm2_g[l], sh2c, sc2c)
        xp = xp + ga2c * _hier_moe(h2, w_router_group[l], b_router_group[l], w_router_expert[l],
                                   b_router_expert[l], w_exp_gate[l], w_exp_up[l], w_exp_down[l])

        h = _modulate(xs, norm1_g[l], sh1s, sc1s)
        u_f, q, k, v, g_a, g_b = _project(h, w_in[l], b_in[l])
        na = _latent_na(q, k, v, cache_k_ctx[:, l], cache_v_ctx[:, l], rpb[l])
        xs = xs + ga1s * _merge(u_f, na, g_a, g_b, w_fourier[l], w_na_o[l], w_out[l])
        h2 = _modulate(xs, norm2_g[l], sh2s, sc2s)
        xs = xs + ga2s * _hier_moe(h2, w_router_group[l], b_router_group[l], w_router_expert[l],
                                   b_router_expert[l], w_exp_gate[l], w_exp_up[l], w_exp_down[l])

    y_prompt = _rmsnorm(xp, final_norm_g)
    y_sample = _rmsnorm(xs, final_norm_g)
    new_k_ctx = jnp.stack(k_list, axis=1)
    new_v_ctx = jnp.stack(v_list, axis=1)
    return (y_prompt, y_sample, new_k_ctx, new_v_ctx)
```

```python
import functools

import jax
import jax.numpy as jnp
import numpy as np
from jax import lax
from jax.experimental import pallas as pl
from jax.experimental.pallas import tpu as pltpu

D_MODEL = 2048
BATCH = 32
SEQ = 256
DEC_BATCH = 4
DEC_SEQ = 1024
PAST_LEN = 256
GRID_W = 64
GRID_ROWS = DEC_SEQ // GRID_W
NA_HEADS = 8
HEAD_DIM = 128
NA_WIDTH = NA_HEADS * HEAD_DIM
FOURIER_GROUPS = 4
FOURIER_GROUP_DIM = 256
FOURIER_WIDTH = FOURIER_GROUPS * FOURIER_GROUP_DIM
WIN_ROWS = 8
WIN_COLS = 16
N_GROUPS = 4
EXPERTS_PER_GROUP = 4
N_EXPERTS = N_GROUPS * EXPERTS_PER_GROUP
EXPERT_HIDDEN = 512
N_MOD = 6
IN_WIDTH = FOURIER_WIDTH + 3 * NA_WIDTH + 2 * D_MODEL
EPS = 1e-6
NEG_INF = -1e30

T_CTX = BATCH * SEQ
T_LAT = DEC_BATCH * DEC_SEQ
T_ALL = T_CTX + T_LAT

LANES = 128
CHUNKS = D_MODEL // LANES
MOD_ROWS = 8

PAIR_A = (0, 0, 0, 1, 1, 3)
PAIR_B = (1, 2, 3, 3, 2, 2)
N_PAIRS = len(PAIR_A)
N_CLASSES = N_GROUPS * N_PAIRS
MOE_TILE = 256
MOE_MAX_TILES = T_ALL // MOE_TILE + N_CLASSES
T_PAD = MOE_MAX_TILES * MOE_TILE

VMEM_LIMIT = 56 * 1024 * 1024

bf16 = jnp.bfloat16
f32 = jnp.float32


def _params(n_axes, vmem=VMEM_LIMIT):
    return pltpu.CompilerParams(dimension_semantics=("arbitrary",) * n_axes, vmem_limit_bytes=vmem)


def _mod_row(tile, n_ctx_tiles, tiles_per_request):
    return jnp.where(tile < n_ctx_tiles, 0, 1 + (tile - n_ctx_tiles) // tiles_per_request)


def _mod_vec(mods_ref, row, k):
    return mods_ref[pl.ds(row, 1), k * D_MODEL:(k + 1) * D_MODEL]


def _rms(x):
    return x * lax.rsqrt(jnp.mean(x * x, axis=-1, keepdims=True) + EPS)


def _ada_kernel(c_ref, w_ref, b_ref, o_ref):
    c = c_ref[...]
    s = (c * jax.nn.sigmoid(c)).astype(bf16)
    o_ref[...] = jnp.dot(s, w_ref[...].astype(bf16), preferred_element_type=f32) + b_ref[...]


def _ada(c_all, w_ada, b_ada):
    tn = 1024
    n = N_MOD * D_MODEL
    return pl.pallas_call(
        _ada_kernel,
        grid=(n // tn,),
        in_specs=[pl.BlockSpec((MOD_ROWS, D_MODEL), lambda j: (0, 0)),
                  pl.BlockSpec((D_MODEL, tn), lambda j: (0, j)),
                  pl.BlockSpec((1, tn), lambda j: (0, j))],
        out_specs=pl.BlockSpec((MOD_ROWS, tn), lambda j: (0, j)),
        out_shape=jax.ShapeDtypeStruct((MOD_ROWS, n), f32),
        compiler_params=_params(1),
        name="ada",
    )(c_all, w_ada, b_ada)


MOD_TILE = 512


def _mod_kernel(xp_ref, xs_ref, mods_ref, g_ref, o_ref):
    i = pl.program_id(0)
    n_ctx = T_CTX // MOD_TILE
    row = _mod_row(i, n_ctx, DEC_SEQ // MOD_TILE)
    sh = _mod_vec(mods_ref, row, 0)
    sc = _mod_vec(mods_ref, row, 1)

    def run(x_ref):
        h = _rms(x_ref[...]) * g_ref[...] * (1.0 + sc) + sh
        o_ref[...] = h.astype(bf16)

    @pl.when(i < n_ctx)
    def _():
        run(xp_ref)

    @pl.when(i >= n_ctx)
    def _():
        run(xs_ref)


def _modulate1(xp, xs, mods, g):
    n_ctx = T_CTX // MOD_TILE
    return pl.pallas_call(
        _mod_kernel,
        grid=(T_ALL // MOD_TILE,),
        in_specs=[pl.BlockSpec((MOD_TILE, D_MODEL), lambda i: (jnp.minimum(i, n_ctx - 1), 0)),
                  pl.BlockSpec((MOD_TILE, D_MODEL), lambda i: (jnp.maximum(i - n_ctx, 0), 0)),
                  pl.BlockSpec((MOD_ROWS, N_MOD * D_MODEL), lambda i: (0, 0)),
                  pl.BlockSpec((1, D_MODEL), lambda i: (0, 0))],
        out_specs=pl.BlockSpec((MOD_TILE, D_MODEL), lambda i: (i, 0)),
        out_shape=jax.ShapeDtypeStruct((T_ALL, D_MODEL), bf16),
        compiler_params=_params(1),
        name="modulate1",
    )(xp, xs, mods, g)


PROJ_TM = 512
PROJ_TN = 1024
K_COL = (FOURIER_WIDTH + NA_WIDTH) // PROJ_TN
V_COL = (FOURIER_WIDTH + 2 * NA_WIDTH) // PROJ_TN


def _proj_kernel(h_ref, w_ref, b_ref, p_ref, k_ref, v_ref, wb_ref):
    j = pl.program_id(0)
    i = pl.program_id(1)
    n_ctx = T_CTX // PROJ_TM

    @pl.when(i == 0)
    def _():
        wb_ref[...] = w_ref[...].astype(bf16)

    acc = jnp.dot(h_ref[...], wb_ref[...], preferred_element_type=f32) + b_ref[...]
    p_ref[...] = acc.astype(bf16)

    @pl.when((j == K_COL) & (i < n_ctx))
    def _():
        k_ref[...] = acc

    @pl.when((j == V_COL) & (i < n_ctx))
    def _():
        v_ref[...] = acc


def _kv_index(col):
    n_ctx = T_CTX // PROJ_TM

    def index(j, i):
        during = jnp.minimum(i, n_ctx - 1)
        return (jnp.where(j == col, during, jnp.where(j < col, 0, n_ctx - 1)), 0)

    return index


def _project(h1, w_in, b_in):
    return pl.pallas_call(
        _proj_kernel,
        grid=(IN_WIDTH // PROJ_TN, T_ALL // PROJ_TM),
        in_specs=[pl.BlockSpec((PROJ_TM, D_MODEL), lambda j, i: (i, 0)),
                  pl.BlockSpec((D_MODEL, PROJ_TN), lambda j, i: (0, j)),
                  pl.BlockSpec((1, PROJ_TN), lambda j, i: (0, j))],
        out_specs=[pl.BlockSpec((PROJ_TM, PROJ_TN), lambda j, i: (i, j)),
                   pl.BlockSpec((PROJ_TM, PROJ_TN), _kv_index(K_COL)),
                   pl.BlockSpec((PROJ_TM, PROJ_TN), _kv_index(V_COL))],
        out_shape=[jax.ShapeDtypeStruct((T_ALL, IN_WIDTH), bf16),
                   jax.ShapeDtypeStruct((T_CTX, NA_WIDTH), f32),
                   jax.ShapeDtypeStruct((T_CTX, NA_WIDTH), f32)],
        scratch_shapes=[pltpu.VMEM((D_MODEL, PROJ_TN), bf16)],
        compiler_params=_params(2),
        name="project",
    )(h1, w_in, b_in)


BIAS_LANES = 2 * WIN_ROWS * GRID_W


def _bias_kernel(r_ref, o_ref):
    qc = lax.broadcasted_iota(jnp.int32, (GRID_W, BIAS_LANES), 0)
    kc = lax.broadcasted_iota(jnp.int32, (GRID_W, BIAS_LANES), 1) & (GRID_W - 1)
    dc = kc - qc + (WIN_COLS - 1)
    start = jnp.clip(qc - WIN_COLS // 2, 0, GRID_W - WIN_COLS)
    ok = (kc >= start) & (kc < start + WIN_COLS)
    acc = jnp.zeros((GRID_W, BIAS_LANES), f32)
    for b in range(2 * WIN_COLS - 1):
        acc = jnp.where(dc == b, r_ref[0, b:b + 1, :], acc)
    o_ref[0] = jnp.where(ok, acc, NEG_INF)


def _bias_table(rpb):
    nb = 2 * WIN_COLS - 1
    rpbx = jnp.repeat(jnp.transpose(rpb, (0, 2, 1)), GRID_W, axis=2)
    rpbx = jnp.pad(rpbx, ((0, 0), (0, 0), (0, BIAS_LANES - rpbx.shape[2])))
    return pl.pallas_call(
        _bias_kernel,
        grid=(NA_HEADS,),
        in_specs=[pl.BlockSpec((1, nb, BIAS_LANES), lambda h: (h, 0, 0))],
        out_specs=pl.BlockSpec((1, GRID_W, BIAS_LANES), lambda h: (h, 0, 0)),
        out_shape=jax.ShapeDtypeStruct((NA_HEADS, GRID_W, BIAS_LANES), f32),
        compiler_params=_params(1),
        name="bias_table",
    )(rpbx)


def _qk(q, k):
    return lax.dot_general(q, k, (((1,), (1,)), ((), ())), preferred_element_type=f32)


def _attn_ctx_kernel(q_ref, k_ref, v_ref, o_ref):
    scale = HEAD_DIM ** -0.5
    for h in range(NA_HEADS):
        sl = slice(h * HEAD_DIM, (h + 1) * HEAD_DIM)
        s = _qk(q_ref[:, sl], k_ref[:, sl]) * scale
        p = jnp.exp(s - jnp.max(s, axis=-1, keepdims=True))
        l = jnp.sum(p, axis=-1, keepdims=True)
        o = jnp.dot(p.astype(bf16), v_ref[:, sl], preferred_element_type=f32) / l
        o_ref[:, sl] = o.astype(bf16)


def _attn_ctx(p_all):
    qcol = FOURIER_WIDTH // NA_WIDTH
    return pl.pallas_call(
        _attn_ctx_kernel,
        grid=(BATCH,),
        in_specs=[pl.BlockSpec((SEQ, NA_WIDTH), lambda b: (b, qcol)),
                  pl.BlockSpec((SEQ, NA_WIDTH), lambda b: (b, qcol + 1)),
                  pl.BlockSpec((SEQ, NA_WIDTH), lambda b: (b, qcol + 2))],
        out_specs=pl.BlockSpec((SEQ, NA_WIDTH), lambda b: (b, 0)),
        out_shape=jax.ShapeDtypeStruct((T_CTX, NA_WIDTH), bf16),
        compiler_params=_params(1),
        name="attn_ctx",
    )(p_all, p_all, p_all)


def _attn_lat_kernel(q_ref, k_ref, v_ref, kc_ref, vc_ref, tt_ref, o_ref):
    scale = HEAD_DIM ** -0.5
    kcb = kc_ref[0].astype(bf16)
    vcb = vc_ref[0].astype(bf16)
    nk = WIN_ROWS * GRID_W
    for r in range(GRID_ROWS):
        start = min(max(r - WIN_ROWS // 2, 0), GRID_ROWS - WIN_ROWS)
        a0 = start - r + WIN_ROWS - 1
        q = q_ref[r * GRID_W:(r + 1) * GRID_W, :]
        kw = k_ref[start * GRID_W:start * GRID_W + nk, :]
        vw = v_ref[start * GRID_W:start * GRID_W + nk, :]
        s_nb = _qk(q, kw) * scale + tt_ref[0, :, a0 * GRID_W:a0 * GRID_W + nk]
        s_cx = _qk(q, kcb) * scale
        m = jnp.maximum(jnp.max(s_nb, axis=-1, keepdims=True), jnp.max(s_cx, axis=-1, keepdims=True))
        p_nb = jnp.exp(s_nb - m)
        p_cx = jnp.exp(s_cx - m)
        l = jnp.sum(p_nb, axis=-1, keepdims=True) + jnp.sum(p_cx, axis=-1, keepdims=True)
        o = (jnp.dot(p_nb.astype(bf16), vw, preferred_element_type=f32)
             + jnp.dot(p_cx.astype(bf16), vcb, preferred_element_type=f32)) / l
        o_ref[r * GRID_W:(r + 1) * GRID_W, :] = o.astype(bf16)


def _attn_lat(p_all, ck, cv, tt):
    row0 = T_CTX // DEC_SEQ
    qcol = FOURIER_WIDTH // HEAD_DIM
    return pl.pallas_call(
        _attn_lat_kernel,
        grid=(DEC_BATCH, NA_HEADS),
        in_specs=[pl.BlockSpec((DEC_SEQ, HEAD_DIM), lambda b, h: (row0 + b, qcol + h)),
                  pl.BlockSpec((DEC_SEQ, HEAD_DIM), lambda b, h: (row0 + b, qcol + NA_HEADS + h)),
                  pl.BlockSpec((DEC_SEQ, HEAD_DIM), lambda b, h: (row0 + b, qcol + 2 * NA_HEADS + h)),
                  pl.BlockSpec((1, PAST_LEN, HEAD_DIM), lambda b, h: (b, 0, h)),
                  pl.BlockSpec((1, PAST_LEN, HEAD_DIM), lambda b, h: (b, 0, h)),
                  pl.BlockSpec((1, GRID_W, BIAS_LANES), lambda b, h: (h, 0, 0))],
        out_specs=pl.BlockSpec((DEC_SEQ, HEAD_DIM), lambda b, h: (b, h)),
        out_shape=jax.ShapeDtypeStruct((T_LAT, NA_WIDTH), bf16),
        compiler_params=_params(2),
        name="attn_lat",
    )(p_all, p_all, p_all, ck, cv, tt)


def _dft_consts(n):
    def cs(m):
        idx = (np.arange(m)[:, None] * np.arange(m)[None, :]) % m
        ang = 2.0 * np.pi * idx.astype(np.float64) / m
        return np.cos(ang), np.sin(ang)

    cc, sc = cs(FOURIER_GROUP_DIM)
    cn, sn = cs(n)
    chan = np.concatenate([cc, sc], axis=1) / np.sqrt(FOURIER_GROUP_DIM)
    pos = np.concatenate([cn, -sn], axis=1) / np.sqrt(n)
    return jnp.asarray(chan, f32).astype(bf16), jnp.asarray(pos, f32).astype(bf16)


def _fourier_kernel(u_ref, chan_ref, pos_ref, o_ref):
    gd = FOURIER_GROUP_DIM
    ys = [jnp.dot(u_ref[:, g * gd:(g + 1) * gd], chan_ref[...], preferred_element_type=f32)
          for g in range(FOURIER_GROUPS)]
    yc = jnp.concatenate([y[:, :gd] for y in ys], axis=1)
    ysn = jnp.concatenate([y[:, gd:] for y in ys], axis=1)
    stacked = jnp.concatenate([yc, ysn], axis=0).astype(bf16)
    o_ref[...] = jnp.dot(pos_ref[...], stacked, preferred_element_type=f32).astype(bf16)


def _fourier(p_all, n, n_req, row0):
    chan, pos = _dft_consts(n)
    return pl.pallas_call(
        _fourier_kernel,
        grid=(n_req,),
        in_specs=[pl.BlockSpec((n, FOURIER_WIDTH), lambda b: (row0 + b, 0)),
                  pl.BlockSpec((FOURIER_GROUP_DIM, 2 * FOURIER_GROUP_DIM), lambda b: (0, 0)),
                  pl.BlockSpec((n, 2 * n), lambda b: (0, 0))],
        out_specs=pl.BlockSpec((n, FOURIER_WIDTH), lambda b: (b, 0)),
        out_shape=jax.ShapeDtypeStruct((n_req * n, FOURIER_WIDTH), bf16),
        compiler_params=_params(1),
        name=f"fourier_{n}",
    )(p_all, chan, pos)


MERGE_TM = 256
ROUTE_LANES = LANES


def _class_of(logits):
    lane = lax.broadcasted_iota(jnp.int32, logits.shape, 1)
    big = jnp.int32(ROUTE_LANES)
    is_group = lane < N_GROUPS
    mg = jnp.max(jnp.where(is_group, logits, -jnp.inf), axis=-1, keepdims=True)
    gsel = jnp.min(jnp.where(is_group & (logits == mg), lane, big), axis=-1, keepdims=True)
    lo = N_GROUPS + EXPERTS_PER_GROUP * gsel
    in_group = (lane >= lo) & (lane < lo + EXPERTS_PER_GROUP)
    m1 = jnp.max(jnp.where(in_group, logits, -jnp.inf), axis=-1, keepdims=True)
    i1 = jnp.min(jnp.where(in_group & (logits == m1), lane, big), axis=-1, keepdims=True)
    rest = in_group & (lane != i1)
    m2 = jnp.max(jnp.where(rest, logits, -jnp.inf), axis=-1, keepdims=True)
    i2 = jnp.min(jnp.where(rest & (logits == m2), lane, big), axis=-1, keepdims=True)
    e_lo = jnp.minimum(i1, i2) - lo
    e_hi = jnp.maximum(i1, i2) - lo
    pair = jnp.zeros_like(e_lo)
    for idx in range(N_PAIRS):
        a, b = sorted((PAIR_A[idx], PAIR_B[idx]))
        pair = jnp.where((e_lo == a) & (e_hi == b), idx, pair)
    return gsel * N_PAIRS + pair


def _merge_kernel(fmc_ref, fml_ref, nac_ref, nal_ref, ga_ref, gb_ref, xp_ref, xs_ref, mods_ref, g2_ref,
                  wf_ref, wn_ref, wo_ref, wr_ref, br_ref,
                  x1_ref, h2c_ref, info_ref, cnt_ref, carry_ref):
    i = pl.program_id(0)
    n_ctx = T_CTX // MERGE_TM
    row = _mod_row(i, n_ctx, DEC_SEQ // MERGE_TM)

    @pl.when(i == 0)
    def _():
        carry_ref[...] = jnp.zeros_like(carry_ref)

    def sublayer1(fm_ref, na_ref, x_ref):
        a = jnp.dot(fm_ref[...], wf_ref[...], preferred_element_type=f32)
        b = jnp.dot(na_ref[...], wn_ref[...], preferred_element_type=f32)
        merged = jax.nn.sigmoid(ga_ref[...].astype(f32)) * a + jax.nn.sigmoid(gb_ref[...].astype(f32)) * b
        out = jnp.dot(merged.astype(bf16), wo_ref[...], preferred_element_type=f32)
        x1_ref[...] = x_ref[...] + _mod_vec(mods_ref, row, 2) * out

    @pl.when(i < n_ctx)
    def _():
        sublayer1(fmc_ref, nac_ref, xp_ref)

    @pl.when(i >= n_ctx)
    def _():
        sublayer1(fml_ref, nal_ref, xs_ref)

    h2 = _rms(x1_ref[...]) * g2_ref[...] * (1.0 + _mod_vec(mods_ref, row, 4)) + _mod_vec(mods_ref, row, 3)
    for s in range(CHUNKS):
        h2c_ref[pl.ds(s, MERGE_TM, stride=CHUNKS), :] = h2[:, s * LANES:(s + 1) * LANES]

    logits = jnp.dot(h2.astype(bf16), wr_ref[...], preferred_element_type=f32) + br_ref[...]
    cls = _class_of(logits)
    lane = lax.broadcasted_iota(jnp.int32, logits.shape, 1)
    onehot = (lane == cls).astype(f32)
    r = lax.broadcasted_iota(jnp.int32, (MERGE_TM, MERGE_TM), 0)
    c = lax.broadcasted_iota(jnp.int32, (MERGE_TM, MERGE_TM), 1)
    before = (c < r).astype(bf16)
    prefix = jnp.dot(before, onehot.astype(bf16), preferred_element_type=f32) + carry_ref[...]
    rank = jnp.sum(onehot * prefix, axis=-1, keepdims=True)
    info_ref[...] = jnp.where(lane == 0, cls.astype(f32), jnp.where(lane == 1, rank, 0.0))
    carry_ref[...] = carry_ref[...] + jnp.sum(onehot, axis=0, keepdims=True)
    cnt_ref[...] = jnp.broadcast_to(carry_ref[...], cnt_ref.shape)


def _merge(fm_c, fm_l, na_c, na_l, p_all, xp, xs, mods, g2, wf, wn, wo, wr, br):
    n_ctx = T_CTX // MERGE_TM
    gcol = (FOURIER_WIDTH + 3 * NA_WIDTH) // D_MODEL
    const = lambda i: (0, 0)
    ctx_tile = lambda i: (jnp.minimum(i, n_ctx - 1), 0)
    lat_tile = lambda i: (jnp.maximum(i - n_ctx, 0), 0)
    once = pl.Buffered(1)
    return pl.pallas_call(
        _merge_kernel,
        grid=(T_ALL // MERGE_TM,),
        in_specs=[pl.BlockSpec((MERGE_TM, FOURIER_WIDTH), ctx_tile),
                  pl.BlockSpec((MERGE_TM, FOURIER_WIDTH), lat_tile),
                  pl.BlockSpec((MERGE_TM, NA_WIDTH), ctx_tile),
                  pl.BlockSpec((MERGE_TM, NA_WIDTH), lat_tile),
                  pl.BlockSpec((MERGE_TM, D_MODEL), lambda i: (i, gcol)),
                  pl.BlockSpec((MERGE_TM, D_MODEL), lambda i: (i, gcol + 1)),
                  pl.BlockSpec((MERGE_TM, D_MODEL), ctx_tile),
                  pl.BlockSpec((MERGE_TM, D_MODEL), lat_tile),
                  pl.BlockSpec((MOD_ROWS, N_MOD * D_MODEL), const, pipeline_mode=once),
                  pl.BlockSpec((1, D_MODEL), const, pipeline_mode=once),
                  pl.BlockSpec((FOURIER_WIDTH, D_MODEL), const, pipeline_mode=once),
                  pl.BlockSpec((NA_WIDTH, D_MODEL), const, pipeline_mode=once),
                  pl.BlockSpec((D_MODEL, D_MODEL), const, pipeline_mode=once),
                  pl.BlockSpec((D_MODEL, ROUTE_LANES), const, pipeline_mode=once),
                  pl.BlockSpec((1, ROUTE_LANES), const, pipeline_mode=once)],
        out_specs=[pl.BlockSpec((MERGE_TM, D_MODEL), lambda i: (i, 0)),
                   pl.BlockSpec((MERGE_TM * CHUNKS, LANES), lambda i: (i, 0)),
                   pl.BlockSpec((MERGE_TM, ROUTE_LANES), lambda i: (i, 0)),
                   pl.BlockSpec((8, ROUTE_LANES), const)],
        out_shape=[jax.ShapeDtypeStruct((T_ALL, D_MODEL), f32),
                   jax.ShapeDtypeStruct((T_ALL * CHUNKS, LANES), f32),
                   jax.ShapeDtypeStruct((T_ALL, ROUTE_LANES), f32),
                   jax.ShapeDtypeStruct((8, ROUTE_LANES), f32)],
        scratch_shapes=[pltpu.VMEM((1, ROUTE_LANES), f32)],
        compiler_params=_params(1),
        name="merge",
    )(fm_c, fm_l, na_c, na_l, p_all, p_all, xp, xs, mods, g2, wf, wn, wo, wr, br)


def _perm_kernel(cls_ref, rank_ref, off_ref, pos_ref, src_ref):
    def clear(p, carry):
        src_ref[p] = 0
        return carry

    lax.fori_loop(0, T_PAD, clear, 0)

    def place(t, carry):
        p = off_ref[cls_ref[t]] + rank_ref[t]
        pos_ref[t] = p
        src_ref[p] = t
        return carry

    lax.fori_loop(0, T_ALL, place, 0)


def _permutation(cls, rank, off):
    smem = pl.BlockSpec(memory_space=pltpu.SMEM)
    return pl.pallas_call(
        _perm_kernel,
        in_specs=[smem, smem, smem],
        out_specs=[smem, smem],
        out_shape=[jax.ShapeDtypeStruct((T_ALL,), jnp.int32),
                   jax.ShapeDtypeStruct((T_PAD,), jnp.int32)],
        name="permutation",
    )(cls, rank, off)


ROW_BLOCK = MOE_TILE * CHUNKS


def _start_gather(idx_ref, base, src_hbm, buf, sem, slot):
    def body(k, carry):
        tok = idx_ref[base + k]
        pltpu.make_async_copy(src_hbm.at[pl.ds(pl.multiple_of(tok * CHUNKS, CHUNKS), CHUNKS), :],
                              buf.at[slot, pl.ds(pl.multiple_of(k * CHUNKS, CHUNKS), CHUNKS), :],
                              sem.at[slot]).start()
        return carry

    lax.fori_loop(0, MOE_TILE, body, 0, unroll=8)


def _wait_gather(buf, sem, slot):
    pltpu.make_async_copy(buf.at[slot], buf.at[slot], sem.at[slot]).wait()


def _gathered_rows(buf, slot):
    return jnp.concatenate([buf[slot, pl.ds(s, MOE_TILE, stride=CHUNKS), :] for s in range(CHUNKS)], axis=1)


def _moe_kernel(ea_ref, eb_ref, nt_ref, src_ref, h2c_hbm, wga_ref, wua_ref, wda_ref, wgb_ref, wub_ref, wdb_ref,
                wr_ref, br_ref, o_ref, buf, sem):
    i = pl.program_id(0)
    nt = nt_ref[0]

    @pl.when(i == 0)
    def _():
        _start_gather(src_ref, 0, h2c_hbm, buf, sem, 0)

    @pl.when(i + 1 < nt)
    def _():
        _start_gather(src_ref, (i + 1) * MOE_TILE, h2c_hbm, buf, sem, (i + 1) % 2)

    @pl.when(i >= nt)
    def _():
        o_ref[...] = jnp.zeros_like(o_ref)

    @pl.when(i < nt)
    def _():
        slot = i % 2
        _wait_gather(buf, sem, slot)
        x = _gathered_rows(buf, slot).astype(bf16)
        logits = jnp.dot(x, wr_ref[...], preferred_element_type=f32) + br_ref[...]
        lane = lax.broadcasted_iota(jnp.int32, logits.shape, 1)
        ea = ea_ref[i]
        eb = eb_ref[i]

        def pick(col):
            return jnp.sum(jnp.where(lane == col, logits, 0.0), axis=-1, keepdims=True)

        lg = pick(ea // EXPERTS_PER_GROUP)
        p_group = 1.0 / jnp.sum(jnp.where(lane < N_GROUPS, jnp.exp(logits - lg), 0.0), axis=-1, keepdims=True)
        la = pick(N_GROUPS + ea)
        lb = pick(N_GROUPS + eb)
        w_a = p_group / (1.0 + jnp.exp(lb - la))
        w_b = p_group / (1.0 + jnp.exp(la - lb))

        def expert(wg_ref, wu_ref, wd_ref, w):
            g = jnp.dot(x, wg_ref[0], preferred_element_type=f32)
            u = jnp.dot(x, wu_ref[0], preferred_element_type=f32)
            hid = (g * jax.nn.sigmoid(g) * u).astype(bf16)
            return w * jnp.dot(hid, wd_ref[0], preferred_element_type=f32)

        y = expert(wga_ref, wua_ref, wda_ref, w_a) + expert(wgb_ref, wub_ref, wdb_ref, w_b)
        for s in range(CHUNKS):
            o_ref[pl.ds(s, MOE_TILE, stride=CHUNKS), :] = y[:, s * LANES:(s + 1) * LANES]


def _moe(tile_ea, tile_eb, ntiles, src, h2c, wg, wu, wd, wr, br):
    up = lambda tab: (lambda i, ea, eb, nt, src: (tab(ea, eb)[i], 0, 0))
    pick_a = lambda ea, eb: ea
    pick_b = lambda ea, eb: eb
    const = lambda i, ea, eb, nt, src: (0, 0)
    grid_spec = pltpu.PrefetchScalarGridSpec(
        num_scalar_prefetch=4,
        grid=(MOE_MAX_TILES,),
        in_specs=[pl.BlockSpec(memory_space=pl.ANY),
                  pl.BlockSpec((1, D_MODEL, EXPERT_HIDDEN), up(pick_a)),
                  pl.BlockSpec((1, D_MODEL, EXPERT_HIDDEN), up(pick_a)),
                  pl.BlockSpec((1, EXPERT_HIDDEN, D_MODEL), up(pick_a)),
                  pl.BlockSpec((1, D_MODEL, EXPERT_HIDDEN), up(pick_b)),
                  pl.BlockSpec((1, D_MODEL, EXPERT_HIDDEN), up(pick_b)),
                  pl.BlockSpec((1, EXPERT_HIDDEN, D_MODEL), up(pick_b)),
                  pl.BlockSpec((D_MODEL, ROUTE_LANES), const),
                  pl.BlockSpec((1, ROUTE_LANES), const)],
        out_specs=pl.BlockSpec((ROW_BLOCK, LANES), lambda i, ea, eb, nt, src: (i, 0)),
        scratch_shapes=[pltpu.VMEM((2, ROW_BLOCK, LANES), f32), pltpu.SemaphoreType.DMA((2,))],
    )
    return pl.pallas_call(
        _moe_kernel,
        grid_spec=grid_spec,
        out_shape=jax.ShapeDtypeStruct((T_PAD * CHUNKS, LANES), f32),
        compiler_params=_params(1),
        name="moe",
    )(tile_ea, tile_eb, ntiles, src, h2c, wg, wu, wd, wg, wu, wd, wr, br)


def _final_kernel(pos_ref, ys_hbm, x1_ref, mods_ref, g_ref, o_ref, buf, sem, *, tile0, n_tiles, n_ctx_tiles):
    i = pl.program_id(0)

    @pl.when(i == 0)
    def _():
        _start_gather(pos_ref, tile0 * MOE_TILE, ys_hbm, buf, sem, 0)

    @pl.when(i + 1 < n_tiles)
    def _():
        _start_gather(pos_ref, (tile0 + i + 1) * MOE_TILE, ys_hbm, buf, sem, (i + 1) % 2)

    slot = i % 2
    _wait_gather(buf, sem, slot)
    y = _gathered_rows(buf, slot)
    row = _mod_row(tile0 + i, n_ctx_tiles, DEC_SEQ // MOE_TILE)
    x2 = x1_ref[...] + _mod_vec(mods_ref, row, 5) * y
    o_ref[...] = _rms(x2) * g_ref[...]


def _final(pos, ys, x1, mods, g, tile0, n_tiles):
    body = functools.partial(_final_kernel, tile0=tile0, n_tiles=n_tiles, n_ctx_tiles=T_CTX // MOE_TILE)
    grid_spec = pltpu.PrefetchScalarGridSpec(
        num_scalar_prefetch=1,
        grid=(n_tiles,),
        in_specs=[pl.BlockSpec(memory_space=pl.ANY),
                  pl.BlockSpec((MOE_TILE, D_MODEL), lambda i, pos: (tile0 + i, 0)),
                  pl.BlockSpec((MOD_ROWS, N_MOD * D_MODEL), lambda i, pos: (0, 0)),
                  pl.BlockSpec((1, D_MODEL), lambda i, pos: (0, 0))],
        out_specs=pl.BlockSpec((MOE_TILE, D_MODEL), lambda i, pos: (i, 0)),
        scratch_shapes=[pltpu.VMEM((2, ROW_BLOCK, LANES), f32), pltpu.SemaphoreType.DMA((2,))],
    )
    return pl.pallas_call(
        body,
        grid_spec=grid_spec,
        out_shape=jax.ShapeDtypeStruct((n_tiles * MOE_TILE, D_MODEL), f32),
        compiler_params=_params(1),
        name=f"final_{tile0}",
    )(pos, ys, x1, mods, g)


def kernel(x_prompt, x_sample, cache_k_ctx, cache_v_ctx, c, c_ctx, norm1_g, norm2_g, w_ada, b_ada, w_in, b_in,
           w_fourier, w_na_o, rpb, w_out, w_router_group, b_router_group, w_router_expert, b_router_expert,
           w_exp_gate, w_exp_up, w_exp_down, final_norm_g):
    xp = x_prompt.reshape(T_CTX, D_MODEL)
    xs = x_sample.reshape(T_LAT, D_MODEL)
    c_all = jnp.concatenate([c_ctx[None, :], c, jnp.zeros((MOD_ROWS - 1 - DEC_BATCH, D_MODEL), f32)], axis=0)
    w_route = jnp.concatenate([w_router_group[0], w_router_expert[0]], axis=1)
    w_route = jnp.pad(w_route, ((0, 0), (0, ROUTE_LANES - w_route.shape[1]))).astype(bf16)
    b_route = jnp.concatenate([b_router_group[0], b_router_expert[0]])
    b_route = jnp.pad(b_route, (0, ROUTE_LANES - b_route.shape[0]))[None, :]

    mods = _ada(c_all, w_ada[0], b_ada[0][None, :])
    h1 = _modulate1(xp, xs, mods, norm1_g)
    p_all, k_new, v_new = _project(h1, w_in[0], b_in)

    tt = _bias_table(rpb[0])
    na_c = _attn_ctx(p_all)
    ck = cache_k_ctx.reshape(DEC_BATCH, PAST_LEN, NA_WIDTH)
    cv = cache_v_ctx.reshape(DEC_BATCH, PAST_LEN, NA_WIDTH)
    na_l = _attn_lat(p_all, ck, cv, tt)
    fm_c = _fourier(p_all, SEQ, BATCH, 0)
    fm_l = _fourier(p_all, DEC_SEQ, DEC_BATCH, T_CTX // DEC_SEQ)

    x1, h2c, info, counts = _merge(fm_c, fm_l, na_c, na_l, p_all, xp, xs, mods, norm2_g,
                                   w_fourier[0].astype(bf16), w_na_o[0].astype(bf16), w_out[0].astype(bf16),
                                   w_route, b_route)

    cls = info[:, 0].astype(jnp.int32)
    rank = info[:, 1].astype(jnp.int32)
    cnt = counts[0, :N_CLASSES].astype(jnp.int32)
    padded = ((cnt + MOE_TILE - 1) // MOE_TILE) * MOE_TILE
    off_end = jnp.cumsum(padded)
    off = off_end - padded
    ntiles = (off_end[-1] // MOE_TILE).astype(jnp.int32)
    tile_start = jnp.arange(MOE_MAX_TILES, dtype=jnp.int32) * MOE_TILE
    last_start = (ntiles - 1) * MOE_TILE
    tile_cls = jnp.sum(jnp.minimum(tile_start, last_start)[:, None] >= off_end[None, :], axis=1)
    pair = tile_cls % N_PAIRS
    group = tile_cls // N_PAIRS
    pair_hot = pair[:, None] == jnp.arange(N_PAIRS)[None, :]
    tile_ea = (group * EXPERTS_PER_GROUP + jnp.sum(pair_hot * jnp.asarray(PAIR_A), axis=1)).astype(jnp.int32)
    tile_eb = (group * EXPERTS_PER_GROUP + jnp.sum(pair_hot * jnp.asarray(PAIR_B), axis=1)).astype(jnp.int32)

    pos, src = _permutation(cls, rank, off.astype(jnp.int32))
    ys = _moe(tile_ea, tile_eb, ntiles[None], src, h2c,
              w_exp_gate[0].astype(bf16), w_exp_up[0].astype(bf16), w_exp_down[0].astype(bf16), w_route, b_route)

    fg = final_norm_g[None, :]
    y_prompt = _final(pos, ys, x1, mods, fg, 0, T_CTX // MOE_TILE)
    y_sample = _final(pos, ys, x1, mods, fg, T_CTX // MOE_TILE, T_LAT // MOE_TILE)

    shape_kv = (BATCH, 1, SEQ, NA_HEADS, HEAD_DIM)
    return (y_prompt.reshape(BATCH, SEQ, D_MODEL), y_sample.reshape(DEC_BATCH, DEC_SEQ, D_MODEL),
            k_new.reshape(shape_kv), v_new.reshape(shape_kv))
```

```python
import functools

import jax
import jax.numpy as jnp
import numpy as np
from jax import lax
from jax.experimental import pallas as pl
from jax.experimental.pallas import tpu as pltpu

D_MODEL = 2048
BATCH = 32
SEQ = 256
DEC_BATCH = 4
DEC_SEQ = 1024
PAST_LEN = 256
GRID_W = 64
GRID_ROWS = DEC_SEQ // GRID_W
NA_HEADS = 8
HEAD_DIM = 128
NA_WIDTH = NA_HEADS * HEAD_DIM
FOURIER_GROUPS = 4
FOURIER_GROUP_DIM = 256
FOURIER_WIDTH = FOURIER_GROUPS * FOURIER_GROUP_DIM
WIN_ROWS = 8
WIN_COLS = 16
N_GROUPS = 4
EXPERTS_PER_GROUP = 4
N_EXPERTS = N_GROUPS * EXPERTS_PER_GROUP
EXPERT_HIDDEN = 512
N_MOD = 6
IN_WIDTH = FOURIER_WIDTH + 3 * NA_WIDTH + 2 * D_MODEL
EPS = 1e-6
NEG_INF = -1e30

T_CTX = BATCH * SEQ
T_LAT = DEC_BATCH * DEC_SEQ
T_ALL = T_CTX + T_LAT

LANES = 128
CHUNKS = D_MODEL // LANES
MOD_ROWS = 8

PAIR_A = (0, 0, 0, 1, 1, 3)
PAIR_B = (1, 2, 3, 3, 2, 2)
N_PAIRS = len(PAIR_A)
N_CLASSES = N_GROUPS * N_PAIRS
MOE_TILE = 256
MOE_MAX_TILES = T_ALL // MOE_TILE + N_CLASSES
T_PAD = MOE_MAX_TILES * MOE_TILE

VMEM_LIMIT = 56 * 1024 * 1024

bf16 = jnp.bfloat16
f32 = jnp.float32


def _params(n_axes, vmem=VMEM_LIMIT):
    return pltpu.CompilerParams(dimension_semantics=("arbitrary",) * n_axes, vmem_limit_bytes=vmem)


def _mod_row(tile, n_ctx_tiles, tiles_per_request):
    return jnp.where(tile < n_ctx_tiles, 0, 1 + (tile - n_ctx_tiles) // tiles_per_request)


def _mod_vec(mods_ref, row, k):
    return mods_ref[pl.ds(row, 1), k * D_MODEL:(k + 1) * D_MODEL]


def _rms(x):
    return x * lax.rsqrt(jnp.mean(x * x, axis=-1, keepdims=True) + EPS)


def _ada_kernel(c_ref, w_ref, b_ref, o_ref):
    c = c_ref[...]
    s = (c * jax.nn.sigmoid(c)).astype(bf16)
    o_ref[...] = jnp.dot(s, w_ref[...].astype(bf16), preferred_element_type=f32) + b_ref[...]


def _ada(c_all, w_ada, b_ada):
    tn = 1024
    n = N_MOD * D_MODEL
    return pl.pallas_call(
        _ada_kernel,
        grid=(n // tn,),
        in_specs=[pl.BlockSpec((MOD_ROWS, D_MODEL), lambda j: (0, 0)),
                  pl.BlockSpec((D_MODEL, tn), lambda j: (0, j)),
                  pl.BlockSpec((1, tn), lambda j: (0, j))],
        out_specs=pl.BlockSpec((MOD_ROWS, tn), lambda j: (0, j)),
        out_shape=jax.ShapeDtypeStruct((MOD_ROWS, n), f32),
        compiler_params=_params(1),
        name="ada",
    )(c_all, w_ada, b_ada)


MOD_TILE = 512


def _mod_kernel(xp_ref, xs_ref, mods_ref, g_ref, o_ref):
    i = pl.program_id(0)
    n_ctx = T_CTX // MOD_TILE
    row = _mod_row(i, n_ctx, DEC_SEQ // MOD_TILE)
    sh = _mod_vec(mods_ref, row, 0)
    sc = _mod_vec(mods_ref, row, 1)

    def run(x_ref):
        h = _rms(x_ref[...]) * g_ref[...] * (1.0 + sc) + sh
        o_ref[...] = h.astype(bf16)

    @pl.when(i < n_ctx)
    def _():
        run(xp_ref)

    @pl.when(i >= n_ctx)
    def _():
        run(xs_ref)


def _modulate1(xp, xs, mods, g):
    n_ctx = T_CTX // MOD_TILE
    return pl.pallas_call(
        _mod_kernel,
        grid=(T_ALL // MOD_TILE,),
        in_specs=[pl.BlockSpec((MOD_TILE, D_MODEL), lambda i: (jnp.minimum(i, n_ctx - 1), 0)),
                  pl.BlockSpec((MOD_TILE, D_MODEL), lambda i: (jnp.maximum(i - n_ctx, 0), 0)),
                  pl.BlockSpec((MOD_ROWS, N_MOD * D_MODEL), lambda i: (0, 0)),
                  pl.BlockSpec((1, D_MODEL), lambda i: (0, 0))],
        out_specs=pl.BlockSpec((MOD_TILE, D_MODEL), lambda i: (i, 0)),
        out_shape=jax.ShapeDtypeStruct((T_ALL, D_MODEL), bf16),
        compiler_params=_params(1),
        name="modulate1",
    )(xp, xs, mods, g)


PROJ_TM = 512
PROJ_TN = 1024
K_COL = (FOURIER_WIDTH + NA_WIDTH) // PROJ_TN
V_COL = (FOURIER_WIDTH + 2 * NA_WIDTH) // PROJ_TN


PROJ_STEPS = (IN_WIDTH // PROJ_TN) * (T_ALL // PROJ_TM)
CAST_SHAPES = ((N_EXPERTS * D_MODEL, EXPERT_HIDDEN, 256),
               (N_EXPERTS * D_MODEL, EXPERT_HIDDEN, 256),
               (N_EXPERTS * EXPERT_HIDDEN, D_MODEL, 64),
               (FOURIER_WIDTH, D_MODEL, 16),
               (NA_WIDTH, D_MODEL, 16),
               (D_MODEL, D_MODEL, 32))
N_CAST = len(CAST_SHAPES)


def _proj_kernel(h_ref, w_ref, b_ref, *rest):
    cast_in = rest[:N_CAST]
    p_ref, k_ref, v_ref = rest[N_CAST:N_CAST + 3]
    cast_out = rest[N_CAST + 3:2 * N_CAST + 3]
    wb_ref = rest[-1]
    j = pl.program_id(0)
    i = pl.program_id(1)
    n_ctx = T_CTX // PROJ_TM
    step = j * (T_ALL // PROJ_TM) + i

    for (rows, _, rb), src, dst in zip(CAST_SHAPES, cast_in, cast_out):
        @pl.when(step < rows // rb)
        def _(src=src, dst=dst):
            dst[...] = src[...].astype(bf16)

    @pl.when(i == 0)
    def _():
        wb_ref[...] = w_ref[...].astype(bf16)

    acc = jnp.dot(h_ref[...], wb_ref[...], preferred_element_type=f32) + b_ref[...]
    p_ref[...] = acc.astype(bf16)

    @pl.when((j == K_COL) & (i < n_ctx))
    def _():
        k_ref[...] = acc

    @pl.when((j == V_COL) & (i < n_ctx))
    def _():
        v_ref[...] = acc


def _kv_index(col):
    n_ctx = T_CTX // PROJ_TM

    def index(j, i):
        during = jnp.minimum(i, n_ctx - 1)
        return (jnp.where(j == col, during, jnp.where(j < col, 0, n_ctx - 1)), 0)

    return index


def _cast_spec(rows, cols, rb):
    m = T_ALL // PROJ_TM
    return pl.BlockSpec((rb, cols), lambda j, i: (jnp.minimum(j * m + i, rows // rb - 1), 0))


def _project(h1, w_in, b_in, cast_weights):
    assert all(rows // rb <= PROJ_STEPS for rows, _, rb in CAST_SHAPES)
    cast_specs = [_cast_spec(*s) for s in CAST_SHAPES]
    outs = pl.pallas_call(
        _proj_kernel,
        grid=(IN_WIDTH // PROJ_TN, T_ALL // PROJ_TM),
        in_specs=[pl.BlockSpec((PROJ_TM, D_MODEL), lambda j, i: (i, 0)),
                  pl.BlockSpec((D_MODEL, PROJ_TN), lambda j, i: (0, j)),
                  pl.BlockSpec((1, PROJ_TN), lambda j, i: (0, j))] + cast_specs,
        out_specs=[pl.BlockSpec((PROJ_TM, PROJ_TN), lambda j, i: (i, j)),
                   pl.BlockSpec((PROJ_TM, PROJ_TN), _kv_index(K_COL)),
                   pl.BlockSpec((PROJ_TM, PROJ_TN), _kv_index(V_COL))] + cast_specs,
        out_shape=[jax.ShapeDtypeStruct((T_ALL, IN_WIDTH), bf16),
                   jax.ShapeDtypeStruct((T_CTX, NA_WIDTH), f32),
                   jax.ShapeDtypeStruct((T_CTX, NA_WIDTH), f32)]
                  + [jax.ShapeDtypeStruct((rows, cols), bf16) for rows, cols, _ in CAST_SHAPES],
        scratch_shapes=[pltpu.VMEM((D_MODEL, PROJ_TN), bf16)],
        compiler_params=_params(2),
        name="project",
    )(h1, w_in, b_in, *cast_weights)
    return outs[0], outs[1], outs[2], outs[3:]


BIAS_LANES = 2 * WIN_ROWS * GRID_W


def _bias_kernel(r_ref, o_ref):
    qc = lax.broadcasted_iota(jnp.int32, (GRID_W, BIAS_LANES), 0)
    kc = lax.broadcasted_iota(jnp.int32, (GRID_W, BIAS_LANES), 1) & (GRID_W - 1)
    dc = kc - qc + (WIN_COLS - 1)
    start = jnp.clip(qc - WIN_COLS // 2, 0, GRID_W - WIN_COLS)
    ok = (kc >= start) & (kc < start + WIN_COLS)
    acc = jnp.zeros((GRID_W, BIAS_LANES), f32)
    for b in range(2 * WIN_COLS - 1):
        acc = jnp.where(dc == b, r_ref[0, b:b + 1, :], acc)
    o_ref[0] = jnp.where(ok, acc, NEG_INF)


def _bias_table(rpb):
    nb = 2 * WIN_COLS - 1
    rpbx = jnp.repeat(jnp.transpose(rpb, (0, 2, 1)), GRID_W, axis=2)
    rpbx = jnp.pad(rpbx, ((0, 0), (0, 0), (0, BIAS_LANES - rpbx.shape[2])))
    return pl.pallas_call(
        _bias_kernel,
        grid=(NA_HEADS,),
        in_specs=[pl.BlockSpec((1, nb, BIAS_LANES), lambda h: (h, 0, 0))],
        out_specs=pl.BlockSpec((1, GRID_W, BIAS_LANES), lambda h: (h, 0, 0)),
        out_shape=jax.ShapeDtypeStruct((NA_HEADS, GRID_W, BIAS_LANES), f32),
        compiler_params=_params(1),
        name="bias_table",
    )(rpbx)


def _qk(q, k):
    return lax.dot_general(q, k, (((1,), (1,)), ((), ())), preferred_element_type=f32)


def _attn_ctx_kernel(q_ref, k_ref, v_ref, o_ref):
    scale = HEAD_DIM ** -0.5
    for h in range(NA_HEADS):
        sl = slice(h * HEAD_DIM, (h + 1) * HEAD_DIM)
        s = _qk(q_ref[:, sl], k_ref[:, sl]) * scale
        p = jnp.exp(s - jnp.max(s, axis=-1, keepdims=True))
        l = jnp.sum(p, axis=-1, keepdims=True)
        o = jnp.dot(p.astype(bf16), v_ref[:, sl], preferred_element_type=f32) / l
        o_ref[:, sl] = o.astype(bf16)


def _attn_ctx(p_all):
    qcol = FOURIER_WIDTH // NA_WIDTH
    return pl.pallas_call(
        _attn_ctx_kernel,
        grid=(BATCH,),
        in_specs=[pl.BlockSpec((SEQ, NA_WIDTH), lambda b: (b, qcol)),
                  pl.BlockSpec((SEQ, NA_WIDTH), lambda b: (b, qcol + 1)),
                  pl.BlockSpec((SEQ, NA_WIDTH), lambda b: (b, qcol + 2))],
        out_specs=pl.BlockSpec((SEQ, NA_WIDTH), lambda b: (b, 0)),
        out_shape=jax.ShapeDtypeStruct((T_CTX, NA_WIDTH), bf16),
        compiler_params=_params(1),
        name="attn_ctx",
    )(p_all, p_all, p_all)


def _attn_lat_kernel(q_ref, k_ref, v_ref, kc_ref, vc_ref, tt_ref, o_ref):
    scale = HEAD_DIM ** -0.5
    kcb = kc_ref[0].astype(bf16)
    vcb = vc_ref[0].astype(bf16)
    nk = WIN_ROWS * GRID_W
    for r in range(GRID_ROWS):
        start = min(max(r - WIN_ROWS // 2, 0), GRID_ROWS - WIN_ROWS)
        a0 = start - r + WIN_ROWS - 1
        q = q_ref[r * GRID_W:(r + 1) * GRID_W, :]
        kw = k_ref[start * GRID_W:start * GRID_W + nk, :]
        vw = v_ref[start * GRID_W:start * GRID_W + nk, :]
        s_nb = _qk(q, kw) * scale + tt_ref[0, :, a0 * GRID_W:a0 * GRID_W + nk]
        s_cx = _qk(q, kcb) * scale
        m = jnp.maximum(jnp.max(s_nb, axis=-1, keepdims=True), jnp.max(s_cx, axis=-1, keepdims=True))
        p_nb = jnp.exp(s_nb - m)
        p_cx = jnp.exp(s_cx - m)
        l = jnp.sum(p_nb, axis=-1, keepdims=True) + jnp.sum(p_cx, axis=-1, keepdims=True)
        o = (jnp.dot(p_nb.astype(bf16), vw, preferred_element_type=f32)
             + jnp.dot(p_cx.astype(bf16), vcb, preferred_element_type=f32)) / l
        o_ref[r * GRID_W:(r + 1) * GRID_W, :] = o.astype(bf16)


def _attn_lat(p_all, ck, cv, tt):
    row0 = T_CTX // DEC_SEQ
    qcol = FOURIER_WIDTH // HEAD_DIM
    return pl.pallas_call(
        _attn_lat_kernel,
        grid=(DEC_BATCH, NA_HEADS),
        in_specs=[pl.BlockSpec((DEC_SEQ, HEAD_DIM), lambda b, h: (row0 + b, qcol + h)),
                  pl.BlockSpec((DEC_SEQ, HEAD_DIM), lambda b, h: (row0 + b, qcol + NA_HEADS + h)),
                  pl.BlockSpec((DEC_SEQ, HEAD_DIM), lambda b, h: (row0 + b, qcol + 2 * NA_HEADS + h)),
                  pl.BlockSpec((1, PAST_LEN, HEAD_DIM), lambda b, h: (b, 0, h)),
                  pl.BlockSpec((1, PAST_LEN, HEAD_DIM), lambda b, h: (b, 0, h)),
                  pl.BlockSpec((1, GRID_W, BIAS_LANES), lambda b, h: (h, 0, 0))],
        out_specs=pl.BlockSpec((DEC_SEQ, HEAD_DIM), lambda b, h: (b, h)),
        out_shape=jax.ShapeDtypeStruct((T_LAT, NA_WIDTH), bf16),
        compiler_params=_params(2),
        name="attn_lat",
    )(p_all, p_all, p_all, ck, cv, tt)


def _dft_consts(n):
    def cs(m):
        idx = (np.arange(m)[:, None] * np.arange(m)[None, :]) % m
        ang = 2.0 * np.pi * idx.astype(np.float64) / m
        return np.cos(ang), np.sin(ang)

    cc, sc = cs(FOURIER_GROUP_DIM)
    cn, sn = cs(n)
    chan = np.concatenate([cc, sc], axis=1) / np.sqrt(FOURIER_GROUP_DIM)
    pos = np.concatenate([cn, -sn], axis=1) / np.sqrt(n)
    return jnp.asarray(chan, f32).astype(bf16), jnp.asarray(pos, f32).astype(bf16)


def _fourier_kernel(u_ref, chan_ref, pos_ref, o_ref):
    gd = FOURIER_GROUP_DIM
    ys = [jnp.dot(u_ref[:, g * gd:(g + 1) * gd], chan_ref[...], preferred_element_type=f32)
          for g in range(FOURIER_GROUPS)]
    yc = jnp.concatenate([y[:, :gd] for y in ys], axis=1)
    ysn = jnp.concatenate([y[:, gd:] for y in ys], axis=1)
    stacked = jnp.concatenate([yc, ysn], axis=0).astype(bf16)
    o_ref[...] = jnp.dot(pos_ref[...], stacked, preferred_element_type=f32).astype(bf16)


def _fourier(p_all, n, n_req, row0):
    chan, pos = _dft_consts(n)
    return pl.pallas_call(
        _fourier_kernel,
        grid=(n_req,),
        in_specs=[pl.BlockSpec((n, FOURIER_WIDTH), lambda b: (row0 + b, 0)),
                  pl.BlockSpec((FOURIER_GROUP_DIM, 2 * FOURIER_GROUP_DIM), lambda b: (0, 0)),
                  pl.BlockSpec((n, 2 * n), lambda b: (0, 0))],
        out_specs=pl.BlockSpec((n, FOURIER_WIDTH), lambda b: (b, 0)),
        out_shape=jax.ShapeDtypeStruct((n_req * n, FOURIER_WIDTH), bf16),
        compiler_params=_params(1),
        name=f"fourier_{n}",
    )(p_all, chan, pos)


MERGE_TM = 256
ROUTE_LANES = LANES


def _class_of(logits):
    lane = lax.broadcasted_iota(jnp.int32, logits.shape, 1)
    big = jnp.int32(ROUTE_LANES)
    is_group = lane < N_GROUPS
    mg = jnp.max(jnp.where(is_group, logits, -jnp.inf), axis=-1, keepdims=True)
    gsel = jnp.min(jnp.where(is_group & (logits == mg), lane, big), axis=-1, keepdims=True)
    lo = N_GROUPS + EXPERTS_PER_GROUP * gsel
    in_group = (lane >= lo) & (lane < lo + EXPERTS_PER_GROUP)
    m1 = jnp.max(jnp.where(in_group, logits, -jnp.inf), axis=-1, keepdims=True)
    i1 = jnp.min(jnp.where(in_group & (logits == m1), lane, big), axis=-1, keepdims=True)
    rest = in_group & (lane != i1)
    m2 = jnp.max(jnp.where(rest, logits, -jnp.inf), axis=-1, keepdims=True)
    i2 = jnp.min(jnp.where(rest & (logits == m2), lane, big), axis=-1, keepdims=True)
    e_lo = jnp.minimum(i1, i2) - lo
    e_hi = jnp.maximum(i1, i2) - lo
    pair = jnp.zeros_like(e_lo)
    for idx in range(N_PAIRS):
        a, b = sorted((PAIR_A[idx], PAIR_B[idx]))
        pair = jnp.where((e_lo == a) & (e_hi == b), idx, pair)
    return gsel * N_PAIRS + pair


def _merge_kernel(fmc_ref, fml_ref, nac_ref, nal_ref, ga_ref, gb_ref, xp_ref, xs_ref, mods_ref, g2_ref,
                  wf_ref, wn_ref, wo_ref, wr_ref, br_ref,
                  x1_ref, h2c_ref, info_ref, cnt_ref, carry_ref):
    i = pl.program_id(0)
    n_ctx = T_CTX // MERGE_TM
    row = _mod_row(i, n_ctx, DEC_SEQ // MERGE_TM)

    @pl.when(i == 0)
    def _():
        carry_ref[...] = jnp.zeros_like(carry_ref)

    def sublayer1(fm_ref, na_ref, x_ref):
        a = jnp.dot(fm_ref[...], wf_ref[...], preferred_element_type=f32)
        b = jnp.dot(na_ref[...], wn_ref[...], preferred_element_type=f32)
        merged = jax.nn.sigmoid(ga_ref[...].astype(f32)) * a + jax.nn.sigmoid(gb_ref[...].astype(f32)) * b
        out = jnp.dot(merged.astype(bf16), wo_ref[...], preferred_element_type=f32)
        x1_ref[...] = x_ref[...] + _mod_vec(mods_ref, row, 2) * out

    @pl.when(i < n_ctx)
    def _():
        sublayer1(fmc_ref, nac_ref, xp_ref)

    @pl.when(i >= n_ctx)
    def _():
        sublayer1(fml_ref, nal_ref, xs_ref)

    h2 = _rms(x1_ref[...]) * g2_ref[...] * (1.0 + _mod_vec(mods_ref, row, 4)) + _mod_vec(mods_ref, row, 3)
    for s in range(CHUNKS):
        h2c_ref[pl.ds(s, MERGE_TM, stride=CHUNKS), :] = h2[:, s * LANES:(s + 1) * LANES]

    logits = jnp.dot(h2.astype(bf16), wr_ref[...], preferred_element_type=f32) + br_ref[...]
    cls = _class_of(logits)
    lane = lax.broadcasted_iota(jnp.int32, logits.shape, 1)
    onehot = (lane == cls).astype(f32)
    r = lax.broadcasted_iota(jnp.int32, (MERGE_TM, MERGE_TM), 0)
    c = lax.broadcasted_iota(jnp.int32, (MERGE_TM, MERGE_TM), 1)
    before = (c < r).astype(bf16)
    prefix = jnp.dot(before, onehot.astype(bf16), preferred_element_type=f32) + carry_ref[...]
    rank = jnp.sum(onehot * prefix, axis=-1, keepdims=True)
    info_ref[...] = jnp.where(lane == 0, cls.astype(f32), jnp.where(lane == 1, rank, 0.0))
    carry_ref[...] = carry_ref[...] + jnp.sum(onehot, axis=0, keepdims=True)
    cnt_ref[...] = jnp.broadcast_to(carry_ref[...], cnt_ref.shape)


def _merge(fm_c, fm_l, na_c, na_l, p_all, xp, xs, mods, g2, wf, wn, wo, wr, br):
    n_ctx = T_CTX // MERGE_TM
    gcol = (FOURIER_WIDTH + 3 * NA_WIDTH) // D_MODEL
    const = lambda i: (0, 0)
    ctx_tile = lambda i: (jnp.minimum(i, n_ctx - 1), 0)
    lat_tile = lambda i: (jnp.maximum(i - n_ctx, 0), 0)
    once = pl.Buffered(1)
    return pl.pallas_call(
        _merge_kernel,
        grid=(T_ALL // MERGE_TM,),
        in_specs=[pl.BlockSpec((MERGE_TM, FOURIER_WIDTH), ctx_tile),
                  pl.BlockSpec((MERGE_TM, FOURIER_WIDTH), lat_tile),
                  pl.BlockSpec((MERGE_TM, NA_WIDTH), ctx_tile),
                  pl.BlockSpec((MERGE_TM, NA_WIDTH), lat_tile),
                  pl.BlockSpec((MERGE_TM, D_MODEL), lambda i: (i, gcol)),
                  pl.BlockSpec((MERGE_TM, D_MODEL), lambda i: (i, gcol + 1)),
                  pl.BlockSpec((MERGE_TM, D_MODEL), ctx_tile),
                  pl.BlockSpec((MERGE_TM, D_MODEL), lat_tile),
                  pl.BlockSpec((MOD_ROWS, N_MOD * D_MODEL), const, pipeline_mode=once),
                  pl.BlockSpec((1, D_MODEL), const, pipeline_mode=once),
                  pl.BlockSpec((FOURIER_WIDTH, D_MODEL), const, pipeline_mode=once),
                  pl.BlockSpec((NA_WIDTH, D_MODEL), const, pipeline_mode=once),
                  pl.BlockSpec((D_MODEL, D_MODEL), const, pipeline_mode=once),
                  pl.BlockSpec((D_MODEL, ROUTE_LANES), const, pipeline_mode=once),
                  pl.BlockSpec((1, ROUTE_LANES), const, pipeline_mode=once)],
        out_specs=[pl.BlockSpec((MERGE_TM, D_MODEL), lambda i: (i, 0)),
                   pl.BlockSpec((MERGE_TM * CHUNKS, LANES), lambda i: (i, 0)),
                   pl.BlockSpec((MERGE_TM, ROUTE_LANES), lambda i: (i, 0)),
                   pl.BlockSpec((8, ROUTE_LANES), const)],
        out_shape=[jax.ShapeDtypeStruct((T_ALL, D_MODEL), f32),
                   jax.ShapeDtypeStruct((T_ALL * CHUNKS, LANES), f32),
                   jax.ShapeDtypeStruct((T_ALL, ROUTE_LANES), f32),
                   jax.ShapeDtypeStruct((8, ROUTE_LANES), f32)],
        scratch_shapes=[pltpu.VMEM((1, ROUTE_LANES), f32)],
        compiler_params=_params(1),
        name="merge",
    )(fm_c, fm_l, na_c, na_l, p_all, p_all, xp, xs, mods, g2, wf, wn, wo, wr, br)


def _perm_kernel(cls_ref, rank_ref, off_ref, pos_ref, src_ref):
    def clear(p, carry):
        src_ref[p] = 0
        return carry

    lax.fori_loop(0, T_PAD, clear, 0, unroll=16)

    def place(t, carry):
        p = off_ref[cls_ref[t]] + rank_ref[t]
        pos_ref[t] = p
        src_ref[p] = t
        return carry

    lax.fori_loop(0, T_ALL, place, 0, unroll=16)


def _permutation(cls, rank, off):
    smem = pl.BlockSpec(memory_space=pltpu.SMEM)
    return pl.pallas_call(
        _perm_kernel,
        in_specs=[smem, smem, smem],
        out_specs=[smem, smem],
        out_shape=[jax.ShapeDtypeStruct((T_ALL,), jnp.int32),
                   jax.ShapeDtypeStruct((T_PAD,), jnp.int32)],
        name="permutation",
    )(cls, rank, off)


ROW_BLOCK = MOE_TILE * CHUNKS


def _start_gather(idx_ref, base, src_hbm, buf, sem, slot):
    def body(k, carry):
        tok = idx_ref[base + k]
        pltpu.make_async_copy(src_hbm.at[pl.ds(pl.multiple_of(tok * CHUNKS, CHUNKS), CHUNKS), :],
                              buf.at[slot, pl.ds(pl.multiple_of(k * CHUNKS, CHUNKS), CHUNKS), :],
                              sem.at[slot]).start()
        return carry

    lax.fori_loop(0, MOE_TILE, body, 0, unroll=8)


def _wait_gather(buf, sem, slot):
    pltpu.make_async_copy(buf.at[slot], buf.at[slot], sem.at[slot]).wait()


def _gathered_rows(buf, slot):
    return jnp.concatenate([buf[slot, pl.ds(s, MOE_TILE, stride=CHUNKS), :] for s in range(CHUNKS)], axis=1)


def _moe_kernel(ea_ref, eb_ref, nt_ref, src_ref, h2c_hbm, wga_ref, wua_ref, wda_ref, wgb_ref, wub_ref, wdb_ref,
                wr_ref, br_ref, o_ref, buf, sem):
    i = pl.program_id(0)
    nt = nt_ref[0]

    @pl.when(i == 0)
    def _():
        _start_gather(src_ref, 0, h2c_hbm, buf, sem, 0)

    @pl.when(i + 1 < nt)
    def _():
        _start_gather(src_ref, (i + 1) * MOE_TILE, h2c_hbm, buf, sem, (i + 1) % 2)

    @pl.when(i >= nt)
    def _():
        o_ref[...] = jnp.zeros_like(o_ref)

    @pl.when(i < nt)
    def _():
        slot = i % 2
        _wait_gather(buf, sem, slot)
        x = _gathered_rows(buf, slot).astype(bf16)
        logits = jnp.dot(x, wr_ref[...], preferred_element_type=f32) + br_ref[...]
        lane = lax.broadcasted_iota(jnp.int32, logits.shape, 1)
        ea = ea_ref[i]
        eb = eb_ref[i]

        def pick(col):
            return jnp.sum(jnp.where(lane == col, logits, 0.0), axis=-1, keepdims=True)

        lg = pick(ea // EXPERTS_PER_GROUP)
        p_group = 1.0 / jnp.sum(jnp.where(lane < N_GROUPS, jnp.exp(logits - lg), 0.0), axis=-1, keepdims=True)
        la = pick(N_GROUPS + ea)
        lb = pick(N_GROUPS + eb)
        w_a = p_group / (1.0 + jnp.exp(lb - la))
        w_b = p_group / (1.0 + jnp.exp(la - lb))

        def expert(wg_ref, wu_ref, wd_ref, w):
            g = jnp.dot(x, wg_ref[0], preferred_element_type=f32)
            u = jnp.dot(x, wu_ref[0], preferred_element_type=f32)
            hid = (g * jax.nn.sigmoid(g) * u).astype(bf16)
            return w * jnp.dot(hid, wd_ref[0], preferred_element_type=f32)

        y = expert(wga_ref, wua_ref, wda_ref, w_a) + expert(wgb_ref, wub_ref, wdb_ref, w_b)
        for s in range(CHUNKS):
            o_ref[pl.ds(s, MOE_TILE, stride=CHUNKS), :] = y[:, s * LANES:(s + 1) * LANES]


def _moe(tile_ea, tile_eb, ntiles, src, h2c, wg, wu, wd, wr, br):
    up = lambda tab: (lambda i, ea, eb, nt, src: (tab(ea, eb)[i], 0, 0))
    pick_a = lambda ea, eb: ea
    pick_b = lambda ea, eb: eb
    const = lambda i, ea, eb, nt, src: (0, 0)
    grid_spec = pltpu.PrefetchScalarGridSpec(
        num_scalar_prefetch=4,
        grid=(MOE_MAX_TILES,),
        in_specs=[pl.BlockSpec(memory_space=pl.ANY),
                  pl.BlockSpec((1, D_MODEL, EXPERT_HIDDEN), up(pick_a)),
                  pl.BlockSpec((1, D_MODEL, EXPERT_HIDDEN), up(pick_a)),
                  pl.BlockSpec((1, EXPERT_HIDDEN, D_MODEL), up(pick_a)),
                  pl.BlockSpec((1, D_MODEL, EXPERT_HIDDEN), up(pick_b)),
                  pl.BlockSpec((1, D_MODEL, EXPERT_HIDDEN), up(pick_b)),
                  pl.BlockSpec((1, EXPERT_HIDDEN, D_MODEL), up(pick_b)),
                  pl.BlockSpec((D_MODEL, ROUTE_LANES), const),
                  pl.BlockSpec((1, ROUTE_LANES), const)],
        out_specs=pl.BlockSpec((ROW_BLOCK, LANES), lambda i, ea, eb, nt, src: (i, 0)),
        scratch_shapes=[pltpu.VMEM((2, ROW_BLOCK, LANES), f32), pltpu.SemaphoreType.DMA((2,))],
    )
    return pl.pallas_call(
        _moe_kernel,
        grid_spec=grid_spec,
        out_shape=jax.ShapeDtypeStruct((T_PAD * CHUNKS, LANES), f32),
        compiler_params=_params(1),
        name="moe",
    )(tile_ea, tile_eb, ntiles, src, h2c, wg, wu, wd, wg, wu, wd, wr, br)


def _final_kernel(pos_ref, ys_hbm, x1_ref, mods_ref, g_ref, o_ref, buf, sem, *, tile0, n_tiles, n_ctx_tiles):
    i = pl.program_id(0)

    @pl.when(i == 0)
    def _():
        _start_gather(pos_ref, tile0 * MOE_TILE, ys_hbm, buf, sem, 0)

    @pl.when(i + 1 < n_tiles)
    def _():
        _start_gather(pos_ref, (tile0 + i + 1) * MOE_TILE, ys_hbm, buf, sem, (i + 1) % 2)

    slot = i % 2
    _wait_gather(buf, sem, slot)
    y = _gathered_rows(buf, slot)
    row = _mod_row(tile0 + i, n_ctx_tiles, DEC_SEQ // MOE_TILE)
    x2 = x1_ref[...] + _mod_vec(mods_ref, row, 5) * y
    o_ref[...] = _rms(x2) * g_ref[...]


def _final(pos, ys, x1, mods, g, tile0, n_tiles):
    body = functools.partial(_final_kernel, tile0=tile0, n_tiles=n_tiles, n_ctx_tiles=T_CTX // MOE_TILE)
    grid_spec = pltpu.PrefetchScalarGridSpec(
        num_scalar_prefetch=1,
        grid=(n_tiles,),
        in_specs=[pl.BlockSpec(memory_space=pl.ANY),
                  pl.BlockSpec((MOE_TILE, D_MODEL), lambda i, pos: (tile0 + i, 0)),
                  pl.BlockSpec((MOD_ROWS, N_MOD * D_MODEL), lambda i, pos: (0, 0)),
                  pl.BlockSpec((1, D_MODEL), lambda i, pos: (0, 0))],
        out_specs=pl.BlockSpec((MOE_TILE, D_MODEL), lambda i, pos: (i, 0)),
        scratch_shapes=[pltpu.VMEM((2, ROW_BLOCK, LANES), f32), pltpu.SemaphoreType.DMA((2,))],
    )
    return pl.pallas_call(
        body,
        grid_spec=grid_spec,
        out_shape=jax.ShapeDtypeStruct((n_tiles * MOE_TILE, D_MODEL), f32),
        compiler_params=_params(1),
        name=f"final_{tile0}",
    )(pos, ys, x1, mods, g)


def kernel(x_prompt, x_sample, cache_k_ctx, cache_v_ctx, c, c_ctx, norm1_g, norm2_g, w_ada, b_ada, w_in, b_in,
           w_fourier, w_na_o, rpb, w_out, w_router_group, b_router_group, w_router_expert, b_router_expert,
           w_exp_gate, w_exp_up, w_exp_down, final_norm_g):
    xp = x_prompt.reshape(T_CTX, D_MODEL)
    xs = x_sample.reshape(T_LAT, D_MODEL)
    c_all = jnp.concatenate([c_ctx[None, :], c, jnp.zeros((MOD_ROWS - 1 - DEC_BATCH, D_MODEL), f32)], axis=0)
    w_route = jnp.concatenate([w_router_group[0], w_router_expert[0]], axis=1)
    w_route = jnp.pad(w_route, ((0, 0), (0, ROUTE_LANES - w_route.shape[1]))).astype(bf16)
    b_route = jnp.concatenate([b_router_group[0], b_router_expert[0]])
    b_route = jnp.pad(b_route, (0, ROUTE_LANES - b_route.shape[0]))[None, :]

    mods = _ada(c_all, w_ada[0], b_ada[0][None, :])
    h1 = _modulate1(xp, xs, mods, norm1_g)
    p_all, k_new, v_new, (wg, wu, wd, wf, wn, wo) = _project(
        h1, w_in[0], b_in,
        (w_exp_gate.reshape(N_EXPERTS * D_MODEL, EXPERT_HIDDEN), w_exp_up.reshape(N_EXPERTS * D_MODEL, EXPERT_HIDDEN),
         w_exp_down.reshape(N_EXPERTS * EXPERT_HIDDEN, D_MODEL), w_fourier[0], w_na_o[0], w_out[0]))
    wg = wg.reshape(N_EXPERTS, D_MODEL, EXPERT_HIDDEN)
    wu = wu.reshape(N_EXPERTS, D_MODEL, EXPERT_HIDDEN)
    wd = wd.reshape(N_EXPERTS, EXPERT_HIDDEN, D_MODEL)

    tt = _bias_table(rpb[0])
    na_c = _attn_ctx(p_all)
    ck = cache_k_ctx.reshape(DEC_BATCH, PAST_LEN, NA_WIDTH)
    cv = cache_v_ctx.reshape(DEC_BATCH, PAST_LEN, NA_WIDTH)
    na_l = _attn_lat(p_all, ck, cv, tt)
    fm_c = _fourier(p_all, SEQ, BATCH, 0)
    fm_l = _fourier(p_all, DEC_SEQ, DEC_BATCH, T_CTX // DEC_SEQ)

    x1, h2c, info, counts = _merge(fm_c, fm_l, na_c, na_l, p_all, xp, xs, mods, norm2_g, wf, wn, wo,
                                   w_route, b_route)

    cls = info[:, 0].astype(jnp.int32)
    rank = info[:, 1].astype(jnp.int32)
    cnt = counts[0, :N_CLASSES].astype(jnp.int32)
    padded = ((cnt + MOE_TILE - 1) // MOE_TILE) * MOE_TILE
    off_end = jnp.cumsum(padded)
    off = off_end - padded
    ntiles = (off_end[-1] // MOE_TILE).astype(jnp.int32)
    tile_start = jnp.arange(MOE_MAX_TILES, dtype=jnp.int32) * MOE_TILE
    last_start = (ntiles - 1) * MOE_TILE
    tile_cls = jnp.sum(jnp.minimum(tile_start, last_start)[:, None] >= off_end[None, :], axis=1)
    pair = tile_cls % N_PAIRS
    group = tile_cls // N_PAIRS
    pair_hot = pair[:, None] == jnp.arange(N_PAIRS)[None, :]
    tile_ea = (group * EXPERTS_PER_GROUP + jnp.sum(pair_hot * jnp.asarray(PAIR_A), axis=1)).astype(jnp.int32)
    tile_eb = (group * EXPERTS_PER_GROUP + jnp.sum(pair_hot * jnp.asarray(PAIR_B), axis=1)).astype(jnp.int32)

    pos, src = _permutation(cls, rank, off.astype(jnp.int32))
    ys = _moe(tile_ea, tile_eb, ntiles[None], src, h2c, wg, wu, wd, w_route, b_route)

    fg = final_norm_g[None, :]
    y_prompt = _final(pos, ys, x1, mods, fg, 0, T_CTX // MOE_TILE)
    y_sample = _final(pos, ys, x1, mods, fg, T_CTX // MOE_TILE, T_LAT // MOE_TILE)

    shape_kv = (BATCH, 1, SEQ, NA_HEADS, HEAD_DIM)
    return (y_prompt.reshape(BATCH, SEQ, D_MODEL), y_sample.reshape(DEC_BATCH, DEC_SEQ, D_MODEL),
            k_new.reshape(shape_kv), v_new.reshape(shape_kv))
```

```python
import functools

import jax
import jax.numpy as jnp
import numpy as np
from jax import lax
from jax.experimental import pallas as pl
from jax.experimental.pallas import tpu as pltpu

D_MODEL = 2048
BATCH = 32
SEQ = 256
DEC_BATCH = 4
DEC_SEQ = 1024
PAST_LEN = 256
GRID_W = 64
GRID_ROWS = DEC_SEQ // GRID_W
NA_HEADS = 8
HEAD_DIM = 128
NA_WIDTH = NA_HEADS * HEAD_DIM
FOURIER_GROUPS = 4
FOURIER_GROUP_DIM = 256
FOURIER_WIDTH = FOURIER_GROUPS * FOURIER_GROUP_DIM
WIN_ROWS = 8
WIN_COLS = 16
N_GROUPS = 4
EXPERTS_PER_GROUP = 4
N_EXPERTS = N_GROUPS * EXPERTS_PER_GROUP
EXPERT_HIDDEN = 512
N_MOD = 6
IN_WIDTH = FOURIER_WIDTH + 3 * NA_WIDTH + 2 * D_MODEL
EPS = 1e-6
NEG_INF = -1e30

T_CTX = BATCH * SEQ
T_LAT = DEC_BATCH * DEC_SEQ
T_ALL = T_CTX + T_LAT

LANES = 128
CHUNKS = D_MODEL // LANES
MOD_ROWS = 8

PAIR_A = (0, 0, 0, 1, 1, 3)
PAIR_B = (1, 2, 3, 3, 2, 2)
N_PAIRS = len(PAIR_A)
N_CLASSES = N_GROUPS * N_PAIRS
MOE_TILE = 256
MOE_MAX_TILES = T_ALL // MOE_TILE + N_CLASSES
T_PAD = MOE_MAX_TILES * MOE_TILE

VMEM_LIMIT = 56 * 1024 * 1024

bf16 = jnp.bfloat16
f32 = jnp.float32


def _params(n_axes, vmem=VMEM_LIMIT):
    return pltpu.CompilerParams(dimension_semantics=("arbitrary",) * n_axes, vmem_limit_bytes=vmem)


def _mod_row(tile, n_ctx_tiles, tiles_per_request):
    return jnp.where(tile < n_ctx_tiles, 0, 1 + (tile - n_ctx_tiles) // tiles_per_request)


def _mod_vec(mods_ref, row, k):
    return mods_ref[pl.ds(row, 1), k * D_MODEL:(k + 1) * D_MODEL]


def _rms(x):
    return x * lax.rsqrt(jnp.mean(x * x, axis=-1, keepdims=True) + EPS)


def _ada_kernel(c_ref, w_ref, b_ref, o_ref):
    c = c_ref[...]
    s = (c * jax.nn.sigmoid(c)).astype(bf16)
    o_ref[...] = jnp.dot(s, w_ref[...].astype(bf16), preferred_element_type=f32) + b_ref[...]


def _ada(c_all, w_ada, b_ada):
    tn = 1024
    n = N_MOD * D_MODEL
    return pl.pallas_call(
        _ada_kernel,
        grid=(n // tn,),
        in_specs=[pl.BlockSpec((MOD_ROWS, D_MODEL), lambda j: (0, 0)),
                  pl.BlockSpec((D_MODEL, tn), lambda j: (0, j)),
                  pl.BlockSpec((1, tn), lambda j: (0, j))],
        out_specs=pl.BlockSpec((MOD_ROWS, tn), lambda j: (0, j)),
        out_shape=jax.ShapeDtypeStruct((MOD_ROWS, n), f32),
        compiler_params=_params(1),
        name="ada",
    )(c_all, w_ada, b_ada)


MOD_TILE = 512


def _mod_kernel(xp_ref, xs_ref, mods_ref, g_ref, o_ref):
    i = pl.program_id(0)
    n_ctx = T_CTX // MOD_TILE
    row = _mod_row(i, n_ctx, DEC_SEQ // MOD_TILE)
    sh = _mod_vec(mods_ref, row, 0)
    sc = _mod_vec(mods_ref, row, 1)

    def run(x_ref):
        h = _rms(x_ref[...]) * g_ref[...] * (1.0 + sc) + sh
        o_ref[...] = h.astype(bf16)

    @pl.when(i < n_ctx)
    def _():
        run(xp_ref)

    @pl.when(i >= n_ctx)
    def _():
        run(xs_ref)


def _modulate1(xp, xs, mods, g):
    n_ctx = T_CTX // MOD_TILE
    return pl.pallas_call(
        _mod_kernel,
        grid=(T_ALL // MOD_TILE,),
        in_specs=[pl.BlockSpec((MOD_TILE, D_MODEL), lambda i: (jnp.minimum(i, n_ctx - 1), 0)),
                  pl.BlockSpec((MOD_TILE, D_MODEL), lambda i: (jnp.maximum(i - n_ctx, 0), 0)),
                  pl.BlockSpec((MOD_ROWS, N_MOD * D_MODEL), lambda i: (0, 0)),
                  pl.BlockSpec((1, D_MODEL), lambda i: (0, 0))],
        out_specs=pl.BlockSpec((MOD_TILE, D_MODEL), lambda i: (i, 0)),
        out_shape=jax.ShapeDtypeStruct((T_ALL, D_MODEL), bf16),
        compiler_params=_params(1),
        name="modulate1",
    )(xp, xs, mods, g)


PROJ_TM = 512
PROJ_TN = 1024
K_COL = (FOURIER_WIDTH + NA_WIDTH) // PROJ_TN
V_COL = (FOURIER_WIDTH + 2 * NA_WIDTH) // PROJ_TN


PROJ_STEPS = (IN_WIDTH // PROJ_TN) * (T_ALL // PROJ_TM)
CAST_SHAPES = ((N_EXPERTS * D_MODEL, EXPERT_HIDDEN, 256),
               (N_EXPERTS * D_MODEL, EXPERT_HIDDEN, 256),
               (N_EXPERTS * EXPERT_HIDDEN, D_MODEL, 64),
               (FOURIER_WIDTH, D_MODEL, 16),
               (NA_WIDTH, D_MODEL, 16),
               (D_MODEL, D_MODEL, 32))
N_CAST = len(CAST_SHAPES)


def _proj_kernel(h_ref, w_ref, b_ref, *rest):
    cast_in = rest[:N_CAST]
    p_ref, k_ref, v_ref = rest[N_CAST:N_CAST + 3]
    cast_out = rest[N_CAST + 3:2 * N_CAST + 3]
    wb_ref = rest[-1]
    j = pl.program_id(0)
    i = pl.program_id(1)
    n_ctx = T_CTX // PROJ_TM
    step = j * (T_ALL // PROJ_TM) + i

    for (rows, _, rb), src, dst in zip(CAST_SHAPES, cast_in, cast_out):
        @pl.when(step < rows // rb)
        def _(src=src, dst=dst):
            dst[...] = src[...].astype(bf16)

    @pl.when(i == 0)
    def _():
        wb_ref[...] = w_ref[...].astype(bf16)

    acc = jnp.dot(h_ref[...], wb_ref[...], preferred_element_type=f32) + b_ref[...]
    p_ref[...] = acc.astype(bf16)

    @pl.when((j == K_COL) & (i < n_ctx))
    def _():
        k_ref[...] = acc

    @pl.when((j == V_COL) & (i < n_ctx))
    def _():
        v_ref[...] = acc


def _kv_index(col):
    n_ctx = T_CTX // PROJ_TM

    def index(j, i):
        during = jnp.minimum(i, n_ctx - 1)
        return (jnp.where(j == col, during, jnp.where(j < col, 0, n_ctx - 1)), 0)

    return index


def _cast_spec(rows, cols, rb):
    m = T_ALL // PROJ_TM
    return pl.BlockSpec((rb, cols), lambda j, i: (jnp.minimum(j * m + i, rows // rb - 1), 0))


def _project(h1, w_in, b_in, cast_weights):
    assert all(rows // rb <= PROJ_STEPS for rows, _, rb in CAST_SHAPES)
    cast_specs = [_cast_spec(*s) for s in CAST_SHAPES]
    outs = pl.pallas_call(
        _proj_kernel,
        grid=(IN_WIDTH // PROJ_TN, T_ALL // PROJ_TM),
        in_specs=[pl.BlockSpec((PROJ_TM, D_MODEL), lambda j, i: (i, 0)),
                  pl.BlockSpec((D_MODEL, PROJ_TN), lambda j, i: (0, j)),
                  pl.BlockSpec((1, PROJ_TN), lambda j, i: (0, j))] + cast_specs,
        out_specs=[pl.BlockSpec((PROJ_TM, PROJ_TN), lambda j, i: (i, j)),
                   pl.BlockSpec((PROJ_TM, PROJ_TN), _kv_index(K_COL)),
                   pl.BlockSpec((PROJ_TM, PROJ_TN), _kv_index(V_COL))] + cast_specs,
        out_shape=[jax.ShapeDtypeStruct((T_ALL, IN_WIDTH), bf16),
                   jax.ShapeDtypeStruct((T_CTX, NA_WIDTH), f32),
                   jax.ShapeDtypeStruct((T_CTX, NA_WIDTH), f32)]
                  + [jax.ShapeDtypeStruct((rows, cols), bf16) for rows, cols, _ in CAST_SHAPES],
        scratch_shapes=[pltpu.VMEM((D_MODEL, PROJ_TN), bf16)],
        compiler_params=_params(2),
        name="project",
    )(h1, w_in, b_in, *cast_weights)
    return outs[0], outs[1], outs[2], outs[3:]


Q_GROUP_ROWS = 4
N_Q_GROUPS = GRID_ROWS // Q_GROUP_ROWS
Q_GROUP = Q_GROUP_ROWS * GRID_W
KEY_SPAN_ROWS = 12
KEY_SPAN = KEY_SPAN_ROWS * GRID_W


def _window_start(r):
    return min(max(r - WIN_ROWS // 2, 0), GRID_ROWS - WIN_ROWS)


def _key_base(g):
    lo = _window_start(g * Q_GROUP_ROWS)
    hi = _window_start((g + 1) * Q_GROUP_ROWS - 1) + WIN_ROWS
    assert hi - lo <= KEY_SPAN_ROWS
    return min(lo, GRID_ROWS - KEY_SPAN_ROWS)


def _block_shift(g, rr):
    return _key_base(g) - (g * Q_GROUP_ROWS + rr) + WIN_ROWS - 1


TABLE_PAD = -min(_block_shift(g, rr) for g in range(N_Q_GROUPS) for rr in range(Q_GROUP_ROWS))
TABLE_BLOCKS = 2 * (-(-(max(_block_shift(g, rr) for g in range(N_Q_GROUPS) for rr in range(Q_GROUP_ROWS))
                        + TABLE_PAD + KEY_SPAN_ROWS + 1) // 2))
TABLE_LANES = TABLE_BLOCKS * GRID_W


def _bias_kernel(r_ref, o_ref):
    qc = lax.broadcasted_iota(jnp.int32, (GRID_W, TABLE_LANES), 0)
    lane = lax.broadcasted_iota(jnp.int32, (GRID_W, TABLE_LANES), 1)
    kc = lane & (GRID_W - 1)
    dc = kc - qc + (WIN_COLS - 1)
    start = jnp.clip(qc - WIN_COLS // 2, 0, GRID_W - WIN_COLS)
    ok = (kc >= start) & (kc < start + WIN_COLS)
    tables = []
    for p in range(2):
        acc = jnp.zeros((GRID_W, TABLE_LANES), f32)
        for b in range(2 * WIN_COLS - 1):
            acc = jnp.where(dc == b, r_ref[0, b, p:p + 1, :], acc)
        tables.append(jnp.where(ok, acc, NEG_INF))
    span_row = jnp.right_shift(lax.broadcasted_iota(jnp.int32, (GRID_W, KEY_SPAN), 1), GRID_W.bit_length() - 1)
    for g in range(N_Q_GROUPS):
        for rr in range(Q_GROUP_ROWS):
            first = _window_start(g * Q_GROUP_ROWS + rr) - _key_base(g)
            e = _block_shift(g, rr) + TABLE_PAD
            lo = (e - e % 2) * GRID_W
            in_window = (span_row >= first) & (span_row < first + WIN_ROWS)
            o_ref[0, g, rr * GRID_W:(rr + 1) * GRID_W, :] = jnp.where(
                in_window, tables[e % 2][:, lo:lo + KEY_SPAN], NEG_INF)


def _bias_table(rpb):
    nb = 2 * WIN_COLS - 1
    by_row = jnp.transpose(rpb, (0, 2, 1))
    by_row = jnp.pad(by_row, ((0, 0), (0, 0), (TABLE_PAD, TABLE_BLOCKS + 1 - TABLE_PAD - by_row.shape[2])))
    rpbx = jnp.stack([jnp.repeat(by_row[:, :, p:p + TABLE_BLOCKS], GRID_W, axis=2) for p in range(2)], axis=2)
    return pl.pallas_call(
        _bias_kernel,
        grid=(NA_HEADS,),
        in_specs=[pl.BlockSpec((1, nb, 2, TABLE_LANES), lambda h: (h, 0, 0, 0))],
        out_specs=pl.BlockSpec((1, N_Q_GROUPS, Q_GROUP, KEY_SPAN), lambda h: (h, 0, 0, 0)),
        out_shape=jax.ShapeDtypeStruct((NA_HEADS, N_Q_GROUPS, Q_GROUP, KEY_SPAN), f32),
        compiler_params=_params(1),
        name="bias_table",
    )(rpbx)


def _qk(q, k):
    return lax.dot_general(q, k, (((1,), (1,)), ((), ())), preferred_element_type=f32)


def _attn_ctx_kernel(q_ref, k_ref, v_ref, o_ref):
    scale = HEAD_DIM ** -0.5
    for h in range(NA_HEADS):
        sl = slice(h * HEAD_DIM, (h + 1) * HEAD_DIM)
        s = _qk(q_ref[:, sl], k_ref[:, sl]) * scale
        p = jnp.exp(s - jnp.max(s, axis=-1, keepdims=True))
        l = jnp.sum(p, axis=-1, keepdims=True)
        o = jnp.dot(p.astype(bf16), v_ref[:, sl], preferred_element_type=f32) / l
        o_ref[:, sl] = o.astype(bf16)


def _attn_ctx(p_all):
    qcol = FOURIER_WIDTH // NA_WIDTH
    return pl.pallas_call(
        _attn_ctx_kernel,
        grid=(BATCH,),
        in_specs=[pl.BlockSpec((SEQ, NA_WIDTH), lambda b: (b, qcol)),
                  pl.BlockSpec((SEQ, NA_WIDTH), lambda b: (b, qcol + 1)),
                  pl.BlockSpec((SEQ, NA_WIDTH), lambda b: (b, qcol + 2))],
        out_specs=pl.BlockSpec((SEQ, NA_WIDTH), lambda b: (b, 0)),
        out_shape=jax.ShapeDtypeStruct((T_CTX, NA_WIDTH), bf16),
        compiler_params=_params(1),
        name="attn_ctx",
    )(p_all, p_all, p_all)


def _attn_lat_kernel(q_ref, k_ref, v_ref, kc_ref, vc_ref, bias_ref, o_ref):
    scale = HEAD_DIM ** -0.5
    kcb = kc_ref[0].astype(bf16)
    vcb = vc_ref[0].astype(bf16)
    for g in range(N_Q_GROUPS):
        k0 = _key_base(g) * GRID_W
        q = q_ref[g * Q_GROUP:(g + 1) * Q_GROUP, :]
        s_nb = _qk(q, k_ref[k0:k0 + KEY_SPAN, :]) * scale + bias_ref[0, g]
        s_cx = _qk(q, kcb) * scale
        m = jnp.maximum(jnp.max(s_nb, axis=-1, keepdims=True), jnp.max(s_cx, axis=-1, keepdims=True))
        p_nb = jnp.exp(s_nb - m)
        p_cx = jnp.exp(s_cx - m)
        l = jnp.sum(p_nb, axis=-1, keepdims=True) + jnp.sum(p_cx, axis=-1, keepdims=True)
        o = (jnp.dot(p_nb.astype(bf16), v_ref[k0:k0 + KEY_SPAN, :], preferred_element_type=f32)
             + jnp.dot(p_cx.astype(bf16), vcb, preferred_element_type=f32)) / l
        o_ref[g * Q_GROUP:(g + 1) * Q_GROUP, :] = o.astype(bf16)


def _attn_lat(p_all, ck, cv, bias):
    row0 = T_CTX // DEC_SEQ
    qcol = FOURIER_WIDTH // HEAD_DIM
    return pl.pallas_call(
        _attn_lat_kernel,
        grid=(NA_HEADS, DEC_BATCH),
        in_specs=[pl.BlockSpec((DEC_SEQ, HEAD_DIM), lambda h, b: (row0 + b, qcol + h)),
                  pl.BlockSpec((DEC_SEQ, HEAD_DIM), lambda h, b: (row0 + b, qcol + NA_HEADS + h)),
                  pl.BlockSpec((DEC_SEQ, HEAD_DIM), lambda h, b: (row0 + b, qcol + 2 * NA_HEADS + h)),
                  pl.BlockSpec((1, PAST_LEN, HEAD_DIM), lambda h, b: (b, 0, h)),
                  pl.BlockSpec((1, PAST_LEN, HEAD_DIM), lambda h, b: (b, 0, h)),
                  pl.BlockSpec((1, N_Q_GROUPS, Q_GROUP, KEY_SPAN), lambda h, b: (h, 0, 0, 0))],
        out_specs=pl.BlockSpec((DEC_SEQ, HEAD_DIM), lambda h, b: (b, h)),
        out_shape=jax.ShapeDtypeStruct((T_LAT, NA_WIDTH), bf16),
        compiler_params=_params(2),
        name="attn_lat",
    )(p_all, p_all, p_all, ck, cv, bias)


def _dft_consts(n):
    def cs(m):
        idx = (np.arange(m)[:, None] * np.arange(m)[None, :]) % m
        ang = 2.0 * np.pi * idx.astype(np.float64) / m
        return np.cos(ang), np.sin(ang)

    cc, sc = cs(FOURIER_GROUP_DIM)
    cn, sn = cs(n)
    chan = np.concatenate([cc, sc], axis=1) / np.sqrt(FOURIER_GROUP_DIM)
    pos = np.concatenate([cn, -sn], axis=1) / np.sqrt(n)
    return jnp.asarray(chan, f32).astype(bf16), jnp.asarray(pos, f32).astype(bf16)


def _fourier_kernel(u_ref, chan_ref, pos_ref, o_ref):
    gd = FOURIER_GROUP_DIM
    ys = [jnp.dot(u_ref[:, g * gd:(g + 1) * gd], chan_ref[...], preferred_element_type=f32)
          for g in range(FOURIER_GROUPS)]
    yc = jnp.concatenate([y[:, :gd] for y in ys], axis=1)
    ysn = jnp.concatenate([y[:, gd:] for y in ys], axis=1)
    stacked = jnp.concatenate([yc, ysn], axis=0).astype(bf16)
    o_ref[...] = jnp.dot(pos_ref[...], stacked, preferred_element_type=f32).astype(bf16)


def _fourier(p_all, n, n_req, row0):
    chan, pos = _dft_consts(n)
    return pl.pallas_call(
        _fourier_kernel,
        grid=(n_req,),
        in_specs=[pl.BlockSpec((n, FOURIER_WIDTH), lambda b: (row0 + b, 0)),
                  pl.BlockSpec((FOURIER_GROUP_DIM, 2 * FOURIER_GROUP_DIM), lambda b: (0, 0)),
                  pl.BlockSpec((n, 2 * n), lambda b: (0, 0))],
        out_specs=pl.BlockSpec((n, FOURIER_WIDTH), lambda b: (b, 0)),
        out_shape=jax.ShapeDtypeStruct((n_req * n, FOURIER_WIDTH), bf16),
        compiler_params=_params(1),
        name=f"fourier_{n}",
    )(p_all, chan, pos)


MERGE_TM = 256
ROUTE_LANES = LANES


def _class_of(logits):
    lane = lax.broadcasted_iota(jnp.int32, logits.shape, 1)
    big = jnp.int32(ROUTE_LANES)
    is_group = lane < N_GROUPS
    mg = jnp.max(jnp.where(is_group, logits, -jnp.inf), axis=-1, keepdims=True)
    gsel = jnp.min(jnp.where(is_group & (logits == mg), lane, big), axis=-1, keepdims=True)
    lo = N_GROUPS + EXPERTS_PER_GROUP * gsel
    in_group = (lane >= lo) & (lane < lo + EXPERTS_PER_GROUP)
    m1 = jnp.max(jnp.where(in_group, logits, -jnp.inf), axis=-1, keepdims=True)
    i1 = jnp.min(jnp.where(in_group & (logits == m1), lane, big), axis=-1, keepdims=True)
    rest = in_group & (lane != i1)
    m2 = jnp.max(jnp.where(rest, logits, -jnp.inf), axis=-1, keepdims=True)
    i2 = jnp.min(jnp.where(rest & (logits == m2), lane, big), axis=-1, keepdims=True)
    e_lo = jnp.minimum(i1, i2) - lo
    e_hi = jnp.maximum(i1, i2) - lo
    pair = jnp.zeros_like(e_lo)
    for idx in range(N_PAIRS):
        a, b = sorted((PAIR_A[idx], PAIR_B[idx]))
        pair = jnp.where((e_lo == a) & (e_hi == b), idx, pair)
    return gsel * N_PAIRS + pair


def _merge_kernel(fmc_ref, fml_ref, nac_ref, nal_ref, ga_ref, gb_ref, xp_ref, xs_ref, mods_ref, g2_ref,
                  wf_ref, wn_ref, wo_ref, wr_ref, br_ref,
                  x1_ref, h2c_ref, info_ref, cnt_ref, carry_ref):
    i = pl.program_id(0)
    n_ctx = T_CTX // MERGE_TM
    row = _mod_row(i, n_ctx, DEC_SEQ // MERGE_TM)

    @pl.when(i == 0)
    def _():
        carry_ref[...] = jnp.zeros_like(carry_ref)

    def sublayer1(fm_ref, na_ref, x_ref):
        a = jnp.dot(fm_ref[...], wf_ref[...], preferred_element_type=f32)
        b = jnp.dot(na_ref[...], wn_ref[...], preferred_element_type=f32)
        merged = jax.nn.sigmoid(ga_ref[...].astype(f32)) * a + jax.nn.sigmoid(gb_ref[...].astype(f32)) * b
        out = jnp.dot(merged.astype(bf16), wo_ref[...], preferred_element_type=f32)
        x1_ref[...] = x_ref[...] + _mod_vec(mods_ref, row, 2) * out

    @pl.when(i < n_ctx)
    def _():
        sublayer1(fmc_ref, nac_ref, xp_ref)

    @pl.when(i >= n_ctx)
    def _():
        sublayer1(fml_ref, nal_ref, xs_ref)

    h2 = _rms(x1_ref[...]) * g2_ref[...] * (1.0 + _mod_vec(mods_ref, row, 4)) + _mod_vec(mods_ref, row, 3)
    for s in range(CHUNKS):
        h2c_ref[pl.ds(s, MERGE_TM, stride=CHUNKS), :] = h2[:, s * LANES:(s + 1) * LANES]

    logits = jnp.dot(h2.astype(bf16), wr_ref[...], preferred_element_type=f32) + br_ref[...]
    cls = _class_of(logits)
    lane = lax.broadcasted_iota(jnp.int32, logits.shape, 1)
    onehot = (lane == cls).astype(f32)
    r = lax.broadcasted_iota(jnp.int32, (MERGE_TM, MERGE_TM), 0)
    c = lax.broadcasted_iota(jnp.int32, (MERGE_TM, MERGE_TM), 1)
    before = (c < r).astype(bf16)
    prefix = jnp.dot(before, onehot.astype(bf16), preferred_element_type=f32) + carry_ref[...]
    rank = jnp.sum(onehot * prefix, axis=-1, keepdims=True)
    info_ref[...] = jnp.where(lane == 0, cls.astype(f32), jnp.where(lane == 1, rank, 0.0))
    carry_ref[...] = carry_ref[...] + jnp.sum(onehot, axis=0, keepdims=True)
    cnt_ref[...] = jnp.broadcast_to(carry_ref[...], cnt_ref.shape)


def _merge(fm_c, fm_l, na_c, na_l, p_all, xp, xs, mods, g2, wf, wn, wo, wr, br):
    n_ctx = T_CTX // MERGE_TM
    gcol = (FOURIER_WIDTH + 3 * NA_WIDTH) // D_MODEL
    const = lambda i: (0, 0)
    ctx_tile = lambda i: (jnp.minimum(i, n_ctx - 1), 0)
    lat_tile = lambda i: (jnp.maximum(i - n_ctx, 0), 0)
    once = pl.Buffered(1)
    return pl.pallas_call(
        _merge_kernel,
        grid=(T_ALL // MERGE_TM,),
        in_specs=[pl.BlockSpec((MERGE_TM, FOURIER_WIDTH), ctx_tile),
                  pl.BlockSpec((MERGE_TM, FOURIER_WIDTH), lat_tile),
                  pl.BlockSpec((MERGE_TM, NA_WIDTH), ctx_tile),
                  pl.BlockSpec((MERGE_TM, NA_WIDTH), lat_tile),
                  pl.BlockSpec((MERGE_TM, D_MODEL), lambda i: (i, gcol)),
                  pl.BlockSpec((MERGE_TM, D_MODEL), lambda i: (i, gcol + 1)),
                  pl.BlockSpec((MERGE_TM, D_MODEL), ctx_tile),
                  pl.BlockSpec((MERGE_TM, D_MODEL), lat_tile),
                  pl.BlockSpec((MOD_ROWS, N_MOD * D_MODEL), const, pipeline_mode=once),
                  pl.BlockSpec((1, D_MODEL), const, pipeline_mode=once),
                  pl.BlockSpec((FOURIER_WIDTH, D_MODEL), const, pipeline_mode=once),
                  pl.BlockSpec((NA_WIDTH, D_MODEL), const, pipeline_mode=once),
                  pl.BlockSpec((D_MODEL, D_MODEL), const, pipeline_mode=once),
                  pl.BlockSpec((D_MODEL, ROUTE_LANES), const, pipeline_mode=once),
                  pl.BlockSpec((1, ROUTE_LANES), const, pipeline_mode=once)],
        out_specs=[pl.BlockSpec((MERGE_TM, D_MODEL), lambda i: (i, 0)),
                   pl.BlockSpec((MERGE_TM * CHUNKS, LANES), lambda i: (i, 0)),
                   pl.BlockSpec((MERGE_TM, ROUTE_LANES), lambda i: (i, 0)),
                   pl.BlockSpec((8, ROUTE_LANES), const)],
        out_shape=[jax.ShapeDtypeStruct((T_ALL, D_MODEL), f32),
                   jax.ShapeDtypeStruct((T_ALL * CHUNKS, LANES), f32),
                   jax.ShapeDtypeStruct((T_ALL, ROUTE_LANES), f32),
                   jax.ShapeDtypeStruct((8, ROUTE_LANES), f32)],
        scratch_shapes=[pltpu.VMEM((1, ROUTE_LANES), f32)],
        compiler_params=_params(1),
        name="merge",
    )(fm_c, fm_l, na_c, na_l, p_all, p_all, xp, xs, mods, g2, wf, wn, wo, wr, br)


def _perm_kernel(cls_ref, rank_ref, off_ref, pos_ref, src_ref):
    def clear(p, carry):
        src_ref[p] = 0
        return carry

    lax.fori_loop(0, T_PAD, clear, 0, unroll=16)

    def place(t, carry):
        p = off_ref[cls_ref[t]] + rank_ref[t]
        pos_ref[t] = p
        src_ref[p] = t
        return carry

    lax.fori_loop(0, T_ALL, place, 0, unroll=16)


def _permutation(cls, rank, off):
    smem = pl.BlockSpec(memory_space=pltpu.SMEM)
    return pl.pallas_call(
        _perm_kernel,
        in_specs=[smem, smem, smem],
        out_specs=[smem, smem],
        out_shape=[jax.ShapeDtypeStruct((T_ALL,), jnp.int32),
                   jax.ShapeDtypeStruct((T_PAD,), jnp.int32)],
        name="permutation",
    )(cls, rank, off)


ROW_BLOCK = MOE_TILE * CHUNKS


def _start_gather(idx_ref, base, src_hbm, buf, sem, slot):
    def body(k, carry):
        tok = idx_ref[base + k]
        pltpu.make_async_copy(src_hbm.at[pl.ds(pl.multiple_of(tok * CHUNKS, CHUNKS), CHUNKS), :],
                              buf.at[slot, pl.ds(pl.multiple_of(k * CHUNKS, CHUNKS), CHUNKS), :],
                              sem.at[slot]).start(priority=1)
        return carry

    lax.fori_loop(0, MOE_TILE, body, 0, unroll=8)


def _wait_gather(buf, sem, slot):
    pltpu.make_async_copy(buf.at[slot], buf.at[slot], sem.at[slot]).wait()


def _gathered_rows(buf, slot):
    return jnp.concatenate([buf[slot, pl.ds(s, MOE_TILE, stride=CHUNKS), :] for s in range(CHUNKS)], axis=1)


def _moe_kernel(ea_ref, eb_ref, nt_ref, src_ref, h2c_hbm, wga_ref, wua_ref, wda_ref, wgb_ref, wub_ref, wdb_ref,
                wr_ref, br_ref, o_ref, buf, sem):
    i = pl.program_id(0)
    nt = nt_ref[0]

    @pl.when(i == 0)
    def _():
        _start_gather(src_ref, 0, h2c_hbm, buf, sem, 0)

    @pl.when(i + 1 < nt)
    def _():
        _start_gather(src_ref, (i + 1) * MOE_TILE, h2c_hbm, buf, sem, (i + 1) % 2)

    @pl.when(i >= nt)
    def _():
        o_ref[...] = jnp.zeros_like(o_ref)

    @pl.when(i < nt)
    def _():
        slot = i % 2
        _wait_gather(buf, sem, slot)
        x = _gathered_rows(buf, slot).astype(bf16)
        logits = jnp.dot(x, wr_ref[...], preferred_element_type=f32) + br_ref[...]
        lane = lax.broadcasted_iota(jnp.int32, logits.shape, 1)
        ea = ea_ref[i]
        eb = eb_ref[i]

        def pick(col):
            return jnp.sum(jnp.where(lane == col, logits, 0.0), axis=-1, keepdims=True)

        lg = pick(ea // EXPERTS_PER_GROUP)
        p_group = 1.0 / jnp.sum(jnp.where(lane < N_GROUPS, jnp.exp(logits - lg), 0.0), axis=-1, keepdims=True)
        la = pick(N_GROUPS + ea)
        lb = pick(N_GROUPS + eb)
        w_a = p_group / (1.0 + jnp.exp(lb - la))
        w_b = p_group / (1.0 + jnp.exp(la - lb))

        def expert(wg_ref, wu_ref, wd_ref, w):
            g = jnp.dot(x, wg_ref[0], preferred_element_type=f32)
            u = jnp.dot(x, wu_ref[0], preferred_element_type=f32)
            hid = (g * jax.nn.sigmoid(g) * u).astype(bf16)
            return w * jnp.dot(hid, wd_ref[0], preferred_element_type=f32)

        y = expert(wga_ref, wua_ref, wda_ref, w_a) + expert(wgb_ref, wub_ref, wdb_ref, w_b)
        for s in range(CHUNKS):
            o_ref[pl.ds(s, MOE_TILE, stride=CHUNKS), :] = y[:, s * LANES:(s + 1) * LANES]


def _moe(tile_ea, tile_eb, ntiles, src, h2c, wg, wu, wd, wr, br):
    up = lambda tab: (lambda i, ea, eb, nt, src: (tab(ea, eb)[i], 0, 0))
    pick_a = lambda ea, eb: ea
    pick_b = lambda ea, eb: eb
    const = lambda i, ea, eb, nt, src: (0, 0)
    grid_spec = pltpu.PrefetchScalarGridSpec(
        num_scalar_prefetch=4,
        grid=(MOE_MAX_TILES,),
        in_specs=[pl.BlockSpec(memory_space=pl.ANY),
                  pl.BlockSpec((1, D_MODEL, EXPERT_HIDDEN), up(pick_a)),
                  pl.BlockSpec((1, D_MODEL, EXPERT_HIDDEN), up(pick_a)),
                  pl.BlockSpec((1, EXPERT_HIDDEN, D_MODEL), up(pick_a)),
                  pl.BlockSpec((1, D_MODEL, EXPERT_HIDDEN), up(pick_b)),
                  pl.BlockSpec((1, D_MODEL, EXPERT_HIDDEN), up(pick_b)),
                  pl.BlockSpec((1, EXPERT_HIDDEN, D_MODEL), up(pick_b)),
                  pl.BlockSpec((D_MODEL, ROUTE_LANES), const),
                  pl.BlockSpec((1, ROUTE_LANES), const)],
        out_specs=pl.BlockSpec((ROW_BLOCK, LANES), lambda i, ea, eb, nt, src: (i, 0)),
        scratch_shapes=[pltpu.VMEM((2, ROW_BLOCK, LANES), f32), pltpu.SemaphoreType.DMA((2,))],
    )
    return pl.pallas_call(
        _moe_kernel,
        grid_spec=grid_spec,
        out_shape=jax.ShapeDtypeStruct((T_PAD * CHUNKS, LANES), f32),
        compiler_params=_params(1),
        name="moe",
    )(tile_ea, tile_eb, ntiles, src, h2c, wg, wu, wd, wg, wu, wd, wr, br)


def _final_kernel(pos_ref, ys_hbm, x1_ref, mods_ref, g_ref, o_ref, buf, sem, *, tile0, n_tiles, n_ctx_tiles):
    i = pl.program_id(0)

    @pl.when(i == 0)
    def _():
        _start_gather(pos_ref, tile0 * MOE_TILE, ys_hbm, buf, sem, 0)

    @pl.when(i + 1 < n_tiles)
    def _():
        _start_gather(pos_ref, (tile0 + i + 1) * MOE_TILE, ys_hbm, buf, sem, (i + 1) % 2)

    slot = i % 2
    _wait_gather(buf, sem, slot)
    y = _gathered_rows(buf, slot)
    row = _mod_row(tile0 + i, n_ctx_tiles, DEC_SEQ // MOE_TILE)
    x2 = x1_ref[...] + _mod_vec(mods_ref, row, 5) * y
    o_ref[...] = _rms(x2) * g_ref[...]


def _final(pos, ys, x1, mods, g, tile0, n_tiles):
    body = functools.partial(_final_kernel, tile0=tile0, n_tiles=n_tiles, n_ctx_tiles=T_CTX // MOE_TILE)
    grid_spec = pltpu.PrefetchScalarGridSpec(
        num_scalar_prefetch=1,
        grid=(n_tiles,),
        in_specs=[pl.BlockSpec(memory_space=pl.ANY),
                  pl.BlockSpec((MOE_TILE, D_MODEL), lambda i, pos: (tile0 + i, 0)),
                  pl.BlockSpec((MOD_ROWS, N_MOD * D_MODEL), lambda i, pos: (0, 0)),
                  pl.BlockSpec((1, D_MODEL), lambda i, pos: (0, 0))],
        out_specs=pl.BlockSpec((MOE_TILE, D_MODEL), lambda i, pos: (i, 0)),
        scratch_shapes=[pltpu.VMEM((2, ROW_BLOCK, LANES), f32), pltpu.SemaphoreType.DMA((2,))],
    )
    return pl.pallas_call(
        body,
        grid_spec=grid_spec,
        out_shape=jax.ShapeDtypeStruct((n_tiles * MOE_TILE, D_MODEL), f32),
        compiler_params=_params(1),
        name=f"final_{tile0}",
    )(pos, ys, x1, mods, g)


def kernel(x_prompt, x_sample, cache_k_ctx, cache_v_ctx, c, c_ctx, norm1_g, norm2_g, w_ada, b_ada, w_in, b_in,
           w_fourier, w_na_o, rpb, w_out, w_router_group, b_router_group, w_router_expert, b_router_expert,
           w_exp_gate, w_exp_up, w_exp_down, final_norm_g):
    xp = x_prompt.reshape(T_CTX, D_MODEL)
    xs = x_sample.reshape(T_LAT, D_MODEL)
    c_all = jnp.concatenate([c_ctx[None, :], c, jnp.zeros((MOD_ROWS - 1 - DEC_BATCH, D_MODEL), f32)], axis=0)
    w_route = jnp.concatenate([w_router_group[0], w_router_expert[0]], axis=1)
    w_route = jnp.pad(w_route, ((0, 0), (0, ROUTE_LANES - w_route.shape[1]))).astype(bf16)
    b_route = jnp.concatenate([b_router_group[0], b_router_expert[0]])
    b_route = jnp.pad(b_route, (0, ROUTE_LANES - b_route.shape[0]))[None, :]

    mods = _ada(c_all, w_ada[0], b_ada[0][None, :])
    h1 = _modulate1(xp, xs, mods, norm1_g)
    p_all, k_new, v_new, (wg, wu, wd, wf, wn, wo) = _project(
        h1, w_in[0], b_in,
        (w_exp_gate.reshape(N_EXPERTS * D_MODEL, EXPERT_HIDDEN), w_exp_up.reshape(N_EXPERTS * D_MODEL, EXPERT_HIDDEN),
         w_exp_down.reshape(N_EXPERTS * EXPERT_HIDDEN, D_MODEL), w_fourier[0], w_na_o[0], w_out[0]))
    wg = wg.reshape(N_EXPERTS, D_MODEL, EXPERT_HIDDEN)
    wu = wu.reshape(N_EXPERTS, D_MODEL, EXPERT_HIDDEN)
    wd = wd.reshape(N_EXPERTS, EXPERT_HIDDEN, D_MODEL)

    tt = _bias_table(rpb[0])
    na_c = _attn_ctx(p_all)
    ck = cache_k_ctx.reshape(DEC_BATCH, PAST_LEN, NA_WIDTH)
    cv = cache_v_ctx.reshape(DEC_BATCH, PAST_LEN, NA_WIDTH)
    na_l = _attn_lat(p_all, ck, cv, tt)
    fm_c = _fourier(p_all, SEQ, BATCH, 0)
    fm_l = _fourier(p_all, DEC_SEQ, DEC_BATCH, T_CTX // DEC_SEQ)

    x1, h2c, info, counts = _merge(fm_c, fm_l, na_c, na_l, p_all, xp, xs, mods, norm2_g, wf, wn, wo,
                                   w_route, b_route)

    cls = info[:, 0].astype(jnp.int32)
    rank = info[:, 1].astype(jnp.int32)
    cnt = counts[0, :N_CLASSES].astype(jnp.int32)
    padded = ((cnt + MOE_TILE - 1) // MOE_TILE) * MOE_TILE
    off_end = jnp.cumsum(padded)
    off = off_end - padded
    ntiles = (off_end[-1] // MOE_TILE).astype(jnp.int32)
    tile_start = jnp.arange(MOE_MAX_TILES, dtype=jnp.int32) * MOE_TILE
    last_start = (ntiles - 1) * MOE_TILE
    tile_cls = jnp.sum(jnp.minimum(tile_start, last_start)[:, None] >= off_end[None, :], axis=1)
    pair = tile_cls % N_PAIRS
    group = tile_cls // N_PAIRS
    pair_hot = pair[:, None] == jnp.arange(N_PAIRS)[None, :]
    tile_ea = (group * EXPERTS_PER_GROUP + jnp.sum(pair_hot * jnp.asarray(PAIR_A), axis=1)).astype(jnp.int32)
    tile_eb = (group * EXPERTS_PER_GROUP + jnp.sum(pair_hot * jnp.asarray(PAIR_B), axis=1)).astype(jnp.int32)

    pos, src = _permutation(cls, rank, off.astype(jnp.int32))
    ys = _moe(tile_ea, tile_eb, ntiles[None], src, h2c, wg, wu, wd, w_route, b_route)

    fg = final_norm_g[None, :]
    y_prompt = _final(pos, ys, x1, mods, fg, 0, T_CTX // MOE_TILE)
    y_sample = _final(pos, ys, x1, mods, fg, T_CTX // MOE_TILE, T_LAT // MOE_TILE)

    shape_kv = (BATCH, 1, SEQ, NA_HEADS, HEAD_DIM)
    return (y_prompt.reshape(BATCH, SEQ, D_MODEL), y_sample.reshape(DEC_BATCH, DEC_SEQ, D_MODEL),
            k_new.reshape(shape_kv), v_new.reshape(shape_kv))
```

```python
import functools

import jax
import jax.numpy as jnp
import numpy as np
from jax import lax
from jax.experimental import pallas as pl
from jax.experimental.pallas import tpu as pltpu

D_MODEL = 2048
BATCH = 32
SEQ = 256
DEC_BATCH = 4
DEC_SEQ = 1024
PAST_LEN = 256
GRID_W = 64
GRID_ROWS = DEC_SEQ // GRID_W
NA_HEADS = 8
HEAD_DIM = 128
NA_WIDTH = NA_HEADS * HEAD_DIM
FOURIER_GROUPS = 4
FOURIER_GROUP_DIM = 256
FOURIER_WIDTH = FOURIER_GROUPS * FOURIER_GROUP_DIM
WIN_ROWS = 8
WIN_COLS = 16
N_GROUPS = 4
EXPERTS_PER_GROUP = 4
N_EXPERTS = N_GROUPS * EXPERTS_PER_GROUP
EXPERT_HIDDEN = 512
N_MOD = 6
IN_WIDTH = FOURIER_WIDTH + 3 * NA_WIDTH + 2 * D_MODEL
EPS = 1e-6
NEG_INF = -1e30

T_CTX = BATCH * SEQ
T_LAT = DEC_BATCH * DEC_SEQ
T_ALL = T_CTX + T_LAT

LANES = 128
CHUNKS = D_MODEL // LANES
MOD_ROWS = 8

PAIR_A = (0, 0, 0, 1, 1, 3)
PAIR_B = (1, 2, 3, 3, 2, 2)
N_PAIRS = len(PAIR_A)
N_CLASSES = N_GROUPS * N_PAIRS
MOE_TILE = 256
MOE_OUT_CHUNK = 512
MOE_MAX_TILES = T_ALL // MOE_TILE + N_CLASSES
T_PAD = MOE_MAX_TILES * MOE_TILE

VMEM_LIMIT = 56 * 1024 * 1024

bf16 = jnp.bfloat16
f32 = jnp.float32


def _params(n_axes, vmem=VMEM_LIMIT):
    return pltpu.CompilerParams(dimension_semantics=("arbitrary",) * n_axes, vmem_limit_bytes=vmem)


def _mod_row(tile, n_ctx_tiles, tiles_per_request):
    return jnp.where(tile < n_ctx_tiles, 0, 1 + (tile - n_ctx_tiles) // tiles_per_request)


def _mod_vec(mods_ref, row, k):
    return mods_ref[pl.ds(row, 1), k * D_MODEL:(k + 1) * D_MODEL]


def _rms(x):
    return x * lax.rsqrt(jnp.mean(x * x, axis=-1, keepdims=True) + EPS)


def _ada_kernel(c_ref, w_ref, b_ref, o_ref):
    c = c_ref[...]
    s = (c * jax.nn.sigmoid(c)).astype(bf16)
    o_ref[...] = jnp.dot(s, w_ref[...].astype(bf16), preferred_element_type=f32) + b_ref[...]


def _ada(c_all, w_ada, b_ada):
    tn = 1024
    n = N_MOD * D_MODEL
    return pl.pallas_call(
        _ada_kernel,
        grid=(n // tn,),
        in_specs=[pl.BlockSpec((MOD_ROWS, D_MODEL), lambda j: (0, 0)),
                  pl.BlockSpec((D_MODEL, tn), lambda j: (0, j)),
                  pl.BlockSpec((1, tn), lambda j: (0, j))],
        out_specs=pl.BlockSpec((MOD_ROWS, tn), lambda j: (0, j)),
        out_shape=jax.ShapeDtypeStruct((MOD_ROWS, n), f32),
        compiler_params=_params(1),
        name="ada",
    )(c_all, w_ada, b_ada)


MOD_TILE = 512


def _mod_kernel(xp_ref, xs_ref, mods_ref, g_ref, o_ref):
    i = pl.program_id(0)
    n_ctx = T_CTX // MOD_TILE
    row = _mod_row(i, n_ctx, DEC_SEQ // MOD_TILE)
    sh = _mod_vec(mods_ref, row, 0)
    sc = _mod_vec(mods_ref, row, 1)

    def run(x_ref):
        h = _rms(x_ref[...]) * g_ref[...] * (1.0 + sc) + sh
        o_ref[...] = h.astype(bf16)

    @pl.when(i < n_ctx)
    def _():
        run(xp_ref)

    @pl.when(i >= n_ctx)
    def _():
        run(xs_ref)


def _modulate1(xp, xs, mods, g):
    n_ctx = T_CTX // MOD_TILE
    return pl.pallas_call(
        _mod_kernel,
        grid=(T_ALL // MOD_TILE,),
        in_specs=[pl.BlockSpec((MOD_TILE, D_MODEL), lambda i: (jnp.minimum(i, n_ctx - 1), 0)),
                  pl.BlockSpec((MOD_TILE, D_MODEL), lambda i: (jnp.maximum(i - n_ctx, 0), 0)),
                  pl.BlockSpec((MOD_ROWS, N_MOD * D_MODEL), lambda i: (0, 0)),
                  pl.BlockSpec((1, D_MODEL), lambda i: (0, 0))],
        out_specs=pl.BlockSpec((MOD_TILE, D_MODEL), lambda i: (i, 0)),
        out_shape=jax.ShapeDtypeStruct((T_ALL, D_MODEL), bf16),
        compiler_params=_params(1),
        name="modulate1",
    )(xp, xs, mods, g)


PROJ_TM = 512
PROJ_TN = 1024
PROJ_CHUNK = 256
K_COL = (FOURIER_WIDTH + NA_WIDTH) // PROJ_TN
V_COL = (FOURIER_WIDTH + 2 * NA_WIDTH) // PROJ_TN


PROJ_STEPS = (IN_WIDTH // PROJ_TN) * (T_ALL // PROJ_TM)
CAST_SHAPES = ((N_EXPERTS * D_MODEL, EXPERT_HIDDEN, 256),
               (N_EXPERTS * D_MODEL, EXPERT_HIDDEN, 256),
               (N_EXPERTS * EXPERT_HIDDEN, D_MODEL, 64),
               (FOURIER_WIDTH, D_MODEL, 16),
               (NA_WIDTH, D_MODEL, 16),
               (D_MODEL, D_MODEL, 32))
N_CAST = len(CAST_SHAPES)


def _proj_kernel(h_ref, w_ref, b_ref, *rest):
    cast_in = rest[:N_CAST]
    p_ref, k_ref, v_ref = rest[N_CAST:N_CAST + 3]
    cast_out = rest[N_CAST + 3:2 * N_CAST + 3]
    wb_ref, acc_ref = rest[-2:]
    j = pl.program_id(0)
    i = pl.program_id(1)
    n_ctx = T_CTX // PROJ_TM
    step = j * (T_ALL // PROJ_TM) + i

    @pl.when(i == 0)
    def _():
        wb_ref[...] = w_ref[...].astype(bf16)

    n_chunks = PROJ_TN // PROJ_CHUNK
    for c in range(n_chunks):
        cols = slice(c * PROJ_CHUNK, (c + 1) * PROJ_CHUNK)
        acc = jnp.dot(h_ref[...], wb_ref[:, cols], preferred_element_type=f32) + b_ref[:, cols]
        p_ref[:, cols] = acc.astype(bf16)
        acc_ref[:, cols] = acc
        for src, dst in list(zip(cast_in, cast_out))[c::n_chunks]:
            dst[...] = src[...].astype(bf16)

    @pl.when((j == K_COL) & (i < n_ctx))
    def _():
        k_ref[...] = acc_ref[...]

    @pl.when((j == V_COL) & (i < n_ctx))
    def _():
        v_ref[...] = acc_ref[...]


def _kv_index(col):
    n_ctx = T_CTX // PROJ_TM

    def index(j, i):
        during = jnp.minimum(i, n_ctx - 1)
        return (jnp.where(j == col, during, jnp.where(j < col, 0, n_ctx - 1)), 0)

    return index


def _cast_spec(rows, cols, rb):
    m = T_ALL // PROJ_TM
    return pl.BlockSpec((rb, cols), lambda j, i: (jnp.minimum(j * m + i, rows // rb - 1), 0))


def _project(h1, w_in, b_in, cast_weights):
    assert all(rows // rb <= PROJ_STEPS for rows, _, rb in CAST_SHAPES)
    cast_specs = [_cast_spec(*s) for s in CAST_SHAPES]
    outs = pl.pallas_call(
        _proj_kernel,
        grid=(IN_WIDTH // PROJ_TN, T_ALL // PROJ_TM),
        in_specs=[pl.BlockSpec((PROJ_TM, D_MODEL), lambda j, i: (i, 0)),
                  pl.BlockSpec((D_MODEL, PROJ_TN), lambda j, i: (0, j)),
                  pl.BlockSpec((1, PROJ_TN), lambda j, i: (0, j))] + cast_specs,
        out_specs=[pl.BlockSpec((PROJ_TM, PROJ_TN), lambda j, i: (i, j)),
                   pl.BlockSpec((PROJ_TM, PROJ_TN), _kv_index(K_COL)),
                   pl.BlockSpec((PROJ_TM, PROJ_TN), _kv_index(V_COL))] + cast_specs,
        out_shape=[jax.ShapeDtypeStruct((T_ALL, IN_WIDTH), bf16),
                   jax.ShapeDtypeStruct((T_CTX, NA_WIDTH), f32),
                   jax.ShapeDtypeStruct((T_CTX, NA_WIDTH), f32)]
                  + [jax.ShapeDtypeStruct((rows, cols), bf16) for rows, cols, _ in CAST_SHAPES],
        scratch_shapes=[pltpu.VMEM((D_MODEL, PROJ_TN), bf16), pltpu.VMEM((PROJ_TM, PROJ_TN), f32)],
        compiler_params=_params(2),
        name="project",
    )(h1, w_in, b_in, *cast_weights)
    return outs[0], outs[1], outs[2], outs[3:]


Q_GROUP_ROWS = 4
N_Q_GROUPS = GRID_ROWS // Q_GROUP_ROWS
Q_GROUP = Q_GROUP_ROWS * GRID_W
KEY_SPAN_ROWS = 12
KEY_SPAN = KEY_SPAN_ROWS * GRID_W


def _window_start(r):
    return min(max(r - WIN_ROWS // 2, 0), GRID_ROWS - WIN_ROWS)


def _key_base(g):
    lo = _window_start(g * Q_GROUP_ROWS)
    hi = _window_start((g + 1) * Q_GROUP_ROWS - 1) + WIN_ROWS
    assert hi - lo <= KEY_SPAN_ROWS
    return min(lo, GRID_ROWS - KEY_SPAN_ROWS)


def _block_shift(g, rr):
    return _key_base(g) - (g * Q_GROUP_ROWS + rr) + WIN_ROWS - 1


TABLE_PAD = -min(_block_shift(g, rr) for g in range(N_Q_GROUPS) for rr in range(Q_GROUP_ROWS))
TABLE_BLOCKS = 2 * (-(-(max(_block_shift(g, rr) for g in range(N_Q_GROUPS) for rr in range(Q_GROUP_ROWS))
                        + TABLE_PAD + KEY_SPAN_ROWS + 1) // 2))
TABLE_LANES = TABLE_BLOCKS * GRID_W


def _bias_kernel(r_ref, o_ref):
    qc = lax.broadcasted_iota(jnp.int32, (GRID_W, TABLE_LANES), 0)
    lane = lax.broadcasted_iota(jnp.int32, (GRID_W, TABLE_LANES), 1)
    kc = lane & (GRID_W - 1)
    dc = kc - qc + (WIN_COLS - 1)
    start = jnp.clip(qc - WIN_COLS // 2, 0, GRID_W - WIN_COLS)
    ok = (kc >= start) & (kc < start + WIN_COLS)
    tables = []
    for p in range(2):
        acc = jnp.zeros((GRID_W, TABLE_LANES), f32)
        for b in range(2 * WIN_COLS - 1):
            acc = jnp.where(dc == b, r_ref[0, b, p:p + 1, :], acc)
        tables.append(jnp.where(ok, acc, NEG_INF))
    span_row = jnp.right_shift(lax.broadcasted_iota(jnp.int32, (GRID_W, KEY_SPAN), 1), GRID_W.bit_length() - 1)
    for g in range(N_Q_GROUPS):
        for rr in range(Q_GROUP_ROWS):
            first = _window_start(g * Q_GROUP_ROWS + rr) - _key_base(g)
            e = _block_shift(g, rr) + TABLE_PAD
            lo = (e - e % 2) * GRID_W
            in_window = (span_row >= first) & (span_row < first + WIN_ROWS)
            o_ref[0, g, rr * GRID_W:(rr + 1) * GRID_W, :] = jnp.where(
                in_window, tables[e % 2][:, lo:lo + KEY_SPAN], NEG_INF)


def _bias_table(rpb):
    nb = 2 * WIN_COLS - 1
    by_row = jnp.transpose(rpb, (0, 2, 1))
    by_row = jnp.pad(by_row, ((0, 0), (0, 0), (TABLE_PAD, TABLE_BLOCKS + 1 - TABLE_PAD - by_row.shape[2])))
    rpbx = jnp.stack([jnp.repeat(by_row[:, :, p:p + TABLE_BLOCKS], GRID_W, axis=2) for p in range(2)], axis=2)
    return pl.pallas_call(
        _bias_kernel,
        grid=(NA_HEADS,),
        in_specs=[pl.BlockSpec((1, nb, 2, TABLE_LANES), lambda h: (h, 0, 0, 0))],
        out_specs=pl.BlockSpec((1, N_Q_GROUPS, Q_GROUP, KEY_SPAN), lambda h: (h, 0, 0, 0)),
        out_shape=jax.ShapeDtypeStruct((NA_HEADS, N_Q_GROUPS, Q_GROUP, KEY_SPAN), f32),
        compiler_params=_params(1),
        name="bias_table",
    )(rpbx)


def _qk(q, k):
    return lax.dot_general(q, k, (((1,), (1,)), ((), ())), preferred_element_type=f32)


def _attn_ctx_kernel(q_ref, k_ref, v_ref, o_ref):
    scale = HEAD_DIM ** -0.5
    for h in range(NA_HEADS):
        sl = slice(h * HEAD_DIM, (h + 1) * HEAD_DIM)
        s = _qk(q_ref[:, sl], k_ref[:, sl]) * scale
        p = jnp.exp(s - jnp.max(s, axis=-1, keepdims=True))
        l = jnp.sum(p, axis=-1, keepdims=True)
        o = jnp.dot(p.astype(bf16), v_ref[:, sl], preferred_element_type=f32) / l
        o_ref[:, sl] = o.astype(bf16)


def _attn_ctx(p_all):
    qcol = FOURIER_WIDTH // NA_WIDTH
    return pl.pallas_call(
        _attn_ctx_kernel,
        grid=(BATCH,),
        in_specs=[pl.BlockSpec((SEQ, NA_WIDTH), lambda b: (b, qcol)),
                  pl.BlockSpec((SEQ, NA_WIDTH), lambda b: (b, qcol + 1)),
                  pl.BlockSpec((SEQ, NA_WIDTH), lambda b: (b, qcol + 2))],
        out_specs=pl.BlockSpec((SEQ, NA_WIDTH), lambda b: (b, 0)),
        out_shape=jax.ShapeDtypeStruct((T_CTX, NA_WIDTH), bf16),
        compiler_params=_params(1),
        name="attn_ctx",
    )(p_all, p_all, p_all)


def _attn_lat_kernel(q_ref, k_ref, v_ref, kc_ref, vc_ref, bias_ref, o_ref):
    scale = HEAD_DIM ** -0.5
    kcb = kc_ref[0].astype(bf16)
    vcb = vc_ref[0].astype(bf16)
    for g in range(N_Q_GROUPS):
        k0 = _key_base(g) * GRID_W
        q = q_ref[g * Q_GROUP:(g + 1) * Q_GROUP, :]
        s_nb = _qk(q, k_ref[k0:k0 + KEY_SPAN, :]) * scale + bias_ref[0, g]
        s_cx = _qk(q, kcb) * scale
        m = jnp.maximum(jnp.max(s_nb, axis=-1, keepdims=True), jnp.max(s_cx, axis=-1, keepdims=True))
        p_nb = jnp.exp(s_nb - m)
        p_cx = jnp.exp(s_cx - m)
        l = jnp.sum(p_nb, axis=-1, keepdims=True) + jnp.sum(p_cx, axis=-1, keepdims=True)
        o = (jnp.dot(p_nb.astype(bf16), v_ref[k0:k0 + KEY_SPAN, :], preferred_element_type=f32)
             + jnp.dot(p_cx.astype(bf16), vcb, preferred_element_type=f32)) / l
        o_ref[g * Q_GROUP:(g + 1) * Q_GROUP, :] = o.astype(bf16)


def _attn_lat(p_all, ck, cv, bias):
    row0 = T_CTX // DEC_SEQ
    qcol = FOURIER_WIDTH // HEAD_DIM
    return pl.pallas_call(
        _attn_lat_kernel,
        grid=(NA_HEADS, DEC_BATCH),
        in_specs=[pl.BlockSpec((DEC_SEQ, HEAD_DIM), lambda h, b: (row0 + b, qcol + h)),
                  pl.BlockSpec((DEC_SEQ, HEAD_DIM), lambda h, b: (row0 + b, qcol + NA_HEADS + h)),
                  pl.BlockSpec((DEC_SEQ, HEAD_DIM), lambda h, b: (row0 + b, qcol + 2 * NA_HEADS + h)),
                  pl.BlockSpec((1, PAST_LEN, HEAD_DIM), lambda h, b: (b, 0, h)),
                  pl.BlockSpec((1, PAST_LEN, HEAD_DIM), lambda h, b: (b, 0, h)),
                  pl.BlockSpec((1, N_Q_GROUPS, Q_GROUP, KEY_SPAN), lambda h, b: (h, 0, 0, 0))],
        out_specs=pl.BlockSpec((DEC_SEQ, HEAD_DIM), lambda h, b: (b, h)),
        out_shape=jax.ShapeDtypeStruct((T_LAT, NA_WIDTH), bf16),
        compiler_params=_params(2),
        name="attn_lat",
    )(p_all, p_all, p_all, ck, cv, bias)


def _dft_consts(n):
    def cs(m):
        idx = (np.arange(m)[:, None] * np.arange(m)[None, :]) % m
        ang = 2.0 * np.pi * idx.astype(np.float64) / m
        return np.cos(ang), np.sin(ang)

    cc, sc = cs(FOURIER_GROUP_DIM)
    cn, sn = cs(n)
    chan = np.concatenate([cc, sc], axis=1) / np.sqrt(FOURIER_GROUP_DIM)
    pos = np.concatenate([cn, -sn], axis=1) / np.sqrt(n)
    return jnp.asarray(chan, f32).astype(bf16), jnp.asarray(pos, f32).astype(bf16)


def _fourier_kernel(u_ref, chan_ref, pos_ref, o_ref):
    gd = FOURIER_GROUP_DIM
    ys = [jnp.dot(u_ref[:, g * gd:(g + 1) * gd], chan_ref[...], preferred_element_type=f32)
          for g in range(FOURIER_GROUPS)]
    yc = jnp.concatenate([y[:, :gd] for y in ys], axis=1)
    ysn = jnp.concatenate([y[:, gd:] for y in ys], axis=1)
    stacked = jnp.concatenate([yc, ysn], axis=0).astype(bf16)
    o_ref[...] = jnp.dot(pos_ref[...], stacked, preferred_element_type=f32).astype(bf16)


def _fourier(p_all, n, n_req, row0):
    chan, pos = _dft_consts(n)
    return pl.pallas_call(
        _fourier_kernel,
        grid=(n_req,),
        in_specs=[pl.BlockSpec((n, FOURIER_WIDTH), lambda b: (row0 + b, 0)),
                  pl.BlockSpec((FOURIER_GROUP_DIM, 2 * FOURIER_GROUP_DIM), lambda b: (0, 0)),
                  pl.BlockSpec((n, 2 * n), lambda b: (0, 0))],
        out_specs=pl.BlockSpec((n, FOURIER_WIDTH), lambda b: (b, 0)),
        out_shape=jax.ShapeDtypeStruct((n_req * n, FOURIER_WIDTH), bf16),
        compiler_params=_params(1),
        name=f"fourier_{n}",
    )(p_all, chan, pos)


MERGE_TM = 256
ROUTE_LANES = LANES


def _class_of(logits):
    lane = lax.broadcasted_iota(jnp.int32, logits.shape, 1)
    big = jnp.int32(ROUTE_LANES)
    is_group = lane < N_GROUPS
    mg = jnp.max(jnp.where(is_group, logits, -jnp.inf), axis=-1, keepdims=True)
    gsel = jnp.min(jnp.where(is_group & (logits == mg), lane, big), axis=-1, keepdims=True)
    lo = N_GROUPS + EXPERTS_PER_GROUP * gsel
    in_group = (lane >= lo) & (lane < lo + EXPERTS_PER_GROUP)
    m1 = jnp.max(jnp.where(in_group, logits, -jnp.inf), axis=-1, keepdims=True)
    i1 = jnp.min(jnp.where(in_group & (logits == m1), lane, big), axis=-1, keepdims=True)
    rest = in_group & (lane != i1)
    m2 = jnp.max(jnp.where(rest, logits, -jnp.inf), axis=-1, keepdims=True)
    i2 = jnp.min(jnp.where(rest & (logits == m2), lane, big), axis=-1, keepdims=True)
    e_lo = jnp.minimum(i1, i2) - lo
    e_hi = jnp.maximum(i1, i2) - lo
    pair = jnp.zeros_like(e_lo)
    for idx in range(N_PAIRS):
        a, b = sorted((PAIR_A[idx], PAIR_B[idx]))
        pair = jnp.where((e_lo == a) & (e_hi == b), idx, pair)
    return gsel * N_PAIRS + pair


def _merge_kernel(fmc_ref, fml_ref, nac_ref, nal_ref, ga_ref, gb_ref, xp_ref, xs_ref, mods_ref, g2_ref,
                  wf_ref, wn_ref, wo_ref, wr_ref, br_ref,
                  x1_ref, h2c_ref, info_ref, cnt_ref, carry_ref, x1s_ref):
    s = pl.program_id(0)
    n_ctx = T_CTX // MERGE_TM
    n_all = T_ALL // MERGE_TM
    per_req = DEC_SEQ // MERGE_TM
    row = _mod_row(jnp.minimum(s, n_all - 1), n_ctx, per_req)
    prev_row = _mod_row(jnp.maximum(s - 1, 0), n_ctx, per_req)

    @pl.when(s == 0)
    def _():
        carry_ref[...] = jnp.zeros_like(carry_ref)
        x1s_ref[1] = jnp.zeros((MERGE_TM, D_MODEL), f32)

    def step(tile_refs):
        x1p = x1s_ref[(s + 1) % 2]
        if tile_refs:
            fm_ref, na_ref, x_ref = tile_refs
            a = jnp.dot(fm_ref[...], wf_ref[...], preferred_element_type=f32)

        h2 = _rms(x1p) * g2_ref[...] * (1.0 + _mod_vec(mods_ref, prev_row, 4)) + _mod_vec(mods_ref, prev_row, 3)
        if tile_refs:
            gate_a = jax.nn.sigmoid(ga_ref[...].astype(f32))
            b = jnp.dot(na_ref[...], wn_ref[...], preferred_element_type=f32)

        for k in range(CHUNKS):
            h2c_ref[pl.ds(k, MERGE_TM, stride=CHUNKS), :] = h2[:, k * LANES:(k + 1) * LANES]
        logits = jnp.dot(h2.astype(bf16), wr_ref[...], preferred_element_type=f32) + br_ref[...]
        if tile_refs:
            merged = gate_a * a + jax.nn.sigmoid(gb_ref[...].astype(f32)) * b
            out = jnp.dot(merged.astype(bf16), wo_ref[...], preferred_element_type=f32)

        cls = _class_of(logits)
        lane = lax.broadcasted_iota(jnp.int32, logits.shape, 1)
        onehot = ((lane == cls) & (s > 0)).astype(f32)
        r = lax.broadcasted_iota(jnp.int32, (MERGE_TM, MERGE_TM), 0)
        c = lax.broadcasted_iota(jnp.int32, (MERGE_TM, MERGE_TM), 1)
        before = (c < r).astype(bf16)
        prefix = jnp.dot(before, onehot.astype(bf16), preferred_element_type=f32) + carry_ref[...]
        rank = jnp.sum(onehot * prefix, axis=-1, keepdims=True)
        info_ref[...] = jnp.where(lane == 0, cls.astype(f32), jnp.where(lane == 1, rank, 0.0))
        carry_ref[...] = carry_ref[...] + jnp.sum(onehot, axis=0, keepdims=True)
        cnt_ref[...] = jnp.broadcast_to(carry_ref[...], cnt_ref.shape)
        if tile_refs:
            x1 = x_ref[...] + _mod_vec(mods_ref, row, 2) * out
            x1_ref[...] = x1
            x1s_ref[s % 2] = x1

    @pl.when(s < n_ctx)
    def _():
        step((fmc_ref, nac_ref, xp_ref))

    @pl.when((s >= n_ctx) & (s < n_all))
    def _():
        step((fml_ref, nal_ref, xs_ref))

    @pl.when(s == n_all)
    def _():
        step(None)


def _merge(fm_c, fm_l, na_c, na_l, p_all, xp, xs, mods, g2, wf, wn, wo, wr, br):
    n_ctx = T_CTX // MERGE_TM
    n_lat = T_LAT // MERGE_TM
    n_all = n_ctx + n_lat
    gcol = (FOURIER_WIDTH + 3 * NA_WIDTH) // D_MODEL
    const = lambda i: (0, 0)
    ctx_tile = lambda i: (jnp.minimum(i, n_ctx - 1), 0)
    lat_tile = lambda i: (jnp.clip(i - n_ctx, 0, n_lat - 1), 0)
    this_tile = lambda i: jnp.minimum(i, n_all - 1)
    prev_tile = lambda i: (jnp.maximum(i - 1, 0), 0)
    once = pl.Buffered(1)
    return pl.pallas_call(
        _merge_kernel,
        grid=(n_all + 1,),
        in_specs=[pl.BlockSpec((MERGE_TM, FOURIER_WIDTH), ctx_tile),
                  pl.BlockSpec((MERGE_TM, FOURIER_WIDTH), lat_tile),
                  pl.BlockSpec((MERGE_TM, NA_WIDTH), ctx_tile),
                  pl.BlockSpec((MERGE_TM, NA_WIDTH), lat_tile),
                  pl.BlockSpec((MERGE_TM, D_MODEL), lambda i: (this_tile(i), gcol)),
                  pl.BlockSpec((MERGE_TM, D_MODEL), lambda i: (this_tile(i), gcol + 1)),
                  pl.BlockSpec((MERGE_TM, D_MODEL), ctx_tile),
                  pl.BlockSpec((MERGE_TM, D_MODEL), lat_tile),
                  pl.BlockSpec((MOD_ROWS, N_MOD * D_MODEL), const, pipeline_mode=once),
                  pl.BlockSpec((1, D_MODEL), const, pipeline_mode=once),
                  pl.BlockSpec((FOURIER_WIDTH, D_MODEL), const, pipeline_mode=once),
                  pl.BlockSpec((NA_WIDTH, D_MODEL), const, pipeline_mode=once),
                  pl.BlockSpec((D_MODEL, D_MODEL), const, pipeline_mode=once),
                  pl.BlockSpec((D_MODEL, ROUTE_LANES), const, pipeline_mode=once),
                  pl.BlockSpec((1, ROUTE_LANES), const, pipeline_mode=once)],
        out_specs=[pl.BlockSpec((MERGE_TM, D_MODEL), lambda i: (this_tile(i), 0)),
                   pl.BlockSpec((MERGE_TM * CHUNKS, LANES), prev_tile),
                   pl.BlockSpec((MERGE_TM, ROUTE_LANES), prev_tile),
                   pl.BlockSpec((8, ROUTE_LANES), const)],
        out_shape=[jax.ShapeDtypeStruct((T_ALL, D_MODEL), f32),
                   jax.ShapeDtypeStruct((T_ALL * CHUNKS, LANES), f32),
                   jax.ShapeDtypeStruct((T_ALL, ROUTE_LANES), f32),
                   jax.ShapeDtypeStruct((8, ROUTE_LANES), f32)],
        scratch_shapes=[pltpu.VMEM((1, ROUTE_LANES), f32), pltpu.VMEM((2, MERGE_TM, D_MODEL), f32)],
        compiler_params=_params(1),
        name="merge",
    )(fm_c, fm_l, na_c, na_l, p_all, p_all, xp, xs, mods, g2, wf, wn, wo, wr, br)


def _perm_kernel(cls_ref, rank_ref, off_ref, pos_ref, src_ref):
    def clear(p, carry):
        src_ref[p] = 0
        return carry

    lax.fori_loop(0, T_PAD, clear, 0, unroll=16)

    def place(t, carry):
        p = off_ref[cls_ref[t]] + rank_ref[t]
        pos_ref[t] = p
        src_ref[p] = t
        return carry

    lax.fori_loop(0, T_ALL, place, 0, unroll=16)


def _permutation(cls, rank, off):
    smem = pl.BlockSpec(memory_space=pltpu.SMEM)
    return pl.pallas_call(
        _perm_kernel,
        in_specs=[smem, smem, smem],
        out_specs=[smem, smem],
        out_shape=[jax.ShapeDtypeStruct((T_ALL,), jnp.int32),
                   jax.ShapeDtypeStruct((T_PAD,), jnp.int32)],
        name="permutation",
    )(cls, rank, off)


ROW_BLOCK = MOE_TILE * CHUNKS


def _row_copy(idx_ref, base, src_hbm, buf, sem, slot, k):
    tok = idx_ref[base + k]
    return pltpu.make_async_copy(src_hbm.at[pl.ds(pl.multiple_of(tok * CHUNKS, CHUNKS), CHUNKS), :],
                                 buf.at[slot, pl.ds(pl.multiple_of(k * CHUNKS, CHUNKS), CHUNKS), :],
                                 sem.at[slot])


def _start_gather(idx_ref, base, src_hbm, buf, sem, slot, part=0, n_parts=1):
    per = MOE_TILE // n_parts

    def body(k8, carry):
        for j in range(8):
            _row_copy(idx_ref, base, src_hbm, buf, sem, slot, k8 * 8 + j).start(priority=j % 2)
        return carry

    lax.fori_loop(part * per // 8, (part + 1) * per // 8, body, 0)


def _wait_gather(buf, sem, slot):
    pltpu.make_async_copy(buf.at[slot], buf.at[slot], sem.at[slot]).wait()


def _gathered_rows(buf, slot):
    return jnp.concatenate([buf[slot, pl.ds(s, MOE_TILE, stride=CHUNKS), :] for s in range(CHUNKS)], axis=1)


def _moe_kernel(ea_ref, eb_ref, nt_ref, src_ref, h2c_hbm, wga_ref, wua_ref, wda_ref, wgb_ref, wub_ref, wdb_ref,
                wr_ref, br_ref, o_ref, buf, sem):
    i = pl.program_id(0)
    nt = nt_ref[0]

    @pl.when(i >= nt)
    def _():
        o_ref[...] = jnp.zeros_like(o_ref)

    @pl.when(i == 0)
    def _():
        _start_gather(src_ref, 0, h2c_hbm, buf, sem, 0)

    def prefetch_next(part):
        @pl.when(i + 1 < nt)
        def _():
            _start_gather(src_ref, (i + 1) * MOE_TILE, h2c_hbm, buf, sem, (i + 1) % 2, part, 2)

    prefetch_next(0)

    @pl.when(i < nt)
    def _():
        slot = i % 2
        _wait_gather(buf, sem, slot)
        x = _gathered_rows(buf, slot).astype(bf16)
        logits = jnp.dot(x, wr_ref[...], preferred_element_type=f32) + br_ref[...]
        lane = lax.broadcasted_iota(jnp.int32, logits.shape, 1)
        ea = ea_ref[i]
        eb = eb_ref[i]

        def pick(col):
            return jnp.sum(jnp.where(lane == col, logits, 0.0), axis=-1, keepdims=True)

        lg = pick(ea // EXPERTS_PER_GROUP)
        p_group = 1.0 / jnp.sum(jnp.where(lane < N_GROUPS, jnp.exp(logits - lg), 0.0), axis=-1, keepdims=True)
        la = pick(N_GROUPS + ea)
        lb = pick(N_GROUPS + eb)
        w_a = p_group / (1.0 + jnp.exp(lb - la))
        w_b = p_group / (1.0 + jnp.exp(la - lb))

        def up(wg_ref, wu_ref):
            return (jnp.dot(x, wg_ref[0], preferred_element_type=f32),
                    jnp.dot(x, wu_ref[0], preferred_element_type=f32))

        def act(g, u):
            return (g * jax.nn.sigmoid(g) * u).astype(bf16)

        g_a, u_a = up(wga_ref, wua_ref)
        g_b, u_b = up(wgb_ref, wub_ref)
        hid_a = act(g_a, u_a)
        hid_b = act(g_b, u_b)
        prefetch_next(1)
        per_chunk = MOE_OUT_CHUNK // LANES
        for c in range(D_MODEL // MOE_OUT_CHUNK):
            cols = slice(c * MOE_OUT_CHUNK, (c + 1) * MOE_OUT_CHUNK)
            y = (w_a * jnp.dot(hid_a, wda_ref[0, :, cols], preferred_element_type=f32)
                 + w_b * jnp.dot(hid_b, wdb_ref[0, :, cols], preferred_element_type=f32))
            for k in range(per_chunk):
                o_ref[pl.ds(c * per_chunk + k, MOE_TILE, stride=CHUNKS), :] = y[:, k * LANES:(k + 1) * LANES]


def _moe(tile_ea, tile_eb, ntiles, src, h2c, wg, wu, wd, wr, br):
    up = lambda tab: (lambda i, ea, eb, nt, src: (tab(ea, eb)[i], 0, 0))
    pick_a = lambda ea, eb: ea
    pick_b = lambda ea, eb: eb
    const = lambda i, ea, eb, nt, src: (0, 0)
    grid_spec = pltpu.PrefetchScalarGridSpec(
        num_scalar_prefetch=4,
        grid=(MOE_MAX_TILES,),
        in_specs=[pl.BlockSpec(memory_space=pl.ANY),
                  pl.BlockSpec((1, D_MODEL, EXPERT_HIDDEN), up(pick_a)),
                  pl.BlockSpec((1, D_MODEL, EXPERT_HIDDEN), up(pick_a)),
                  pl.BlockSpec((1, EXPERT_HIDDEN, D_MODEL), up(pick_a)),
                  pl.BlockSpec((1, D_MODEL, EXPERT_HIDDEN), up(pick_b)),
                  pl.BlockSpec((1, D_MODEL, EXPERT_HIDDEN), up(pick_b)),
                  pl.BlockSpec((1, EXPERT_HIDDEN, D_MODEL), up(pick_b)),
                  pl.BlockSpec((D_MODEL, ROUTE_LANES), const),
                  pl.BlockSpec((1, ROUTE_LANES), const)],
        out_specs=pl.BlockSpec((ROW_BLOCK, LANES), lambda i, ea, eb, nt, src: (i, 0)),
        scratch_shapes=[pltpu.VMEM((2, ROW_BLOCK, LANES), f32), pltpu.SemaphoreType.DMA((2,))],
    )
    return pl.pallas_call(
        _moe_kernel,
        grid_spec=grid_spec,
        out_shape=jax.ShapeDtypeStruct((T_PAD * CHUNKS, LANES), f32),
        compiler_params=_params(1),
        name="moe",
    )(tile_ea, tile_eb, ntiles, src, h2c, wg, wu, wd, wg, wu, wd, wr, br)


def _final_kernel(pos_ref, ys_hbm, x1_ref, mods_ref, g_ref, o_ref, buf, sem, *, tile0, n_tiles, n_ctx_tiles):
    i = pl.program_id(0)

    @pl.when(i == 0)
    def _():
        _start_gather(pos_ref, tile0 * MOE_TILE, ys_hbm, buf, sem, 0)

    @pl.when(i + 1 < n_tiles)
    def _():
        _start_gather(pos_ref, (tile0 + i + 1) * MOE_TILE, ys_hbm, buf, sem, (i + 1) % 2)

    slot = i % 2
    _wait_gather(buf, sem, slot)
    y = _gathered_rows(buf, slot)
    row = _mod_row(tile0 + i, n_ctx_tiles, DEC_SEQ // MOE_TILE)
    x2 = x1_ref[...] + _mod_vec(mods_ref, row, 5) * y
    o_ref[...] = _rms(x2) * g_ref[...]


def _final(pos, ys, x1, mods, g, tile0, n_tiles):
    body = functools.partial(_final_kernel, tile0=tile0, n_tiles=n_tiles, n_ctx_tiles=T_CTX // MOE_TILE)
    grid_spec = pltpu.PrefetchScalarGridSpec(
        num_scalar_prefetch=1,
        grid=(n_tiles,),
        in_specs=[pl.BlockSpec(memory_space=pl.ANY),
                  pl.BlockSpec((MOE_TILE, D_MODEL), lambda i, pos: (tile0 + i, 0)),
                  pl.BlockSpec((MOD_ROWS, N_MOD * D_MODEL), lambda i, pos: (0, 0)),
                  pl.BlockSpec((1, D_MODEL), lambda i, pos: (0, 0))],
        out_specs=pl.BlockSpec((MOE_TILE, D_MODEL), lambda i, pos: (i, 0)),
        scratch_shapes=[pltpu.VMEM((2, ROW_BLOCK, LANES), f32), pltpu.SemaphoreType.DMA((2,))],
    )
    return pl.pallas_call(
        body,
        grid_spec=grid_spec,
        out_shape=jax.ShapeDtypeStruct((n_tiles * MOE_TILE, D_MODEL), f32),
        compiler_params=_params(1),
        name=f"final_{tile0}",
    )(pos, ys, x1, mods, g)


def kernel(x_prompt, x_sample, cache_k_ctx, cache_v_ctx, c, c_ctx, norm1_g, norm2_g, w_ada, b_ada, w_in, b_in,
           w_fourier, w_na_o, rpb, w_out, w_router_group, b_router_group, w_router_expert, b_router_expert,
           w_exp_gate, w_exp_up, w_exp_down, final_norm_g):
    xp = x_prompt.reshape(T_CTX, D_MODEL)
    xs = x_sample.reshape(T_LAT, D_MODEL)
    c_all = jnp.concatenate([c_ctx[None, :], c, jnp.zeros((MOD_ROWS - 1 - DEC_BATCH, D_MODEL), f32)], axis=0)
    w_route = jnp.concatenate([w_router_group[0], w_router_expert[0]], axis=1)
    w_route = jnp.pad(w_route, ((0, 0), (0, ROUTE_LANES - w_route.shape[1]))).astype(bf16)
    b_route = jnp.concatenate([b_router_group[0], b_router_expert[0]])
    b_route = jnp.pad(b_route, (0, ROUTE_LANES - b_route.shape[0]))[None, :]

    mods = _ada(c_all, w_ada[0], b_ada[0][None, :])
    h1 = _modulate1(xp, xs, mods, norm1_g)
    p_all, k_new, v_new, (wg, wu, wd, wf, wn, wo) = _project(
        h1, w_in[0], b_in,
        (w_exp_gate.reshape(N_EXPERTS * D_MODEL, EXPERT_HIDDEN), w_exp_up.reshape(N_EXPERTS * D_MODEL, EXPERT_HIDDEN),
         w_exp_down.reshape(N_EXPERTS * EXPERT_HIDDEN, D_MODEL), w_fourier[0], w_na_o[0], w_out[0]))
    wg = wg.reshape(N_EXPERTS, D_MODEL, EXPERT_HIDDEN)
    wu = wu.reshape(N_EXPERTS, D_MODEL, EXPERT_HIDDEN)
    wd = wd.reshape(N_EXPERTS, EXPERT_HIDDEN, D_MODEL)

    tt = _bias_table(rpb[0])
    na_c = _attn_ctx(p_all)
    ck = cache_k_ctx.reshape(DEC_BATCH, PAST_LEN, NA_WIDTH)
    cv = cache_v_ctx.reshape(DEC_BATCH, PAST_LEN, NA_WIDTH)
    na_l = _attn_lat(p_all, ck, cv, tt)
    fm_c = _fourier(p_all, SEQ, BATCH, 0)
    fm_l = _fourier(p_all, DEC_SEQ, DEC_BATCH, T_CTX // DEC_SEQ)

    x1, h2c, info, counts = _merge(fm_c, fm_l, na_c, na_l, p_all, xp, xs, mods, norm2_g, wf, wn, wo,
                                   w_route, b_route)

    cls = info[:, 0].astype(jnp.int32)
    rank = info[:, 1].astype(jnp.int32)
    cnt = counts[0, :N_CLASSES].astype(jnp.int32)
    padded = ((cnt + MOE_TILE - 1) // MOE_TILE) * MOE_TILE
    off_end = jnp.cumsum(padded)
    off = off_end - padded
    ntiles = (off_end[-1] // MOE_TILE).astype(jnp.int32)
    tile_start = jnp.arange(MOE_MAX_TILES, dtype=jnp.int32) * MOE_TILE
    last_start = (ntiles - 1) * MOE_TILE
    tile_cls = jnp.sum(jnp.minimum(tile_start, last_start)[:, None] >= off_end[None, :], axis=1)
    pair = tile_cls % N_PAIRS
    group = tile_cls // N_PAIRS
    pair_hot = pair[:, None] == jnp.arange(N_PAIRS)[None, :]
    tile_ea = (group * EXPERTS_PER_GROUP + jnp.sum(pair_hot * jnp.asarray(PAIR_A), axis=1)).astype(jnp.int32)
    tile_eb = (group * EXPERTS_PER_GROUP + jnp.sum(pair_hot * jnp.asarray(PAIR_B), axis=1)).astype(jnp.int32)

    pos, src = _permutation(cls, rank, off.astype(jnp.int32))
    ys = _moe(tile_ea, tile_eb, ntiles[None], src, h2c, wg, wu, wd, w_route, b_route)

    fg = final_norm_g[None, :]
    y_prompt = _final(pos, ys, x1, mods, fg, 0, T_CTX // MOE_TILE)
    y_sample = _final(pos, ys, x1, mods, fg, T_CTX // MOE_TILE, T_LAT // MOE_TILE)

    shape_kv = (BATCH, 1, SEQ, NA_HEADS, HEAD_DIM)
    return (y_prompt.reshape(BATCH, SEQ, D_MODEL), y_sample.reshape(DEC_BATCH, DEC_SEQ, D_MODEL),
            k_new.reshape(shape_kv), v_new.reshape(shape_kv))
```

```python
import functools

import jax
import jax.numpy as jnp
import numpy as np
from jax import lax
from jax.experimental import pallas as pl
from jax.experimental.pallas import tpu as pltpu

D_MODEL = 2048
BATCH = 32
SEQ = 256
DEC_BATCH = 4
DEC_SEQ = 1024
PAST_LEN = 256
GRID_W = 64
GRID_ROWS = DEC_SEQ // GRID_W
NA_HEADS = 8
HEAD_DIM = 128
NA_WIDTH = NA_HEADS * HEAD_DIM
FOURIER_GROUPS = 4
FOURIER_GROUP_DIM = 256
FOURIER_WIDTH = FOURIER_GROUPS * FOURIER_GROUP_DIM
WIN_ROWS = 8
WIN_COLS = 16
N_GROUPS = 4
EXPERTS_PER_GROUP = 4
N_EXPERTS = N_GROUPS * EXPERTS_PER_GROUP
EXPERT_HIDDEN = 512
N_MOD = 6
IN_WIDTH = FOURIER_WIDTH + 3 * NA_WIDTH + 2 * D_MODEL
EPS = 1e-6
NEG_INF = -1e30

T_CTX = BATCH * SEQ
T_LAT = DEC_BATCH * DEC_SEQ
T_ALL = T_CTX + T_LAT

LANES = 128
CHUNKS = D_MODEL // LANES
MOD_ROWS = 8

PAIR_A = (0, 0, 0, 1, 1, 3)
PAIR_B = (1, 2, 3, 3, 2, 2)
N_PAIRS = len(PAIR_A)
N_CLASSES = N_GROUPS * N_PAIRS
MOE_TILE = 288
MOE_OUT_CHUNK = 512
MOE_AHEAD = 3
MOE_SLOTS = MOE_AHEAD + 1
MOE_MAX_TILES = -(-T_ALL // MOE_TILE) + N_CLASSES
FINAL_TILE = 256
FINAL_AHEAD = 3
FINAL_SLOTS = FINAL_AHEAD + 1
T_PAD = MOE_MAX_TILES * MOE_TILE

VMEM_LIMIT = 56 * 1024 * 1024

bf16 = jnp.bfloat16
f32 = jnp.float32


def _params(n_axes, vmem=VMEM_LIMIT):
    return pltpu.CompilerParams(dimension_semantics=("arbitrary",) * n_axes, vmem_limit_bytes=vmem)


def _mod_row(tile, n_ctx_tiles, tiles_per_request):
    return jnp.where(tile < n_ctx_tiles, 0, 1 + (tile - n_ctx_tiles) // tiles_per_request)


def _mod_vec(mods_ref, row, k):
    return mods_ref[pl.ds(row, 1), k * D_MODEL:(k + 1) * D_MODEL]


def _rms(x):
    return x * lax.rsqrt(jnp.mean(x * x, axis=-1, keepdims=True) + EPS)


def _ada_kernel(c_ref, w_ref, b_ref, o_ref):
    c = c_ref[...]
    s = (c * jax.nn.sigmoid(c)).astype(bf16)
    o_ref[...] = jnp.dot(s, w_ref[...].astype(bf16), preferred_element_type=f32) + b_ref[...]


def _ada(c_all, w_ada, b_ada):
    tn = 1024
    n = N_MOD * D_MODEL
    return pl.pallas_call(
        _ada_kernel,
        grid=(n // tn,),
        in_specs=[pl.BlockSpec((MOD_ROWS, D_MODEL), lambda j: (0, 0)),
                  pl.BlockSpec((D_MODEL, tn), lambda j: (0, j)),
                  pl.BlockSpec((1, tn), lambda j: (0, j))],
        out_specs=pl.BlockSpec((MOD_ROWS, tn), lambda j: (0, j)),
        out_shape=jax.ShapeDtypeStruct((MOD_ROWS, n), f32),
        compiler_params=_params(1),
        name="ada",
    )(c_all, w_ada, b_ada)


MOD_TILE = 512


def _mod_kernel(xp_ref, xs_ref, mods_ref, g_ref, o_ref):
    i = pl.program_id(0)
    n_ctx = T_CTX // MOD_TILE
    row = _mod_row(i, n_ctx, DEC_SEQ // MOD_TILE)
    sh = _mod_vec(mods_ref, row, 0)
    sc = _mod_vec(mods_ref, row, 1)

    def run(x_ref):
        h = _rms(x_ref[...]) * g_ref[...] * (1.0 + sc) + sh
        o_ref[...] = h.astype(bf16)

    @pl.when(i < n_ctx)
    def _():
        run(xp_ref)

    @pl.when(i >= n_ctx)
    def _():
        run(xs_ref)


def _modulate1(xp, xs, mods, g):
    n_ctx = T_CTX // MOD_TILE
    return pl.pallas_call(
        _mod_kernel,
        grid=(T_ALL // MOD_TILE,),
        in_specs=[pl.BlockSpec((MOD_TILE, D_MODEL), lambda i: (jnp.minimum(i, n_ctx - 1), 0)),
                  pl.BlockSpec((MOD_TILE, D_MODEL), lambda i: (jnp.maximum(i - n_ctx, 0), 0)),
                  pl.BlockSpec((MOD_ROWS, N_MOD * D_MODEL), lambda i: (0, 0)),
                  pl.BlockSpec((1, D_MODEL), lambda i: (0, 0))],
        out_specs=pl.BlockSpec((MOD_TILE, D_MODEL), lambda i: (i, 0)),
        out_shape=jax.ShapeDtypeStruct((T_ALL, D_MODEL), bf16),
        compiler_params=_params(1),
        name="modulate1",
    )(xp, xs, mods, g)


PROJ_TM = 512
PROJ_TN = 1024
PROJ_CHUNK = 256
K_COL = (FOURIER_WIDTH + NA_WIDTH) // PROJ_TN
V_COL = (FOURIER_WIDTH + 2 * NA_WIDTH) // PROJ_TN


PROJ_STEPS = (IN_WIDTH // PROJ_TN) * (T_ALL // PROJ_TM)
CAST_SHAPES = ((N_EXPERTS * D_MODEL, EXPERT_HIDDEN, 256),
               (N_EXPERTS * D_MODEL, EXPERT_HIDDEN, 256),
               (N_EXPERTS * EXPERT_HIDDEN, D_MODEL, 64),
               (FOURIER_WIDTH, D_MODEL, 16),
               (NA_WIDTH, D_MODEL, 16),
               (D_MODEL, D_MODEL, 32))
N_CAST = len(CAST_SHAPES)


def _proj_kernel(h_ref, w_ref, b_ref, *rest):
    cast_in = rest[:N_CAST]
    p_ref, k_ref, v_ref = rest[N_CAST:N_CAST + 3]
    cast_out = rest[N_CAST + 3:2 * N_CAST + 3]
    wb_ref, acc_ref = rest[-2:]
    j = pl.program_id(0)
    i = pl.program_id(1)
    n_ctx = T_CTX // PROJ_TM
    step = j * (T_ALL // PROJ_TM) + i

    @pl.when(i == 0)
    def _():
        wb_ref[...] = w_ref[...].astype(bf16)

    n_chunks = PROJ_TN // PROJ_CHUNK
    for c in range(n_chunks):
        cols = slice(c * PROJ_CHUNK, (c + 1) * PROJ_CHUNK)
        acc = jnp.dot(h_ref[...], wb_ref[:, cols], preferred_element_type=f32) + b_ref[:, cols]
        p_ref[:, cols] = acc.astype(bf16)
        acc_ref[:, cols] = acc
        for src, dst in list(zip(cast_in, cast_out))[c::n_chunks]:
            dst[...] = src[...].astype(bf16)

    @pl.when((j == K_COL) & (i < n_ctx))
    def _():
        k_ref[...] = acc_ref[...]

    @pl.when((j == V_COL) & (i < n_ctx))
    def _():
        v_ref[...] = acc_ref[...]


def _kv_index(col):
    n_ctx = T_CTX // PROJ_TM

    def index(j, i):
        during = jnp.minimum(i, n_ctx - 1)
        return (jnp.where(j == col, during, jnp.where(j < col, 0, n_ctx - 1)), 0)

    return index


def _cast_spec(rows, cols, rb):
    m = T_ALL // PROJ_TM
    return pl.BlockSpec((rb, cols), lambda j, i: (jnp.minimum(j * m + i, rows // rb - 1), 0))


def _project(h1, w_in, b_in, cast_weights):
    assert all(rows // rb <= PROJ_STEPS for rows, _, rb in CAST_SHAPES)
    cast_specs = [_cast_spec(*s) for s in CAST_SHAPES]
    outs = pl.pallas_call(
        _proj_kernel,
        grid=(IN_WIDTH // PROJ_TN, T_ALL // PROJ_TM),
        in_specs=[pl.BlockSpec((PROJ_TM, D_MODEL), lambda j, i: (i, 0)),
                  pl.BlockSpec((D_MODEL, PROJ_TN), lambda j, i: (0, j)),
                  pl.BlockSpec((1, PROJ_TN), lambda j, i: (0, j))] + cast_specs,
        out_specs=[pl.BlockSpec((PROJ_TM, PROJ_TN), lambda j, i: (i, j)),
                   pl.BlockSpec((PROJ_TM, PROJ_TN), _kv_index(K_COL)),
                   pl.BlockSpec((PROJ_TM, PROJ_TN), _kv_index(V_COL))] + cast_specs,
        out_shape=[jax.ShapeDtypeStruct((T_ALL, IN_WIDTH), bf16),
                   jax.ShapeDtypeStruct((T_CTX, NA_WIDTH), f32),
                   jax.ShapeDtypeStruct((T_CTX, NA_WIDTH), f32)]
                  + [jax.ShapeDtypeStruct((rows, cols), bf16) for rows, cols, _ in CAST_SHAPES],
        scratch_shapes=[pltpu.VMEM((D_MODEL, PROJ_TN), bf16), pltpu.VMEM((PROJ_TM, PROJ_TN), f32)],
        compiler_params=_params(2),
        name="project",
    )(h1, w_in, b_in, *cast_weights)
    return outs[0], outs[1], outs[2], outs[3:]


Q_GROUP_ROWS = 4
N_Q_GROUPS = GRID_ROWS // Q_GROUP_ROWS
Q_GROUP = Q_GROUP_ROWS * GRID_W
KEY_SPAN_ROWS = 12
KEY_SPAN = KEY_SPAN_ROWS * GRID_W


def _window_start(r):
    return min(max(r - WIN_ROWS // 2, 0), GRID_ROWS - WIN_ROWS)


def _key_base(g):
    lo = _window_start(g * Q_GROUP_ROWS)
    hi = _window_start((g + 1) * Q_GROUP_ROWS - 1) + WIN_ROWS
    assert hi - lo <= KEY_SPAN_ROWS
    return min(lo, GRID_ROWS - KEY_SPAN_ROWS)


def _block_shift(g, rr):
    return _key_base(g) - (g * Q_GROUP_ROWS + rr) + WIN_ROWS - 1


TABLE_PAD = -min(_block_shift(g, rr) for g in range(N_Q_GROUPS) for rr in range(Q_GROUP_ROWS))
TABLE_BLOCKS = 2 * (-(-(max(_block_shift(g, rr) for g in range(N_Q_GROUPS) for rr in range(Q_GROUP_ROWS))
                        + TABLE_PAD + KEY_SPAN_ROWS + 1) // 2))
TABLE_LANES = TABLE_BLOCKS * GRID_W


def _bias_kernel(r_ref, o_ref):
    qc = lax.broadcasted_iota(jnp.int32, (GRID_W, TABLE_LANES), 0)
    lane = lax.broadcasted_iota(jnp.int32, (GRID_W, TABLE_LANES), 1)
    kc = lane & (GRID_W - 1)
    dc = kc - qc + (WIN_COLS - 1)
    start = jnp.clip(qc - WIN_COLS // 2, 0, GRID_W - WIN_COLS)
    ok = (kc >= start) & (kc < start + WIN_COLS)
    tables = []
    for p in range(2):
        acc = jnp.zeros((GRID_W, TABLE_LANES), f32)
        for b in range(2 * WIN_COLS - 1):
            acc = jnp.where(dc == b, r_ref[0, b, p:p + 1, :], acc)
        tables.append(jnp.where(ok, acc, NEG_INF))
    span_row = jnp.right_shift(lax.broadcasted_iota(jnp.int32, (GRID_W, KEY_SPAN), 1), GRID_W.bit_length() - 1)
    for g in range(N_Q_GROUPS):
        for rr in range(Q_GROUP_ROWS):
            first = _window_start(g * Q_GROUP_ROWS + rr) - _key_base(g)
            e = _block_shift(g, rr) + TABLE_PAD
            lo = (e - e % 2) * GRID_W
            in_window = (span_row >= first) & (span_row < first + WIN_ROWS)
            o_ref[0, g, rr * GRID_W:(rr + 1) * GRID_W, :] = jnp.where(
                in_window, tables[e % 2][:, lo:lo + KEY_SPAN], NEG_INF)


def _bias_table(rpb):
    nb = 2 * WIN_COLS - 1
    by_row = jnp.transpose(rpb, (0, 2, 1))
    by_row = jnp.pad(by_row, ((0, 0), (0, 0), (TABLE_PAD, TABLE_BLOCKS + 1 - TABLE_PAD - by_row.shape[2])))
    rpbx = jnp.stack([jnp.repeat(by_row[:, :, p:p + TABLE_BLOCKS], GRID_W, axis=2) for p in range(2)], axis=2)
    return pl.pallas_call(
        _bias_kernel,
        grid=(NA_HEADS,),
        in_specs=[pl.BlockSpec((1, nb, 2, TABLE_LANES), lambda h: (h, 0, 0, 0))],
        out_specs=pl.BlockSpec((1, N_Q_GROUPS, Q_GROUP, KEY_SPAN), lambda h: (h, 0, 0, 0)),
        out_shape=jax.ShapeDtypeStruct((NA_HEADS, N_Q_GROUPS, Q_GROUP, KEY_SPAN), f32),
        compiler_params=_params(1),
        name="bias_table",
    )(rpbx)


def _qk(q, k):
    return lax.dot_general(q, k, (((1,), (1,)), ((), ())), preferred_element_type=f32)


def _attn_ctx_kernel(q_ref, k_ref, v_ref, o_ref):
    scale = HEAD_DIM ** -0.5
    for h in range(NA_HEADS):
        sl = slice(h * HEAD_DIM, (h + 1) * HEAD_DIM)
        s = _qk(q_ref[:, sl], k_ref[:, sl]) * scale
        p = jnp.exp(s - jnp.max(s, axis=-1, keepdims=True))
        l = jnp.sum(p, axis=-1, keepdims=True)
        o = jnp.dot(p.astype(bf16), v_ref[:, sl], preferred_element_type=f32) / l
        o_ref[:, sl] = o.astype(bf16)


def _attn_ctx(p_all):
    qcol = FOURIER_WIDTH // NA_WIDTH
    return pl.pallas_call(
        _attn_ctx_kernel,
        grid=(BATCH,),
        in_specs=[pl.BlockSpec((SEQ, NA_WIDTH), lambda b: (b, qcol)),
                  pl.BlockSpec((SEQ, NA_WIDTH), lambda b: (b, qcol + 1)),
                  pl.BlockSpec((SEQ, NA_WIDTH), lambda b: (b, qcol + 2))],
        out_specs=pl.BlockSpec((SEQ, NA_WIDTH), lambda b: (b, 0)),
        out_shape=jax.ShapeDtypeStruct((T_CTX, NA_WIDTH), bf16),
        compiler_params=_params(1),
        name="attn_ctx",
    )(p_all, p_all, p_all)


def _attn_lat_kernel(q_ref, k_ref, v_ref, kc_ref, vc_ref, bias_ref, o_ref):
    scale = HEAD_DIM ** -0.5
    kcb = kc_ref[0].astype(bf16)
    vcb = vc_ref[0].astype(bf16)
    for g in range(N_Q_GROUPS):
        k0 = _key_base(g) * GRID_W
        q = q_ref[g * Q_GROUP:(g + 1) * Q_GROUP, :]
        s_nb = _qk(q, k_ref[k0:k0 + KEY_SPAN, :]) * scale + bias_ref[0, g]
        s_cx = _qk(q, kcb) * scale
        m = jnp.maximum(jnp.max(s_nb, axis=-1, keepdims=True), jnp.max(s_cx, axis=-1, keepdims=True))
        p_nb = jnp.exp(s_nb - m)
        p_cx = jnp.exp(s_cx - m)
        l = jnp.sum(p_nb, axis=-1, keepdims=True) + jnp.sum(p_cx, axis=-1, keepdims=True)
        o = (jnp.dot(p_nb.astype(bf16), v_ref[k0:k0 + KEY_SPAN, :], preferred_element_type=f32)
             + jnp.dot(p_cx.astype(bf16), vcb, preferred_element_type=f32)) / l
        o_ref[g * Q_GROUP:(g + 1) * Q_GROUP, :] = o.astype(bf16)


def _attn_lat(p_all, ck, cv, bias):
    row0 = T_CTX // DEC_SEQ
    qcol = FOURIER_WIDTH // HEAD_DIM
    return pl.pallas_call(
        _attn_lat_kernel,
        grid=(NA_HEADS, DEC_BATCH),
        in_specs=[pl.BlockSpec((DEC_SEQ, HEAD_DIM), lambda h, b: (row0 + b, qcol + h)),
                  pl.BlockSpec((DEC_SEQ, HEAD_DIM), lambda h, b: (row0 + b, qcol + NA_HEADS + h)),
                  pl.BlockSpec((DEC_SEQ, HEAD_DIM), lambda h, b: (row0 + b, qcol + 2 * NA_HEADS + h)),
                  pl.BlockSpec((1, PAST_LEN, HEAD_DIM), lambda h, b: (b, 0, h)),
                  pl.BlockSpec((1, PAST_LEN, HEAD_DIM), lambda h, b: (b, 0, h)),
                  pl.BlockSpec((1, N_Q_GROUPS, Q_GROUP, KEY_SPAN), lambda h, b: (h, 0, 0, 0))],
        out_specs=pl.BlockSpec((DEC_SEQ, HEAD_DIM), lambda h, b: (b, h)),
        out_shape=jax.ShapeDtypeStruct((T_LAT, NA_WIDTH), bf16),
        compiler_params=_params(2),
        name="attn_lat",
    )(p_all, p_all, p_all, ck, cv, bias)


def _dft_consts(n):
    def cs(m):
        idx = (np.arange(m)[:, None] * np.arange(m)[None, :]) % m
        ang = 2.0 * np.pi * idx.astype(np.float64) / m
        return np.cos(ang), np.sin(ang)

    cc, sc = cs(FOURIER_GROUP_DIM)
    cn, sn = cs(n)
    chan = np.concatenate([cc, sc], axis=1) / np.sqrt(FOURIER_GROUP_DIM)
    pos = np.concatenate([cn, -sn], axis=1) / np.sqrt(n)
    return jnp.asarray(chan, f32).astype(bf16), jnp.asarray(pos, f32).astype(bf16)


def _fourier_kernel(u_ref, chan_ref, pos_ref, o_ref):
    gd = FOURIER_GROUP_DIM
    ys = [jnp.dot(u_ref[:, g * gd:(g + 1) * gd], chan_ref[...], preferred_element_type=f32)
          for g in range(FOURIER_GROUPS)]
    yc = jnp.concatenate([y[:, :gd] for y in ys], axis=1)
    ysn = jnp.concatenate([y[:, gd:] for y in ys], axis=1)
    stacked = jnp.concatenate([yc, ysn], axis=0).astype(bf16)
    o_ref[...] = jnp.dot(pos_ref[...], stacked, preferred_element_type=f32).astype(bf16)


def _fourier(p_all, n, n_req, row0):
    chan, pos = _dft_consts(n)
    return pl.pallas_call(
        _fourier_kernel,
        grid=(n_req,),
        in_specs=[pl.BlockSpec((n, FOURIER_WIDTH), lambda b: (row0 + b, 0)),
                  pl.BlockSpec((FOURIER_GROUP_DIM, 2 * FOURIER_GROUP_DIM), lambda b: (0, 0)),
                  pl.BlockSpec((n, 2 * n), lambda b: (0, 0))],
        out_specs=pl.BlockSpec((n, FOURIER_WIDTH), lambda b: (b, 0)),
        out_shape=jax.ShapeDtypeStruct((n_req * n, FOURIER_WIDTH), bf16),
        compiler_params=_params(1),
        name=f"fourier_{n}",
    )(p_all, chan, pos)


MERGE_TM = 256
ROUTE_LANES = LANES


def _class_of(logits):
    lane = lax.broadcasted_iota(jnp.int32, logits.shape, 1)
    big = jnp.int32(ROUTE_LANES)
    is_group = lane < N_GROUPS
    mg = jnp.max(jnp.where(is_group, logits, -jnp.inf), axis=-1, keepdims=True)
    gsel = jnp.min(jnp.where(is_group & (logits == mg), lane, big), axis=-1, keepdims=True)
    lo = N_GROUPS + EXPERTS_PER_GROUP * gsel
    in_group = (lane >= lo) & (lane < lo + EXPERTS_PER_GROUP)
    m1 = jnp.max(jnp.where(in_group, logits, -jnp.inf), axis=-1, keepdims=True)
    i1 = jnp.min(jnp.where(in_group & (logits == m1), lane, big), axis=-1, keepdims=True)
    rest = in_group & (lane != i1)
    m2 = jnp.max(jnp.where(rest, logits, -jnp.inf), axis=-1, keepdims=True)
    i2 = jnp.min(jnp.where(rest & (logits == m2), lane, big), axis=-1, keepdims=True)
    e_lo = jnp.minimum(i1, i2) - lo
    e_hi = jnp.maximum(i1, i2) - lo
    pair = jnp.zeros_like(e_lo)
    for idx in range(N_PAIRS):
        a, b = sorted((PAIR_A[idx], PAIR_B[idx]))
        pair = jnp.where((e_lo == a) & (e_hi == b), idx, pair)
    return gsel * N_PAIRS + pair


def _merge_kernel(fmc_ref, fml_ref, nac_ref, nal_ref, ga_ref, gb_ref, xp_ref, xs_ref, mods_ref, g2_ref,
                  wf_ref, wn_ref, wo_ref, wr_ref, br_ref,
                  x1_ref, h2c_ref, info_ref, cnt_ref, carry_ref, x1s_ref):
    s = pl.program_id(0)
    n_ctx = T_CTX // MERGE_TM
    n_all = T_ALL // MERGE_TM
    per_req = DEC_SEQ // MERGE_TM
    row = _mod_row(jnp.minimum(s, n_all - 1), n_ctx, per_req)
    prev_row = _mod_row(jnp.maximum(s - 1, 0), n_ctx, per_req)

    @pl.when(s == 0)
    def _():
        carry_ref[...] = jnp.zeros_like(carry_ref)
        x1s_ref[1] = jnp.zeros((MERGE_TM, D_MODEL), f32)

    def step(tile_refs):
        x1p = x1s_ref[(s + 1) % 2]
        if tile_refs:
            fm_ref, na_ref, x_ref = tile_refs
            a = jnp.dot(fm_ref[...], wf_ref[...], preferred_element_type=f32)

        h2 = _rms(x1p) * g2_ref[...] * (1.0 + _mod_vec(mods_ref, prev_row, 4)) + _mod_vec(mods_ref, prev_row, 3)
        if tile_refs:
            gate_a = jax.nn.sigmoid(ga_ref[...].astype(f32))
            b = jnp.dot(na_ref[...], wn_ref[...], preferred_element_type=f32)

        for k in range(CHUNKS):
            h2c_ref[pl.ds(k, MERGE_TM, stride=CHUNKS), :] = h2[:, k * LANES:(k + 1) * LANES]
        logits = jnp.dot(h2.astype(bf16), wr_ref[...], preferred_element_type=f32) + br_ref[...]
        if tile_refs:
            merged = gate_a * a + jax.nn.sigmoid(gb_ref[...].astype(f32)) * b
            out = jnp.dot(merged.astype(bf16), wo_ref[...], preferred_element_type=f32)

        cls = _class_of(logits)
        lane = lax.broadcasted_iota(jnp.int32, logits.shape, 1)
        onehot = ((lane == cls) & (s > 0)).astype(f32)
        r = lax.broadcasted_iota(jnp.int32, (MERGE_TM, MERGE_TM), 0)
        c = lax.broadcasted_iota(jnp.int32, (MERGE_TM, MERGE_TM), 1)
        before = (c < r).astype(bf16)
        prefix = jnp.dot(before, onehot.astype(bf16), preferred_element_type=f32) + carry_ref[...]
        rank = jnp.sum(onehot * prefix, axis=-1, keepdims=True)
        info_ref[...] = jnp.where(lane == 0, cls.astype(f32), jnp.where(lane == 1, rank, 0.0))
        carry_ref[...] = carry_ref[...] + jnp.sum(onehot, axis=0, keepdims=True)
        cnt_ref[...] = jnp.broadcast_to(carry_ref[...], cnt_ref.shape)
        if tile_refs:
            x1 = x_ref[...] + _mod_vec(mods_ref, row, 2) * out
            x1_ref[...] = x1
            x1s_ref[s % 2] = x1

    @pl.when(s < n_ctx)
    def _():
        step((fmc_ref, nac_ref, xp_ref))

    @pl.when((s >= n_ctx) & (s < n_all))
    def _():
        step((fml_ref, nal_ref, xs_ref))

    @pl.when(s == n_all)
    def _():
        step(None)


def _merge(fm_c, fm_l, na_c, na_l, p_all, xp, xs, mods, g2, wf, wn, wo, wr, br):
    n_ctx = T_CTX // MERGE_TM
    n_lat = T_LAT // MERGE_TM
    n_all = n_ctx + n_lat
    gcol = (FOURIER_WIDTH + 3 * NA_WIDTH) // D_MODEL
    const = lambda i: (0, 0)
    ctx_tile = lambda i: (jnp.minimum(i, n_ctx - 1), 0)
    lat_tile = lambda i: (jnp.clip(i - n_ctx, 0, n_lat - 1), 0)
    this_tile = lambda i: jnp.minimum(i, n_all - 1)
    prev_tile = lambda i: (jnp.maximum(i - 1, 0), 0)
    once = pl.Buffered(1)
    return pl.pallas_call(
        _merge_kernel,
        grid=(n_all + 1,),
        in_specs=[pl.BlockSpec((MERGE_TM, FOURIER_WIDTH), ctx_tile),
                  pl.BlockSpec((MERGE_TM, FOURIER_WIDTH), lat_tile),
                  pl.BlockSpec((MERGE_TM, NA_WIDTH), ctx_tile),
                  pl.BlockSpec((MERGE_TM, NA_WIDTH), lat_tile),
                  pl.BlockSpec((MERGE_TM, D_MODEL), lambda i: (this_tile(i), gcol)),
                  pl.BlockSpec((MERGE_TM, D_MODEL), lambda i: (this_tile(i), gcol + 1)),
                  pl.BlockSpec((MERGE_TM, D_MODEL), ctx_tile),
                  pl.BlockSpec((MERGE_TM, D_MODEL), lat_tile),
                  pl.BlockSpec((MOD_ROWS, N_MOD * D_MODEL), const, pipeline_mode=once),
                  pl.BlockSpec((1, D_MODEL), const, pipeline_mode=once),
                  pl.BlockSpec((FOURIER_WIDTH, D_MODEL), const, pipeline_mode=once),
                  pl.BlockSpec((NA_WIDTH, D_MODEL), const, pipeline_mode=once),
                  pl.BlockSpec((D_MODEL, D_MODEL), const, pipeline_mode=once),
                  pl.BlockSpec((D_MODEL, ROUTE_LANES), const, pipeline_mode=once),
                  pl.BlockSpec((1, ROUTE_LANES), const, pipeline_mode=once)],
        out_specs=[pl.BlockSpec((MERGE_TM, D_MODEL), lambda i: (this_tile(i), 0)),
                   pl.BlockSpec((MERGE_TM * CHUNKS, LANES), prev_tile),
                   pl.BlockSpec((MERGE_TM, ROUTE_LANES), prev_tile),
                   pl.BlockSpec((8, ROUTE_LANES), const)],
        out_shape=[jax.ShapeDtypeStruct((T_ALL, D_MODEL), f32),
                   jax.ShapeDtypeStruct((T_ALL * CHUNKS, LANES), f32),
                   jax.ShapeDtypeStruct((T_ALL, ROUTE_LANES), f32),
                   jax.ShapeDtypeStruct((8, ROUTE_LANES), f32)],
        scratch_shapes=[pltpu.VMEM((1, ROUTE_LANES), f32), pltpu.VMEM((2, MERGE_TM, D_MODEL), f32)],
        compiler_params=_params(1),
        name="merge",
    )(fm_c, fm_l, na_c, na_l, p_all, p_all, xp, xs, mods, g2, wf, wn, wo, wr, br)


def _perm_kernel(pos_ref, src_ref):
    def clear(p, carry):
        src_ref[p] = 0
        return carry

    lax.fori_loop(0, T_PAD, clear, 0, unroll=16)

    def place(t, carry):
        src_ref[pos_ref[t]] = t
        return carry

    lax.fori_loop(0, T_ALL, place, 0, unroll=16)


def _inverse_positions(pos):
    smem = pl.BlockSpec(memory_space=pltpu.SMEM)
    return pl.pallas_call(
        _perm_kernel,
        in_specs=[smem],
        out_specs=smem,
        out_shape=jax.ShapeDtypeStruct((T_PAD,), jnp.int32),
        name="permutation",
    )(pos)


def _row_copy(idx_ref, base, src_hbm, buf, sem, slot, k):
    tok = idx_ref[base + k]
    return pltpu.make_async_copy(src_hbm.at[pl.ds(pl.multiple_of(tok * CHUNKS, CHUNKS), CHUNKS), :],
                                 buf.at[slot, pl.ds(pl.multiple_of(k * CHUNKS, CHUNKS), CHUNKS), :],
                                 sem.at[slot])


def _start_gather(idx_ref, base, src_hbm, buf, sem, slot, rows, queues=(0, 1)):
    def body(k8, carry):
        for j in range(8):
            _row_copy(idx_ref, base, src_hbm, buf, sem, slot, k8 * 8 + j).start(priority=queues[j % len(queues)])
        return carry

    lax.fori_loop(0, rows // 8, body, 0)


def _wait_gather(buf, sem, slot):
    pltpu.make_async_copy(buf.at[slot], buf.at[slot], sem.at[slot]).wait()


def _gathered_rows(buf, slot, rows):
    return jnp.concatenate([buf[slot, pl.ds(s, rows, stride=CHUNKS), :] for s in range(CHUNKS)], axis=1)


def _moe_kernel(ea_ref, eb_ref, nt_ref, src_ref, h2c_hbm, wga_ref, wua_ref, wda_ref, wgb_ref, wub_ref, wdb_ref,
                wr_ref, br_ref, o_ref, buf, sem):
    i = pl.program_id(0)
    nt = nt_ref[0]

    @pl.when(i >= nt)
    def _():
        o_ref[...] = jnp.zeros_like(o_ref)

    def fetch(tile):
        _start_gather(src_ref, tile * MOE_TILE, h2c_hbm, buf, sem, tile % MOE_SLOTS, MOE_TILE, queues=(1,))

    @pl.when(i == 0)
    def _():
        for t in range(MOE_AHEAD):
            @pl.when(t < nt)
            def _(t=t):
                fetch(t)

    @pl.when(i + MOE_AHEAD < nt)
    def _():
        fetch(i + MOE_AHEAD)

    @pl.when(i < nt)
    def _():
        slot = i % MOE_SLOTS
        _wait_gather(buf, sem, slot)
        x = _gathered_rows(buf, slot, MOE_TILE).astype(bf16)
        logits = jnp.dot(x, wr_ref[...], preferred_element_type=f32) + br_ref[...]
        lane = lax.broadcasted_iota(jnp.int32, logits.shape, 1)
        ea = ea_ref[i]
        eb = eb_ref[i]

        def pick(col):
            return jnp.sum(jnp.where(lane == col, logits, 0.0), axis=-1, keepdims=True)

        lg = pick(ea // EXPERTS_PER_GROUP)
        p_group = 1.0 / jnp.sum(jnp.where(lane < N_GROUPS, jnp.exp(logits - lg), 0.0), axis=-1, keepdims=True)
        la = pick(N_GROUPS + ea)
        lb = pick(N_GROUPS + eb)
        w_a = p_group / (1.0 + jnp.exp(lb - la))
        w_b = p_group / (1.0 + jnp.exp(la - lb))

        def up(wg_ref, wu_ref):
            return (jnp.dot(x, wg_ref[0], preferred_element_type=f32),
                    jnp.dot(x, wu_ref[0], preferred_element_type=f32))

        def act(g, u):
            return (g * jax.nn.sigmoid(g) * u).astype(bf16)

        g_a, u_a = up(wga_ref, wua_ref)
        g_b, u_b = up(wgb_ref, wub_ref)
        hid_a = act(g_a, u_a)
        hid_b = act(g_b, u_b)
        per_chunk = MOE_OUT_CHUNK // LANES
        for c in range(D_MODEL // MOE_OUT_CHUNK):
            cols = slice(c * MOE_OUT_CHUNK, (c + 1) * MOE_OUT_CHUNK)
            y = (w_a * jnp.dot(hid_a, wda_ref[0, :, cols], preferred_element_type=f32)
                 + w_b * jnp.dot(hid_b, wdb_ref[0, :, cols], preferred_element_type=f32))
            for k in range(per_chunk):
                o_ref[pl.ds(c * per_chunk + k, MOE_TILE, stride=CHUNKS), :] = y[:, k * LANES:(k + 1) * LANES]


def _moe(tile_ea, tile_eb, ntiles, src, h2c, wg, wu, wd, wr, br):
    up = lambda tab: (lambda i, ea, eb, nt, src: (tab(ea, eb)[i], 0, 0))
    pick_a = lambda ea, eb: ea
    pick_b = lambda ea, eb: eb
    const = lambda i, ea, eb, nt, src: (0, 0)
    grid_spec = pltpu.PrefetchScalarGridSpec(
        num_scalar_prefetch=4,
        grid=(MOE_MAX_TILES,),
        in_specs=[pl.BlockSpec(memory_space=pl.ANY),
                  pl.BlockSpec((1, D_MODEL, EXPERT_HIDDEN), up(pick_a)),
                  pl.BlockSpec((1, D_MODEL, EXPERT_HIDDEN), up(pick_a)),
                  pl.BlockSpec((1, EXPERT_HIDDEN, D_MODEL), up(pick_a)),
                  pl.BlockSpec((1, D_MODEL, EXPERT_HIDDEN), up(pick_b)),
                  pl.BlockSpec((1, D_MODEL, EXPERT_HIDDEN), up(pick_b)),
                  pl.BlockSpec((1, EXPERT_HIDDEN, D_MODEL), up(pick_b)),
                  pl.BlockSpec((D_MODEL, ROUTE_LANES), const),
                  pl.BlockSpec((1, ROUTE_LANES), const)],
        out_specs=pl.BlockSpec((MOE_TILE * CHUNKS, LANES), lambda i, ea, eb, nt, src: (i, 0)),
        scratch_shapes=[pltpu.VMEM((MOE_SLOTS, MOE_TILE * CHUNKS, LANES), f32),
                        pltpu.SemaphoreType.DMA((MOE_SLOTS,))],
    )
    return pl.pallas_call(
        _moe_kernel,
        grid_spec=grid_spec,
        out_shape=jax.ShapeDtypeStruct((T_PAD * CHUNKS, LANES), f32),
        compiler_params=_params(1),
        name="moe",
    )(tile_ea, tile_eb, ntiles, src, h2c, wg, wu, wd, wg, wu, wd, wr, br)


def _final_kernel(pos_ref, ys_hbm, x1_ref, mods_ref, g_ref, o_ref, buf, sem, *, tile0, n_tiles, n_ctx_tiles):
    i = pl.program_id(0)

    def fetch(tile):
        _start_gather(pos_ref, (tile0 + tile) * FINAL_TILE, ys_hbm, buf, sem, tile % FINAL_SLOTS, FINAL_TILE)

    @pl.when(i == 0)
    def _():
        for t in range(min(FINAL_AHEAD, n_tiles)):
            fetch(t)

    @pl.when(i + FINAL_AHEAD < n_tiles)
    def _():
        fetch(i + FINAL_AHEAD)

    slot = i % FINAL_SLOTS
    _wait_gather(buf, sem, slot)
    y = _gathered_rows(buf, slot, FINAL_TILE)
    row = _mod_row(tile0 + i, n_ctx_tiles, DEC_SEQ // FINAL_TILE)
    x2 = x1_ref[...] + _mod_vec(mods_ref, row, 5) * y
    o_ref[...] = _rms(x2) * g_ref[...]


def _final(pos, ys, x1, mods, g, tile0, n_tiles):
    body = functools.partial(_final_kernel, tile0=tile0, n_tiles=n_tiles, n_ctx_tiles=T_CTX // FINAL_TILE)
    grid_spec = pltpu.PrefetchScalarGridSpec(
        num_scalar_prefetch=1,
        grid=(n_tiles,),
        in_specs=[pl.BlockSpec(memory_space=pl.ANY),
                  pl.BlockSpec((FINAL_TILE, D_MODEL), lambda i, pos: (tile0 + i, 0)),
                  pl.BlockSpec((MOD_ROWS, N_MOD * D_MODEL), lambda i, pos: (0, 0)),
                  pl.BlockSpec((1, D_MODEL), lambda i, pos: (0, 0))],
        out_specs=pl.BlockSpec((FINAL_TILE, D_MODEL), lambda i, pos: (i, 0)),
        scratch_shapes=[pltpu.VMEM((FINAL_SLOTS, FINAL_TILE * CHUNKS, LANES), f32),
                        pltpu.SemaphoreType.DMA((FINAL_SLOTS,))],
    )
    return pl.pallas_call(
        body,
        grid_spec=grid_spec,
        out_shape=jax.ShapeDtypeStruct((n_tiles * FINAL_TILE, D_MODEL), f32),
        compiler_params=_params(1),
        name=f"final_{tile0}",
    )(pos, ys, x1, mods, g)


def kernel(x_prompt, x_sample, cache_k_ctx, cache_v_ctx, c, c_ctx, norm1_g, norm2_g, w_ada, b_ada, w_in, b_in,
           w_fourier, w_na_o, rpb, w_out, w_router_group, b_router_group, w_router_expert, b_router_expert,
           w_exp_gate, w_exp_up, w_exp_down, final_norm_g):
    xp = x_prompt.reshape(T_CTX, D_MODEL)
    xs = x_sample.reshape(T_LAT, D_MODEL)
    c_all = jnp.concatenate([c_ctx[None, :], c, jnp.zeros((MOD_ROWS - 1 - DEC_BATCH, D_MODEL), f32)], axis=0)
    w_route = jnp.concatenate([w_router_group[0], w_router_expert[0]], axis=1)
    w_route = jnp.pad(w_route, ((0, 0), (0, ROUTE_LANES - w_route.shape[1]))).astype(bf16)
    b_route = jnp.concatenate([b_router_group[0], b_router_expert[0]])
    b_route = jnp.pad(b_route, (0, ROUTE_LANES - b_route.shape[0]))[None, :]

    mods = _ada(c_all, w_ada[0], b_ada[0][None, :])
    h1 = _modulate1(xp, xs, mods, norm1_g)
    p_all, k_new, v_new, (wg, wu, wd, wf, wn, wo) = _project(
        h1, w_in[0], b_in,
        (w_exp_gate.reshape(N_EXPERTS * D_MODEL, EXPERT_HIDDEN), w_exp_up.reshape(N_EXPERTS * D_MODEL, EXPERT_HIDDEN),
         w_exp_down.reshape(N_EXPERTS * EXPERT_HIDDEN, D_MODEL), w_fourier[0], w_na_o[0], w_out[0]))
    wg = wg.reshape(N_EXPERTS, D_MODEL, EXPERT_HIDDEN)
    wu = wu.reshape(N_EXPERTS, D_MODEL, EXPERT_HIDDEN)
    wd = wd.reshape(N_EXPERTS, EXPERT_HIDDEN, D_MODEL)

    tt = _bias_table(rpb[0])
    na_c = _attn_ctx(p_all)
    ck = cache_k_ctx.reshape(DEC_BATCH, PAST_LEN, NA_WIDTH)
    cv = cache_v_ctx.reshape(DEC_BATCH, PAST_LEN, NA_WIDTH)
    na_l = _attn_lat(p_all, ck, cv, tt)
    fm_c = _fourier(p_all, SEQ, BATCH, 0)
    fm_l = _fourier(p_all, DEC_SEQ, DEC_BATCH, T_CTX // DEC_SEQ)

    x1, h2c, info, counts = _merge(fm_c, fm_l, na_c, na_l, p_all, xp, xs, mods, norm2_g, wf, wn, wo,
                                   w_route, b_route)

    cnt = counts[0, :N_CLASSES].astype(jnp.int32)
    padded = ((cnt + MOE_TILE - 1) // MOE_TILE) * MOE_TILE
    off_end = jnp.cumsum(padded)
    off = off_end - padded
    ntiles = (off_end[-1] // MOE_TILE).astype(jnp.int32)
    tile_start = jnp.arange(MOE_MAX_TILES, dtype=jnp.int32) * MOE_TILE
    last_start = (ntiles - 1) * MOE_TILE
    tile_cls = jnp.sum(jnp.minimum(tile_start, last_start)[:, None] >= off_end[None, :], axis=1)
    pair = tile_cls % N_PAIRS
    group = tile_cls // N_PAIRS
    pair_hot = pair[:, None] == jnp.arange(N_PAIRS)[None, :]
    tile_ea = (group * EXPERTS_PER_GROUP + jnp.sum(pair_hot * jnp.asarray(PAIR_A), axis=1)).astype(jnp.int32)
    tile_eb = (group * EXPERTS_PER_GROUP + jnp.sum(pair_hot * jnp.asarray(PAIR_B), axis=1)).astype(jnp.int32)

    class_hot = info[:, 0:1] == jnp.arange(N_CLASSES, dtype=f32)[None, :]
    pos = (info[:, 1] + jnp.sum(jnp.where(class_hot, off.astype(f32)[None, :], 0.0), axis=1)).astype(jnp.int32)
    src = _inverse_positions(pos)
    ys = _moe(tile_ea, tile_eb, ntiles[None], src, h2c, wg, wu, wd, w_route, b_route)

    fg = final_norm_g[None, :]
    y_prompt = _final(pos, ys, x1, mods, fg, 0, T_CTX // FINAL_TILE)
    y_sample = _final(pos, ys, x1, mods, fg, T_CTX // FINAL_TILE, T_LAT // FINAL_TILE)

    shape_kv = (BATCH, 1, SEQ, NA_HEADS, HEAD_DIM)
    return (y_prompt.reshape(BATCH, SEQ, D_MODEL), y_sample.reshape(DEC_BATCH, DEC_SEQ, D_MODEL),
            k_new.reshape(shape_kv), v_new.reshape(shape_kv))
```

```python
import functools

import jax
import jax.numpy as jnp
import numpy as np
from jax import lax
from jax.experimental import pallas as pl
from jax.experimental.pallas import tpu as pltpu

D_MODEL = 2048
BATCH = 32
SEQ = 256
DEC_BATCH = 4
DEC_SEQ = 1024
PAST_LEN = 256
GRID_W = 64
GRID_ROWS = DEC_SEQ // GRID_W
NA_HEADS = 8
HEAD_DIM = 128
NA_WIDTH = NA_HEADS * HEAD_DIM
FOURIER_GROUPS = 4
FOURIER_GROUP_DIM = 256
FOURIER_WIDTH = FOURIER_GROUPS * FOURIER_GROUP_DIM
WIN_ROWS = 8
WIN_COLS = 16
N_GROUPS = 4
EXPERTS_PER_GROUP = 4
N_EXPERTS = N_GROUPS * EXPERTS_PER_GROUP
EXPERT_HIDDEN = 512
N_MOD = 6
IN_WIDTH = FOURIER_WIDTH + 3 * NA_WIDTH + 2 * D_MODEL
EPS = 1e-6
NEG_INF = -1e30

T_CTX = BATCH * SEQ
T_LAT = DEC_BATCH * DEC_SEQ
T_ALL = T_CTX + T_LAT

LANES = 128
CHUNKS = D_MODEL // LANES
MOD_ROWS = 8

PAIR_A = (0, 0, 0, 1, 1, 3)
PAIR_B = (1, 2, 3, 3, 2, 2)
N_PAIRS = len(PAIR_A)
N_CLASSES = N_GROUPS * N_PAIRS
MOE_TILE = 256
MOE_OUT_CHUNK = 512
MOE_AHEAD = 3
MOE_SLOTS = MOE_AHEAD + 1
MOE_MAX_TILES = -(-T_ALL // MOE_TILE) + N_CLASSES
FINAL_TILE = 256
FINAL_AHEAD = 3
FINAL_SLOTS = FINAL_AHEAD + 1
T_PAD = MOE_MAX_TILES * MOE_TILE

VMEM_LIMIT = 56 * 1024 * 1024

bf16 = jnp.bfloat16
f32 = jnp.float32


def _params(n_axes, vmem=VMEM_LIMIT):
    return pltpu.CompilerParams(dimension_semantics=("arbitrary",) * n_axes, vmem_limit_bytes=vmem)


def _mod_row(tile, n_ctx_tiles, tiles_per_request):
    return jnp.where(tile < n_ctx_tiles, 0, 1 + (tile - n_ctx_tiles) // tiles_per_request)


def _mod_vec(mods_ref, row, k):
    return mods_ref[pl.ds(row, 1), k * D_MODEL:(k + 1) * D_MODEL]


def _rms(x):
    return x * lax.rsqrt(jnp.mean(x * x, axis=-1, keepdims=True) + EPS)


def _ada_kernel(c_ref, w_ref, b_ref, o_ref):
    c = c_ref[...]
    s = (c * jax.nn.sigmoid(c)).astype(bf16)
    o_ref[...] = jnp.dot(s, w_ref[...].astype(bf16), preferred_element_type=f32) + b_ref[...]


def _ada(c_all, w_ada, b_ada):
    tn = 1024
    n = N_MOD * D_MODEL
    return pl.pallas_call(
        _ada_kernel,
        grid=(n // tn,),
        in_specs=[pl.BlockSpec((MOD_ROWS, D_MODEL), lambda j: (0, 0)),
                  pl.BlockSpec((D_MODEL, tn), lambda j: (0, j)),
                  pl.BlockSpec((1, tn), lambda j: (0, j))],
        out_specs=pl.BlockSpec((MOD_ROWS, tn), lambda j: (0, j)),
        out_shape=jax.ShapeDtypeStruct((MOD_ROWS, n), f32),
        compiler_params=_params(1),
        name="ada",
    )(c_all, w_ada, b_ada)


MOD_TILE = 512


def _mod_kernel(xp_ref, xs_ref, mods_ref, g_ref, o_ref):
    i = pl.program_id(0)
    n_ctx = T_CTX // MOD_TILE
    row = _mod_row(i, n_ctx, DEC_SEQ // MOD_TILE)
    sh = _mod_vec(mods_ref, row, 0)
    sc = _mod_vec(mods_ref, row, 1)

    def run(x_ref):
        h = _rms(x_ref[...]) * g_ref[...] * (1.0 + sc) + sh
        o_ref[...] = h.astype(bf16)

    @pl.when(i < n_ctx)
    def _():
        run(xp_ref)

    @pl.when(i >= n_ctx)
    def _():
        run(xs_ref)


def _modulate1(xp, xs, mods, g):
    n_ctx = T_CTX // MOD_TILE
    return pl.pallas_call(
        _mod_kernel,
        grid=(T_ALL // MOD_TILE,),
        in_specs=[pl.BlockSpec((MOD_TILE, D_MODEL), lambda i: (jnp.minimum(i, n_ctx - 1), 0)),
                  pl.BlockSpec((MOD_TILE, D_MODEL), lambda i: (jnp.maximum(i - n_ctx, 0), 0)),
                  pl.BlockSpec((MOD_ROWS, N_MOD * D_MODEL), lambda i: (0, 0)),
                  pl.BlockSpec((1, D_MODEL), lambda i: (0, 0))],
        out_specs=pl.BlockSpec((MOD_TILE, D_MODEL), lambda i: (i, 0)),
        out_shape=jax.ShapeDtypeStruct((T_ALL, D_MODEL), bf16),
        compiler_params=_params(1),
        name="modulate1",
    )(xp, xs, mods, g)


PROJ_TM = 1024
PROJ_TN = 1024
PROJ_CHUNK = 256
K_COL = (FOURIER_WIDTH + NA_WIDTH) // PROJ_TN
V_COL = (FOURIER_WIDTH + 2 * NA_WIDTH) // PROJ_TN


PROJ_STEPS = (IN_WIDTH // PROJ_TN) * (T_ALL // PROJ_TM)
CAST_SHAPES = ((N_EXPERTS * D_MODEL, EXPERT_HIDDEN, 512),
               (N_EXPERTS * D_MODEL, EXPERT_HIDDEN, 512),
               (N_EXPERTS * EXPERT_HIDDEN, D_MODEL, 128),
               (FOURIER_WIDTH, D_MODEL, 16),
               (NA_WIDTH, D_MODEL, 16),
               (D_MODEL, D_MODEL, 32))
N_CAST = len(CAST_SHAPES)


def _kv_copy(acc_ref, dst_hbm, tile, sem):
    rows = pl.ds(pl.multiple_of(tile * PROJ_TM, PROJ_TM), PROJ_TM)
    return pltpu.make_async_copy(acc_ref.at[tile % 2], dst_hbm.at[rows, :], sem)


def _proj_kernel(h_ref, w_ref, b_ref, *rest):
    cast_in = rest[:N_CAST]
    p_ref, k_hbm, v_hbm = rest[N_CAST:N_CAST + 3]
    cast_out = rest[N_CAST + 3:2 * N_CAST + 3]
    wb_ref, acc_ref, sem = rest[-3:]
    j = pl.program_id(0)
    i = pl.program_id(1)
    n_ctx = T_CTX // PROJ_TM
    is_k = (j == K_COL) & (i < n_ctx)
    is_v = (j == V_COL) & (i < n_ctx)

    @pl.when(i == 0)
    def _():
        wb_ref[...] = w_ref[...].astype(bf16)

    n_chunks = PROJ_TN // PROJ_CHUNK
    for c in range(n_chunks):
        cols = slice(c * PROJ_CHUNK, (c + 1) * PROJ_CHUNK)
        acc = jnp.dot(h_ref[...], wb_ref[:, cols], preferred_element_type=f32) + b_ref[:, cols]
        p_ref[:, cols] = acc.astype(bf16)
        acc_ref[i % 2, :, cols] = acc
        for src, dst in list(zip(cast_in, cast_out))[c::n_chunks]:
            dst[...] = src[...].astype(bf16)

    @pl.when(((j == K_COL) | (j == V_COL)) & (i >= 1) & (i <= n_ctx))
    def _():
        _kv_copy(acc_ref, k_hbm, i - 1, sem).wait()

    @pl.when(is_k)
    def _():
        _kv_copy(acc_ref, k_hbm, i, sem).start()

    @pl.when(is_v)
    def _():
        _kv_copy(acc_ref, v_hbm, i, sem).start()


def _cast_spec(rows, cols, rb):
    m = T_ALL // PROJ_TM
    return pl.BlockSpec((rb, cols), lambda j, i: (jnp.minimum(j * m + i, rows // rb - 1), 0))


def _project(h1, w_in, b_in, cast_weights):
    assert all(rows // rb <= PROJ_STEPS for rows, _, rb in CAST_SHAPES)
    assert T_CTX // PROJ_TM < T_ALL // PROJ_TM
    cast_specs = [_cast_spec(*s) for s in CAST_SHAPES]
    hbm = pl.BlockSpec(memory_space=pl.ANY)
    outs = pl.pallas_call(
        _proj_kernel,
        grid=(IN_WIDTH // PROJ_TN, T_ALL // PROJ_TM),
        in_specs=[pl.BlockSpec((PROJ_TM, D_MODEL), lambda j, i: (i, 0)),
                  pl.BlockSpec((D_MODEL, PROJ_TN), lambda j, i: (0, j)),
                  pl.BlockSpec((1, PROJ_TN), lambda j, i: (0, j))] + cast_specs,
        out_specs=[pl.BlockSpec((PROJ_TM, PROJ_TN), lambda j, i: (i, j)), hbm, hbm] + cast_specs,
        out_shape=[jax.ShapeDtypeStruct((T_ALL, IN_WIDTH), bf16),
                   jax.ShapeDtypeStruct((T_CTX, NA_WIDTH), f32),
                   jax.ShapeDtypeStruct((T_CTX, NA_WIDTH), f32)]
                  + [jax.ShapeDtypeStruct((rows, cols), bf16) for rows, cols, _ in CAST_SHAPES],
        scratch_shapes=[pltpu.VMEM((D_MODEL, PROJ_TN), bf16), pltpu.VMEM((2, PROJ_TM, PROJ_TN), f32),
                        pltpu.SemaphoreType.DMA(())],
        compiler_params=_params(2),
        name="project",
    )(h1, w_in, b_in, *cast_weights)
    return outs[0], outs[1], outs[2], outs[3:]


Q_GROUP_ROWS = 4
N_Q_GROUPS = GRID_ROWS // Q_GROUP_ROWS
Q_GROUP = Q_GROUP_ROWS * GRID_W
KEY_SPAN_ROWS = 12
KEY_SPAN = KEY_SPAN_ROWS * GRID_W


def _window_start(r):
    return min(max(r - WIN_ROWS // 2, 0), GRID_ROWS - WIN_ROWS)


def _key_base(g):
    lo = _window_start(g * Q_GROUP_ROWS)
    hi = _window_start((g + 1) * Q_GROUP_ROWS - 1) + WIN_ROWS
    assert hi - lo <= KEY_SPAN_ROWS
    return min(lo, GRID_ROWS - KEY_SPAN_ROWS)


def _visible_span(g):
    base = _key_base(g)
    first = _window_start(g * Q_GROUP_ROWS) - base
    last = _window_start((g + 1) * Q_GROUP_ROWS - 1) + WIN_ROWS - base
    lo = (first // 2) * 2 * GRID_W
    hi = -(-last // 2) * 2 * GRID_W
    return lo, hi


def _block_shift(g, rr):
    return _key_base(g) - (g * Q_GROUP_ROWS + rr) + WIN_ROWS - 1


TABLE_PAD = -min(_block_shift(g, rr) for g in range(N_Q_GROUPS) for rr in range(Q_GROUP_ROWS))
TABLE_BLOCKS = 2 * (-(-(max(_block_shift(g, rr) for g in range(N_Q_GROUPS) for rr in range(Q_GROUP_ROWS))
                        + TABLE_PAD + KEY_SPAN_ROWS + 1) // 2))
TABLE_LANES = TABLE_BLOCKS * GRID_W


def _bias_kernel(r_ref, o_ref):
    qc = lax.broadcasted_iota(jnp.int32, (GRID_W, TABLE_LANES), 0)
    lane = lax.broadcasted_iota(jnp.int32, (GRID_W, TABLE_LANES), 1)
    kc = lane & (GRID_W - 1)
    dc = kc - qc + (WIN_COLS - 1)
    start = jnp.clip(qc - WIN_COLS // 2, 0, GRID_W - WIN_COLS)
    ok = (kc >= start) & (kc < start + WIN_COLS)
    tables = []
    for p in range(2):
        acc = jnp.zeros((GRID_W, TABLE_LANES), f32)
        for b in range(2 * WIN_COLS - 1):
            acc = jnp.where(dc == b, r_ref[0, b, p:p + 1, :], acc)
        tables.append(jnp.where(ok, acc, NEG_INF))
    span_row = jnp.right_shift(lax.broadcasted_iota(jnp.int32, (GRID_W, KEY_SPAN), 1), GRID_W.bit_length() - 1)
    for g in range(N_Q_GROUPS):
        for rr in range(Q_GROUP_ROWS):
            first = _window_start(g * Q_GROUP_ROWS + rr) - _key_base(g)
            e = _block_shift(g, rr) + TABLE_PAD
            lo = (e - e % 2) * GRID_W
            in_window = (span_row >= first) & (span_row < first + WIN_ROWS)
            o_ref[0, g, rr * GRID_W:(rr + 1) * GRID_W, :] = jnp.where(
                in_window, tables[e % 2][:, lo:lo + KEY_SPAN], NEG_INF)


def _bias_table(rpb):
    nb = 2 * WIN_COLS - 1
    by_row = jnp.transpose(rpb, (0, 2, 1))
    by_row = jnp.pad(by_row, ((0, 0), (0, 0), (TABLE_PAD, TABLE_BLOCKS + 1 - TABLE_PAD - by_row.shape[2])))
    rpbx = jnp.stack([jnp.repeat(by_row[:, :, p:p + TABLE_BLOCKS], GRID_W, axis=2) for p in range(2)], axis=2)
    return pl.pallas_call(
        _bias_kernel,
        grid=(NA_HEADS,),
        in_specs=[pl.BlockSpec((1, nb, 2, TABLE_LANES), lambda h: (h, 0, 0, 0))],
        out_specs=pl.BlockSpec((1, N_Q_GROUPS, Q_GROUP, KEY_SPAN), lambda h: (h, 0, 0, 0)),
        out_shape=jax.ShapeDtypeStruct((NA_HEADS, N_Q_GROUPS, Q_GROUP, KEY_SPAN), f32),
        compiler_params=_params(1),
        name="bias_table",
    )(rpbx)


def _qk(q, k):
    return lax.dot_general(q, k, (((1,), (1,)), ((), ())), preferred_element_type=f32)


def _attn_ctx_kernel(q_ref, k_ref, v_ref, o_ref):
    scale = HEAD_DIM ** -0.5
    for h in range(NA_HEADS):
        sl = slice(h * HEAD_DIM, (h + 1) * HEAD_DIM)
        s = _qk(q_ref[:, sl], k_ref[:, sl]) * scale
        p = jnp.exp(s - jnp.max(s, axis=-1, keepdims=True))
        l = jnp.sum(p, axis=-1, keepdims=True)
        o = jnp.dot(p.astype(bf16), v_ref[:, sl], preferred_element_type=f32) / l
        o_ref[:, sl] = o.astype(bf16)


def _attn_ctx(p_all):
    qcol = FOURIER_WIDTH // NA_WIDTH
    return pl.pallas_call(
        _attn_ctx_kernel,
        grid=(BATCH,),
        in_specs=[pl.BlockSpec((SEQ, NA_WIDTH), lambda b: (b, qcol)),
                  pl.BlockSpec((SEQ, NA_WIDTH), lambda b: (b, qcol + 1)),
                  pl.BlockSpec((SEQ, NA_WIDTH), lambda b: (b, qcol + 2))],
        out_specs=pl.BlockSpec((SEQ, NA_WIDTH), lambda b: (b, 0)),
        out_shape=jax.ShapeDtypeStruct((T_CTX, NA_WIDTH), bf16),
        compiler_params=_params(1),
        name="attn_ctx",
    )(p_all, p_all, p_all)


def _attn_lat_kernel(q_ref, k_ref, v_ref, kc_ref, vc_ref, bias_ref, o_ref):
    scale = HEAD_DIM ** -0.5
    kcb = kc_ref[0].astype(bf16)
    vcb = vc_ref[0].astype(bf16)
    for g in range(N_Q_GROUPS):
        lo, hi = _visible_span(g)
        k0 = _key_base(g) * GRID_W + lo
        nk = hi - lo
        q = q_ref[g * Q_GROUP:(g + 1) * Q_GROUP, :]
        s_nb = _qk(q, k_ref[k0:k0 + nk, :]) * scale + bias_ref[0, g, :, lo:hi]
        s_cx = _qk(q, kcb) * scale
        m = jnp.maximum(jnp.max(s_nb, axis=-1, keepdims=True), jnp.max(s_cx, axis=-1, keepdims=True))
        p_nb = jnp.exp(s_nb - m)
        p_cx = jnp.exp(s_cx - m)
        l = jnp.sum(p_nb, axis=-1, keepdims=True) + jnp.sum(p_cx, axis=-1, keepdims=True)
        o = (jnp.dot(p_nb.astype(bf16), v_ref[k0:k0 + nk, :], preferred_element_type=f32)
             + jnp.dot(p_cx.astype(bf16), vcb, preferred_element_type=f32)) / l
        o_ref[g * Q_GROUP:(g + 1) * Q_GROUP, :] = o.astype(bf16)


def _attn_lat(p_all, ck, cv, bias):
    row0 = T_CTX // DEC_SEQ
    qcol = FOURIER_WIDTH // HEAD_DIM
    return pl.pallas_call(
        _attn_lat_kernel,
        grid=(NA_HEADS, DEC_BATCH),
        in_specs=[pl.BlockSpec((DEC_SEQ, HEAD_DIM), lambda h, b: (row0 + b, qcol + h)),
                  pl.BlockSpec((DEC_SEQ, HEAD_DIM), lambda h, b: (row0 + b, qcol + NA_HEADS + h)),
                  pl.BlockSpec((DEC_SEQ, HEAD_DIM), lambda h, b: (row0 + b, qcol + 2 * NA_HEADS + h)),
                  pl.BlockSpec((1, PAST_LEN, HEAD_DIM), lambda h, b: (b, 0, h)),
                  pl.BlockSpec((1, PAST_LEN, HEAD_DIM), lambda h, b: (b, 0, h)),
                  pl.BlockSpec((1, N_Q_GROUPS, Q_GROUP, KEY_SPAN), lambda h, b: (h, 0, 0, 0))],
        out_specs=pl.BlockSpec((DEC_SEQ, HEAD_DIM), lambda h, b: (b, h)),
        out_shape=jax.ShapeDtypeStruct((T_LAT, NA_WIDTH), bf16),
        compiler_params=_params(2),
        name="attn_lat",
    )(p_all, p_all, p_all, ck, cv, bias)


def _dft_consts(n):
    def cs(m):
        idx = (np.arange(m)[:, None] * np.arange(m)[None, :]) % m
        ang = 2.0 * np.pi * idx.astype(np.float64) / m
        return np.cos(ang), np.sin(ang)

    cc, sc = cs(FOURIER_GROUP_DIM)
    cn, sn = cs(n)
    chan = np.concatenate([cc, sc], axis=1) / np.sqrt(FOURIER_GROUP_DIM)
    pos = np.concatenate([cn, -sn], axis=1) / np.sqrt(n)
    return jnp.asarray(chan, f32).astype(bf16), jnp.asarray(pos, f32).astype(bf16)


def _fourier_kernel(u_ref, chan_ref, pos_ref, o_ref):
    gd = FOURIER_GROUP_DIM
    ys = [jnp.dot(u_ref[:, g * gd:(g + 1) * gd], chan_ref[...], preferred_element_type=f32)
          for g in range(FOURIER_GROUPS)]
    yc = jnp.concatenate([y[:, :gd] for y in ys], axis=1)
    ysn = jnp.concatenate([y[:, gd:] for y in ys], axis=1)
    stacked = jnp.concatenate([yc, ysn], axis=0).astype(bf16)
    o_ref[...] = jnp.dot(pos_ref[...], stacked, preferred_element_type=f32).astype(bf16)


def _fourier(p_all, n, n_req, row0):
    chan, pos = _dft_consts(n)
    return pl.pallas_call(
        _fourier_kernel,
        grid=(n_req,),
        in_specs=[pl.BlockSpec((n, FOURIER_WIDTH), lambda b: (row0 + b, 0)),
                  pl.BlockSpec((FOURIER_GROUP_DIM, 2 * FOURIER_GROUP_DIM), lambda b: (0, 0)),
                  pl.BlockSpec((n, 2 * n), lambda b: (0, 0))],
        out_specs=pl.BlockSpec((n, FOURIER_WIDTH), lambda b: (b, 0)),
        out_shape=jax.ShapeDtypeStruct((n_req * n, FOURIER_WIDTH), bf16),
        compiler_params=_params(1),
        name=f"fourier_{n}",
    )(p_all, chan, pos)


MERGE_TM = 256
ROUTE_LANES = LANES


def _class_of(logits):
    lane = lax.broadcasted_iota(jnp.int32, logits.shape, 1)
    big = jnp.int32(ROUTE_LANES)
    is_group = lane < N_GROUPS
    mg = jnp.max(jnp.where(is_group, logits, -jnp.inf), axis=-1, keepdims=True)
    gsel = jnp.min(jnp.where(is_group & (logits == mg), lane, big), axis=-1, keepdims=True)
    lo = N_GROUPS + EXPERTS_PER_GROUP * gsel
    in_group = (lane >= lo) & (lane < lo + EXPERTS_PER_GROUP)
    m1 = jnp.max(jnp.where(in_group, logits, -jnp.inf), axis=-1, keepdims=True)
    i1 = jnp.min(jnp.where(in_group & (logits == m1), lane, big), axis=-1, keepdims=True)
    rest = in_group & (lane != i1)
    m2 = jnp.max(jnp.where(rest, logits, -jnp.inf), axis=-1, keepdims=True)
    i2 = jnp.min(jnp.where(rest & (logits == m2), lane, big), axis=-1, keepdims=True)
    e_lo = jnp.minimum(i1, i2) - lo
    e_hi = jnp.maximum(i1, i2) - lo
    pair = jnp.zeros_like(e_lo)
    for idx in range(N_PAIRS):
        a, b = sorted((PAIR_A[idx], PAIR_B[idx]))
        pair = jnp.where((e_lo == a) & (e_hi == b), idx, pair)
    return gsel * N_PAIRS + pair


def _merge_kernel(fmc_ref, fml_ref, nac_ref, nal_ref, ga_ref, gb_ref, xp_ref, xs_ref, mods_ref, g2_ref,
                  wf_ref, wn_ref, wo_ref, wr_ref, br_ref,
                  x1_ref, h2c_ref, info_ref, cnt_ref, carry_ref, x1s_ref):
    s = pl.program_id(0)
    n_ctx = T_CTX // MERGE_TM
    n_all = T_ALL // MERGE_TM
    per_req = DEC_SEQ // MERGE_TM
    row = _mod_row(jnp.minimum(s, n_all - 1), n_ctx, per_req)
    prev_row = _mod_row(jnp.maximum(s - 1, 0), n_ctx, per_req)

    @pl.when(s == 0)
    def _():
        carry_ref[...] = jnp.zeros_like(carry_ref)
        x1s_ref[1] = jnp.zeros((MERGE_TM, D_MODEL), f32)

    def step(tile_refs):
        x1p = x1s_ref[(s + 1) % 2]
        if tile_refs:
            fm_ref, na_ref, x_ref = tile_refs
            a = jnp.dot(fm_ref[...], wf_ref[...], preferred_element_type=f32)

        h2 = _rms(x1p) * g2_ref[...] * (1.0 + _mod_vec(mods_ref, prev_row, 4)) + _mod_vec(mods_ref, prev_row, 3)
        if tile_refs:
            gate_a = jax.nn.sigmoid(ga_ref[...].astype(f32))
            b = jnp.dot(na_ref[...], wn_ref[...], preferred_element_type=f32)

        for k in range(CHUNKS):
            h2c_ref[pl.ds(k, MERGE_TM, stride=CHUNKS), :] = h2[:, k * LANES:(k + 1) * LANES]
        logits = jnp.dot(h2.astype(bf16), wr_ref[...], preferred_element_type=f32) + br_ref[...]
        if tile_refs:
            merged = gate_a * a + jax.nn.sigmoid(gb_ref[...].astype(f32)) * b
            out = jnp.dot(merged.astype(bf16), wo_ref[...], preferred_element_type=f32)

        cls = _class_of(logits)
        lane = lax.broadcasted_iota(jnp.int32, logits.shape, 1)
        onehot = ((lane == cls) & (s > 0)).astype(f32)
        r = lax.broadcasted_iota(jnp.int32, (MERGE_TM, MERGE_TM), 0)
        c = lax.broadcasted_iota(jnp.int32, (MERGE_TM, MERGE_TM), 1)
        before = (c < r).astype(bf16)
        prefix = jnp.dot(before, onehot.astype(bf16), preferred_element_type=f32) + carry_ref[...]
        rank = jnp.sum(onehot * prefix, axis=-1, keepdims=True)
        info_ref[...] = jnp.where(lane == 0, cls.astype(f32), jnp.where(lane == 1, rank, 0.0))
        carry_ref[...] = carry_ref[...] + jnp.sum(onehot, axis=0, keepdims=True)
        cnt_ref[...] = jnp.broadcast_to(carry_ref[...], cnt_ref.shape)
        if tile_refs:
            x1 = x_ref[...] + _mod_vec(mods_ref, row, 2) * out
            x1_ref[...] = x1
            x1s_ref[s % 2] = x1

    @pl.when(s < n_ctx)
    def _():
        step((fmc_ref, nac_ref, xp_ref))

    @pl.when((s >= n_ctx) & (s < n_all))
    def _():
        step((fml_ref, nal_ref, xs_ref))

    @pl.when(s == n_all)
    def _():
        step(None)


def _merge(fm_c, fm_l, na_c, na_l, p_all, xp, xs, mods, g2, wf, wn, wo, wr, br):
    n_ctx = T_CTX // MERGE_TM
    n_lat = T_LAT // MERGE_TM
    n_all = n_ctx + n_lat
    gcol = (FOURIER_WIDTH + 3 * NA_WIDTH) // D_MODEL
    const = lambda i: (0, 0)
    ctx_tile = lambda i: (jnp.minimum(i, n_ctx - 1), 0)
    lat_tile = lambda i: (jnp.clip(i - n_ctx, 0, n_lat - 1), 0)
    this_tile = lambda i: jnp.minimum(i, n_all - 1)
    prev_tile = lambda i: (jnp.maximum(i - 1, 0), 0)
    once = pl.Buffered(1)
    return pl.pallas_call(
        _merge_kernel,
        grid=(n_all + 1,),
        in_specs=[pl.BlockSpec((MERGE_TM, FOURIER_WIDTH), ctx_tile),
                  pl.BlockSpec((MERGE_TM, FOURIER_WIDTH), lat_tile),
                  pl.BlockSpec((MERGE_TM, NA_WIDTH), ctx_tile),
                  pl.BlockSpec((MERGE_TM, NA_WIDTH), lat_tile),
                  pl.BlockSpec((MERGE_TM, D_MODEL), lambda i: (this_tile(i), gcol)),
                  pl.BlockSpec((MERGE_TM, D_MODEL), lambda i: (this_tile(i), gcol + 1)),
                  pl.BlockSpec((MERGE_TM, D_MODEL), ctx_tile),
                  pl.BlockSpec((MERGE_TM, D_MODEL), lat_tile),
                  pl.BlockSpec((MOD_ROWS, N_MOD * D_MODEL), const, pipeline_mode=once),
                  pl.BlockSpec((1, D_MODEL), const, pipeline_mode=once),
                  pl.BlockSpec((FOURIER_WIDTH, D_MODEL), const, pipeline_mode=once),
                  pl.BlockSpec((NA_WIDTH, D_MODEL), const, pipeline_mode=once),
                  pl.BlockSpec((D_MODEL, D_MODEL), const, pipeline_mode=once),
                  pl.BlockSpec((D_MODEL, ROUTE_LANES), const, pipeline_mode=once),
                  pl.BlockSpec((1, ROUTE_LANES), const, pipeline_mode=once)],
        out_specs=[pl.BlockSpec((MERGE_TM, D_MODEL), lambda i: (this_tile(i), 0)),
                   pl.BlockSpec((MERGE_TM * CHUNKS, LANES), prev_tile),
                   pl.BlockSpec((MERGE_TM, ROUTE_LANES), prev_tile),
                   pl.BlockSpec((8, ROUTE_LANES), const)],
        out_shape=[jax.ShapeDtypeStruct((T_ALL, D_MODEL), f32),
                   jax.ShapeDtypeStruct((T_ALL * CHUNKS, LANES), f32),
                   jax.ShapeDtypeStruct((T_ALL, ROUTE_LANES), f32),
                   jax.ShapeDtypeStruct((8, ROUTE_LANES), f32)],
        scratch_shapes=[pltpu.VMEM((1, ROUTE_LANES), f32), pltpu.VMEM((2, MERGE_TM, D_MODEL), f32)],
        compiler_params=_params(1),
        name="merge",
    )(fm_c, fm_l, na_c, na_l, p_all, p_all, xp, xs, mods, g2, wf, wn, wo, wr, br)


def _perm_kernel(pos_ref, src_ref):
    def clear(p, carry):
        src_ref[p] = 0
        return carry

    lax.fori_loop(0, T_PAD, clear, 0, unroll=16)

    def place(t, carry):
        src_ref[pos_ref[t]] = t
        return carry

    lax.fori_loop(0, T_ALL, place, 0, unroll=16)


def _inverse_positions(pos):
    smem = pl.BlockSpec(memory_space=pltpu.SMEM)
    return pl.pallas_call(
        _perm_kernel,
        in_specs=[smem],
        out_specs=smem,
        out_shape=jax.ShapeDtypeStruct((T_PAD,), jnp.int32),
        name="permutation",
    )(pos)


def _row_copy(idx_ref, base, src_hbm, buf, sem, slot, k):
    tok = idx_ref[base + k]
    return pltpu.make_async_copy(src_hbm.at[pl.ds(pl.multiple_of(tok * CHUNKS, CHUNKS), CHUNKS), :],
                                 buf.at[slot, pl.ds(pl.multiple_of(k * CHUNKS, CHUNKS), CHUNKS), :],
                                 sem.at[slot])


def _start_gather(idx_ref, base, src_hbm, buf, sem, slot, rows, queues=(0, 1)):
    def body(k8, carry):
        for j in range(8):
            _row_copy(idx_ref, base, src_hbm, buf, sem, slot, k8 * 8 + j).start(priority=queues[j % len(queues)])
        return carry

    lax.fori_loop(0, rows // 8, body, 0)


def _wait_gather(buf, sem, slot):
    pltpu.make_async_copy(buf.at[slot], buf.at[slot], sem.at[slot]).wait()


def _gathered_rows(buf, slot, rows):
    return jnp.concatenate([buf[slot, pl.ds(s, rows, stride=CHUNKS), :] for s in range(CHUNKS)], axis=1)


def _moe_kernel(ea_ref, eb_ref, nt_ref, src_ref, h2c_hbm, wga_ref, wua_ref, wda_ref, wgb_ref, wub_ref, wdb_ref,
                wr_ref, br_ref, o_ref, buf, sem):
    i = pl.program_id(0)
    nt = nt_ref[0]

    @pl.when(i >= nt)
    def _():
        o_ref[...] = jnp.zeros_like(o_ref)

    def fetch(tile):
        _start_gather(src_ref, tile * MOE_TILE, h2c_hbm, buf, sem, tile % MOE_SLOTS, MOE_TILE, queues=(1,))

    @pl.when(i == 0)
    def _():
        for t in range(MOE_AHEAD):
            @pl.when(t < nt)
            def _(t=t):
                fetch(t)

    @pl.when(i + MOE_AHEAD < nt)
    def _():
        fetch(i + MOE_AHEAD)

    @pl.when(i < nt)
    def _():
        slot = i % MOE_SLOTS
        _wait_gather(buf, sem, slot)
        x = _gathered_rows(buf, slot, MOE_TILE).astype(bf16)
        logits = jnp.dot(x, wr_ref[...], preferred_element_type=f32) + br_ref[...]
        lane = lax.broadcasted_iota(jnp.int32, logits.shape, 1)
        ea = ea_ref[i]
        eb = eb_ref[i]

        def pick(col):
            return jnp.sum(jnp.where(lane == col, logits, 0.0), axis=-1, keepdims=True)

        lg = pick(ea // EXPERTS_PER_GROUP)
        p_group = 1.0 / jnp.sum(jnp.where(lane < N_GROUPS, jnp.exp(logits - lg), 0.0), axis=-1, keepdims=True)
        la = pick(N_GROUPS + ea)
        lb = pick(N_GROUPS + eb)
        w_a = p_group / (1.0 + jnp.exp(lb - la))
        w_b = p_group / (1.0 + jnp.exp(la - lb))

        def up(wg_ref, wu_ref):
            return (jnp.dot(x, wg_ref[0], preferred_element_type=f32),
                    jnp.dot(x, wu_ref[0], preferred_element_type=f32))

        def act(g, u):
            return (g * jax.nn.sigmoid(g) * u).astype(bf16)

        g_a, u_a = up(wga_ref, wua_ref)
        g_b, u_b = up(wgb_ref, wub_ref)
        hid_a = act(g_a, u_a)
        hid_b = act(g_b, u_b)
        per_chunk = MOE_OUT_CHUNK // LANES
        for c in range(D_MODEL // MOE_OUT_CHUNK):
            cols = slice(c * MOE_OUT_CHUNK, (c + 1) * MOE_OUT_CHUNK)
            y = (w_a * jnp.dot(hid_a, wda_ref[0, :, cols], preferred_element_type=f32)
                 + w_b * jnp.dot(hid_b, wdb_ref[0, :, cols], preferred_element_type=f32))
            for k in range(per_chunk):
                o_ref[pl.ds(c * per_chunk + k, MOE_TILE, stride=CHUNKS), :] = y[:, k * LANES:(k + 1) * LANES]


def _moe(tile_ea, tile_eb, ntiles, src, h2c, wg, wu, wd, wr, br):
    up = lambda tab: (lambda i, ea, eb, nt, src: (tab(ea, eb)[i], 0, 0))
    pick_a = lambda ea, eb: ea
    pick_b = lambda ea, eb: eb
    const = lambda i, ea, eb, nt, src: (0, 0)
    grid_spec = pltpu.PrefetchScalarGridSpec(
        num_scalar_prefetch=4,
        grid=(MOE_MAX_TILES,),
        in_specs=[pl.BlockSpec(memory_space=pl.ANY),
                  pl.BlockSpec((1, D_MODEL, EXPERT_HIDDEN), up(pick_a)),
                  pl.BlockSpec((1, D_MODEL, EXPERT_HIDDEN), up(pick_a)),
                  pl.BlockSpec((1, EXPERT_HIDDEN, D_MODEL), up(pick_a)),
                  pl.BlockSpec((1, D_MODEL, EXPERT_HIDDEN), up(pick_b)),
                  pl.BlockSpec((1, D_MODEL, EXPERT_HIDDEN), up(pick_b)),
                  pl.BlockSpec((1, EXPERT_HIDDEN, D_MODEL), up(pick_b)),
                  pl.BlockSpec((D_MODEL, ROUTE_LANES), const),
                  pl.BlockSpec((1, ROUTE_LANES), const)],
        out_specs=pl.BlockSpec((MOE_TILE * CHUNKS, LANES), lambda i, ea, eb, nt, src: (i, 0)),
        scratch_shapes=[pltpu.VMEM((MOE_SLOTS, MOE_TILE * CHUNKS, LANES), f32),
                        pltpu.SemaphoreType.DMA((MOE_SLOTS,))],
    )
    return pl.pallas_call(
        _moe_kernel,
        grid_spec=grid_spec,
        out_shape=jax.ShapeDtypeStruct((T_PAD * CHUNKS, LANES), f32),
        compiler_params=_params(1),
        name="moe",
    )(tile_ea, tile_eb, ntiles, src, h2c, wg, wu, wd, wg, wu, wd, wr, br)


def _final_kernel(pos_ref, ys_hbm, x1_ref, mods_ref, g_ref, o_ref, buf, sem, *, tile0, n_tiles, n_ctx_tiles):
    i = pl.program_id(0)

    def fetch(tile):
        _start_gather(pos_ref, (tile0 + tile) * FINAL_TILE, ys_hbm, buf, sem, tile % FINAL_SLOTS, FINAL_TILE)

    @pl.when(i == 0)
    def _():
        for t in range(min(FINAL_AHEAD, n_tiles)):
            fetch(t)

    @pl.when(i + FINAL_AHEAD < n_tiles)
    def _():
        fetch(i + FINAL_AHEAD)

    slot = i % FINAL_SLOTS
    _wait_gather(buf, sem, slot)
    y = _gathered_rows(buf, slot, FINAL_TILE)
    row = _mod_row(tile0 + i, n_ctx_tiles, DEC_SEQ // FINAL_TILE)
    x2 = x1_ref[...] + _mod_vec(mods_ref, row, 5) * y
    o_ref[...] = _rms(x2) * g_ref[...]


def _final(pos, ys, x1, mods, g, tile0, n_tiles):
    body = functools.partial(_final_kernel, tile0=tile0, n_tiles=n_tiles, n_ctx_tiles=T_CTX // FINAL_TILE)
    grid_spec = pltpu.PrefetchScalarGridSpec(
        num_scalar_prefetch=1,
        grid=(n_tiles,),
        in_specs=[pl.BlockSpec(memory_space=pl.ANY),
                  pl.BlockSpec((FINAL_TILE, D_MODEL), lambda i, pos: (tile0 + i, 0)),
                  pl.BlockSpec((MOD_ROWS, N_MOD * D_MODEL), lambda i, pos: (0, 0)),
                  pl.BlockSpec((1, D_MODEL), lambda i, pos: (0, 0))],
        out_specs=pl.BlockSpec((FINAL_TILE, D_MODEL), lambda i, pos: (i, 0)),
        scratch_shapes=[pltpu.VMEM((FINAL_SLOTS, FINAL_TILE * CHUNKS, LANES), f32),
                        pltpu.SemaphoreType.DMA((FINAL_SLOTS,))],
    )
    return pl.pallas_call(
        body,
        grid_spec=grid_spec,
        out_shape=jax.ShapeDtypeStruct((n_tiles * FINAL_TILE, D_MODEL), f32),
        compiler_params=_params(1),
        name=f"final_{tile0}",
    )(pos, ys, x1, mods, g)


def kernel(x_prompt, x_sample, cache_k_ctx, cache_v_ctx, c, c_ctx, norm1_g, norm2_g, w_ada, b_ada, w_in, b_in,
           w_fourier, w_na_o, rpb, w_out, w_router_group, b_router_group, w_router_expert, b_router_expert,
           w_exp_gate, w_exp_up, w_exp_down, final_norm_g):
    xp = x_prompt.reshape(T_CTX, D_MODEL)
    xs = x_sample.reshape(T_LAT, D_MODEL)
    c_all = jnp.concatenate([c_ctx[None, :], c, jnp.zeros((MOD_ROWS - 1 - DEC_BATCH, D_MODEL), f32)], axis=0)
    w_route = jnp.concatenate([w_router_group[0], w_router_expert[0]], axis=1)
    w_route = jnp.pad(w_route, ((0, 0), (0, ROUTE_LANES - w_route.shape[1]))).astype(bf16)
    b_route = jnp.concatenate([b_router_group[0], b_router_expert[0]])
    b_route = jnp.pad(b_route, (0, ROUTE_LANES - b_route.shape[0]))[None, :]

    mods = _ada(c_all, w_ada[0], b_ada[0][None, :])
    h1 = _modulate1(xp, xs, mods, norm1_g)
    p_all, k_new, v_new, (wg, wu, wd, wf, wn, wo) = _project(
        h1, w_in[0], b_in,
        (w_exp_gate.reshape(N_EXPERTS * D_MODEL, EXPERT_HIDDEN), w_exp_up.reshape(N_EXPERTS * D_MODEL, EXPERT_HIDDEN),
         w_exp_down.reshape(N_EXPERTS * EXPERT_HIDDEN, D_MODEL), w_fourier[0], w_na_o[0], w_out[0]))
    wg = wg.reshape(N_EXPERTS, D_MODEL, EXPERT_HIDDEN)
    wu = wu.reshape(N_EXPERTS, D_MODEL, EXPERT_HIDDEN)
    wd = wd.reshape(N_EXPERTS, EXPERT_HIDDEN, D_MODEL)

    tt = _bias_table(rpb[0])
    na_c = _attn_ctx(p_all)
    ck = cache_k_ctx.reshape(DEC_BATCH, PAST_LEN, NA_WIDTH)
    cv = cache_v_ctx.reshape(DEC_BATCH, PAST_LEN, NA_WIDTH)
    na_l = _attn_lat(p_all, ck, cv, tt)
    fm_c = _fourier(p_all, SEQ, BATCH, 0)
    fm_l = _fourier(p_all, DEC_SEQ, DEC_BATCH, T_CTX // DEC_SEQ)

    x1, h2c, info, counts = _merge(fm_c, fm_l, na_c, na_l, p_all, xp, xs, mods, norm2_g, wf, wn, wo,
                                   w_route, b_route)

    cnt = counts[0, :N_CLASSES].astype(jnp.int32)
    padded = ((cnt + MOE_TILE - 1) // MOE_TILE) * MOE_TILE
    off_end = jnp.cumsum(padded)
    off = off_end - padded
    ntiles = (off_end[-1] // MOE_TILE).astype(jnp.int32)
    tile_start = jnp.arange(MOE_MAX_TILES, dtype=jnp.int32) * MOE_TILE
    last_start = (ntiles - 1) * MOE_TILE
    tile_cls = jnp.sum(jnp.minimum(tile_start, last_start)[:, None] >= off_end[None, :], axis=1)
    pair = tile_cls % N_PAIRS
    group = tile_cls // N_PAIRS
    pair_hot = pair[:, None] == jnp.arange(N_PAIRS)[None, :]
    tile_ea = (group * EXPERTS_PER_GROUP + jnp.sum(pair_hot * jnp.asarray(PAIR_A), axis=1)).astype(jnp.int32)
    tile_eb = (group * EXPERTS_PER_GROUP + jnp.sum(pair_hot * jnp.asarray(PAIR_B), axis=1)).astype(jnp.int32)

    class_hot = info[:, 0:1] == jnp.arange(N_CLASSES, dtype=f32)[None, :]
    pos = (info[:, 1] + jnp.sum(jnp.where(class_hot, off.astype(f32)[None, :], 0.0), axis=1)).astype(jnp.int32)
    src = _inverse_positions(pos)
    ys = _moe(tile_ea, tile_eb, ntiles[None], src, h2c, wg, wu, wd, w_route, b_route)

    fg = final_norm_g[None, :]
    y_prompt = _final(pos, ys, x1, mods, fg, 0, T_CTX // FINAL_TILE)
    y_sample = _final(pos, ys, x1, mods, fg, T_CTX // FINAL_TILE, T_LAT // FINAL_TILE)

    shape_kv = (BATCH, 1, SEQ, NA_HEADS, HEAD_DIM)
    return (y_prompt.reshape(BATCH, SEQ, D_MODEL), y_sample.reshape(DEC_BATCH, DEC_SEQ, D_MODEL),
            k_new.reshape(shape_kv), v_new.reshape(shape_kv))
```

```python
import functools

import jax
import jax.numpy as jnp
import numpy as np
from jax import lax
from jax.experimental import pallas as pl
from jax.experimental.pallas import tpu as pltpu

D_MODEL = 2048
BATCH = 32
SEQ = 256
DEC_BATCH = 4
DEC_SEQ = 1024
PAST_LEN = 256
GRID_W = 64
GRID_ROWS = DEC_SEQ // GRID_W
NA_HEADS = 8
HEAD_DIM = 128
NA_WIDTH = NA_HEADS * HEAD_DIM
FOURIER_GROUPS = 4
FOURIER_GROUP_DIM = 256
FOURIER_WIDTH = FOURIER_GROUPS * FOURIER_GROUP_DIM
WIN_ROWS = 8
WIN_COLS = 16
N_GROUPS = 4
EXPERTS_PER_GROUP = 4
N_EXPERTS = N_GROUPS * EXPERTS_PER_GROUP
EXPERT_HIDDEN = 512
N_MOD = 6
IN_WIDTH = FOURIER_WIDTH + 3 * NA_WIDTH + 2 * D_MODEL
EPS = 1e-6
NEG_INF = -1e30

T_CTX = BATCH * SEQ
T_LAT = DEC_BATCH * DEC_SEQ
T_ALL = T_CTX + T_LAT

LANES = 128
CHUNKS = D_MODEL // LANES
MOD_ROWS = 8

PAIR_A = (0, 0, 0, 1, 1, 3)
PAIR_B = (1, 2, 3, 3, 2, 2)
N_PAIRS = len(PAIR_A)
N_CLASSES = N_GROUPS * N_PAIRS
MOE_TILE = 256
MOE_OUT_CHUNK = 512
MOE_AHEAD = 3
MOE_SLOTS = MOE_AHEAD + 1
MOE_MAX_TILES = -(-T_ALL // MOE_TILE) + N_CLASSES
FINAL_TILE = 256
FINAL_AHEAD = 3
FINAL_SLOTS = FINAL_AHEAD + 1
T_PAD = MOE_MAX_TILES * MOE_TILE

VMEM_LIMIT = 56 * 1024 * 1024

bf16 = jnp.bfloat16
f32 = jnp.float32


def _params(n_axes, vmem=VMEM_LIMIT):
    return pltpu.CompilerParams(dimension_semantics=("arbitrary",) * n_axes, vmem_limit_bytes=vmem)


def _mod_row(tile, n_ctx_tiles, tiles_per_request):
    return jnp.where(tile < n_ctx_tiles, 0, 1 + (tile - n_ctx_tiles) // tiles_per_request)


def _mod_vec(mods_ref, row, k):
    return mods_ref[pl.ds(row, 1), k * D_MODEL:(k + 1) * D_MODEL]


def _rms(x):
    return x * lax.rsqrt(jnp.mean(x * x, axis=-1, keepdims=True) + EPS)


def _ada_kernel(c_ref, w_ref, b_ref, o_ref):
    c = c_ref[...]
    s = (c * jax.nn.sigmoid(c)).astype(bf16)
    o_ref[...] = jnp.dot(s, w_ref[...].astype(bf16), preferred_element_type=f32) + b_ref[...]


def _ada(c_all, w_ada, b_ada):
    tn = 1024
    n = N_MOD * D_MODEL
    return pl.pallas_call(
        _ada_kernel,
        grid=(n // tn,),
        in_specs=[pl.BlockSpec((MOD_ROWS, D_MODEL), lambda j: (0, 0)),
                  pl.BlockSpec((D_MODEL, tn), lambda j: (0, j)),
                  pl.BlockSpec((1, tn), lambda j: (0, j))],
        out_specs=pl.BlockSpec((MOD_ROWS, tn), lambda j: (0, j)),
        out_shape=jax.ShapeDtypeStruct((MOD_ROWS, n), f32),
        compiler_params=_params(1),
        name="ada",
    )(c_all, w_ada, b_ada)


MOD_TILE = 1024
MOD_ROWS_PER_PASS = 128


def _mod_kernel(xp_ref, xs_ref, mods_ref, g_ref, o_ref):
    i = pl.program_id(0)
    n_ctx = T_CTX // MOD_TILE
    row = _mod_row(i, n_ctx, DEC_SEQ // MOD_TILE)
    sh = _mod_vec(mods_ref, row, 0)
    sc = _mod_vec(mods_ref, row, 1)

    gain = g_ref[...] * (1.0 + sc)

    def run(x_ref):
        def body(r, carry):
            rows = pl.ds(pl.multiple_of(r * MOD_ROWS_PER_PASS, MOD_ROWS_PER_PASS), MOD_ROWS_PER_PASS)
            o_ref[rows, :] = (_rms(x_ref[rows, :]) * gain + sh).astype(bf16)
            return carry

        lax.fori_loop(0, MOD_TILE // MOD_ROWS_PER_PASS, body, 0)

    @pl.when(i < n_ctx)
    def _():
        run(xp_ref)

    @pl.when(i >= n_ctx)
    def _():
        run(xs_ref)


def _modulate1(xp, xs, mods, g):
    n_ctx = T_CTX // MOD_TILE
    return pl.pallas_call(
        _mod_kernel,
        grid=(T_ALL // MOD_TILE,),
        in_specs=[pl.BlockSpec((MOD_TILE, D_MODEL), lambda i: (jnp.minimum(i, n_ctx - 1), 0)),
                  pl.BlockSpec((MOD_TILE, D_MODEL), lambda i: (jnp.maximum(i - n_ctx, 0), 0)),
                  pl.BlockSpec((MOD_ROWS, N_MOD * D_MODEL), lambda i: (0, 0)),
                  pl.BlockSpec((1, D_MODEL), lambda i: (0, 0))],
        out_specs=pl.BlockSpec((MOD_TILE, D_MODEL), lambda i: (i, 0)),
        out_shape=jax.ShapeDtypeStruct((T_ALL, D_MODEL), bf16),
        compiler_params=_params(1),
        name="modulate1",
    )(xp, xs, mods, g)


PROJ_TM = 1024
PROJ_TN = 1024
PROJ_CHUNK = 256
K_COL = (FOURIER_WIDTH + NA_WIDTH) // PROJ_TN
V_COL = (FOURIER_WIDTH + 2 * NA_WIDTH) // PROJ_TN


PROJ_STEPS = (IN_WIDTH // PROJ_TN) * (T_ALL // PROJ_TM)
CAST_SHAPES = ((N_EXPERTS * D_MODEL, EXPERT_HIDDEN, 512),
               (N_EXPERTS * D_MODEL, EXPERT_HIDDEN, 512),
               (N_EXPERTS * EXPERT_HIDDEN, D_MODEL, 128),
               (FOURIER_WIDTH, D_MODEL, 16),
               (NA_WIDTH, D_MODEL, 16),
               (D_MODEL, D_MODEL, 32))
N_CAST = len(CAST_SHAPES)


def _kv_copy(acc_ref, dst_hbm, tile, sem):
    rows = pl.ds(pl.multiple_of(tile * PROJ_TM, PROJ_TM), PROJ_TM)
    return pltpu.make_async_copy(acc_ref.at[tile % 2], dst_hbm.at[rows, :], sem)


def _proj_kernel(h_ref, w_ref, b_ref, *rest):
    cast_in = rest[:N_CAST]
    p_ref, k_hbm, v_hbm = rest[N_CAST:N_CAST + 3]
    cast_out = rest[N_CAST + 3:2 * N_CAST + 3]
    wb_ref, acc_ref, sem = rest[-3:]
    j = pl.program_id(0)
    i = pl.program_id(1)
    n_ctx = T_CTX // PROJ_TM
    is_k = (j == K_COL) & (i < n_ctx)
    is_v = (j == V_COL) & (i < n_ctx)

    @pl.when(i == 0)
    def _():
        wb_ref[...] = w_ref[...].astype(bf16)

    n_chunks = PROJ_TN // PROJ_CHUNK
    for c in range(n_chunks):
        cols = slice(c * PROJ_CHUNK, (c + 1) * PROJ_CHUNK)
        acc = jnp.dot(h_ref[...], wb_ref[:, cols], preferred_element_type=f32) + b_ref[:, cols]
        p_ref[:, cols] = acc.astype(bf16)
        acc_ref[i % 2, :, cols] = acc
        for src, dst in list(zip(cast_in, cast_out))[c::n_chunks]:
            dst[...] = src[...].astype(bf16)

    @pl.when(((j == K_COL) | (j == V_COL)) & (i >= 1) & (i <= n_ctx))
    def _():
        _kv_copy(acc_ref, k_hbm, i - 1, sem).wait()

    @pl.when(is_k)
    def _():
        _kv_copy(acc_ref, k_hbm, i, sem).start()

    @pl.when(is_v)
    def _():
        _kv_copy(acc_ref, v_hbm, i, sem).start()


def _cast_spec(rows, cols, rb):
    m = T_ALL // PROJ_TM
    return pl.BlockSpec((rb, cols), lambda j, i: (jnp.minimum(j * m + i, rows // rb - 1), 0))


def _project(h1, w_in, b_in, cast_weights):
    assert all(rows // rb <= PROJ_STEPS for rows, _, rb in CAST_SHAPES)
    assert T_CTX // PROJ_TM < T_ALL // PROJ_TM
    cast_specs = [_cast_spec(*s) for s in CAST_SHAPES]
    hbm = pl.BlockSpec(memory_space=pl.ANY)
    outs = pl.pallas_call(
        _proj_kernel,
        grid=(IN_WIDTH // PROJ_TN, T_ALL // PROJ_TM),
        in_specs=[pl.BlockSpec((PROJ_TM, D_MODEL), lambda j, i: (i, 0)),
                  pl.BlockSpec((D_MODEL, PROJ_TN), lambda j, i: (0, j)),
                  pl.BlockSpec((1, PROJ_TN), lambda j, i: (0, j))] + cast_specs,
        out_specs=[pl.BlockSpec((PROJ_TM, PROJ_TN), lambda j, i: (i, j)), hbm, hbm] + cast_specs,
        out_shape=[jax.ShapeDtypeStruct((T_ALL, IN_WIDTH), bf16),
                   jax.ShapeDtypeStruct((T_CTX, NA_WIDTH), f32),
                   jax.ShapeDtypeStruct((T_CTX, NA_WIDTH), f32)]
                  + [jax.ShapeDtypeStruct((rows, cols), bf16) for rows, cols, _ in CAST_SHAPES],
        scratch_shapes=[pltpu.VMEM((D_MODEL, PROJ_TN), bf16), pltpu.VMEM((2, PROJ_TM, PROJ_TN), f32),
                        pltpu.SemaphoreType.DMA(())],
        compiler_params=_params(2),
        name="project",
    )(h1, w_in, b_in, *cast_weights)
    return outs[0], outs[1], outs[2], outs[3:]


Q_GROUP_ROWS = 4
N_Q_GROUPS = GRID_ROWS // Q_GROUP_ROWS
Q_GROUP = Q_GROUP_ROWS * GRID_W
KEY_SPAN_ROWS = 12
KEY_SPAN = KEY_SPAN_ROWS * GRID_W


def _window_start(r):
    return min(max(r - WIN_ROWS // 2, 0), GRID_ROWS - WIN_ROWS)


def _key_base(g):
    lo = _window_start(g * Q_GROUP_ROWS)
    hi = _window_start((g + 1) * Q_GROUP_ROWS - 1) + WIN_ROWS
    assert hi - lo <= KEY_SPAN_ROWS
    return min(lo, GRID_ROWS - KEY_SPAN_ROWS)


def _visible_span(g):
    base = _key_base(g)
    first = _window_start(g * Q_GROUP_ROWS) - base
    last = _window_start((g + 1) * Q_GROUP_ROWS - 1) + WIN_ROWS - base
    lo = (first // 2) * 2 * GRID_W
    hi = -(-last // 2) * 2 * GRID_W
    return lo, hi


def _block_shift(g, rr):
    return _key_base(g) - (g * Q_GROUP_ROWS + rr) + WIN_ROWS - 1


TABLE_PAD = -min(_block_shift(g, rr) for g in range(N_Q_GROUPS) for rr in range(Q_GROUP_ROWS))
TABLE_BLOCKS = 2 * (-(-(max(_block_shift(g, rr) for g in range(N_Q_GROUPS) for rr in range(Q_GROUP_ROWS))
                        + TABLE_PAD + KEY_SPAN_ROWS + 1) // 2))
TABLE_LANES = TABLE_BLOCKS * GRID_W


def _bias_kernel(r0_ref, r1_ref, o_ref):
    qc = lax.broadcasted_iota(jnp.int32, (GRID_W, TABLE_LANES), 0)
    lane = lax.broadcasted_iota(jnp.int32, (GRID_W, TABLE_LANES), 1)
    kc = lane & (GRID_W - 1)
    dc = kc - qc + (WIN_COLS - 1)
    start = jnp.clip(qc - WIN_COLS // 2, 0, GRID_W - WIN_COLS)
    ok = (kc >= start) & (kc < start + WIN_COLS)
    tables = []
    for r_ref in (r0_ref, r1_ref):
        acc = jnp.zeros((GRID_W, TABLE_LANES), f32)
        for b in range(2 * WIN_COLS - 1):
            acc = jnp.where(dc == b, r_ref[0, b:b + 1, :], acc)
        tables.append(jnp.where(ok, acc, NEG_INF))
    span_row = jnp.right_shift(lax.broadcasted_iota(jnp.int32, (GRID_W, KEY_SPAN), 1), GRID_W.bit_length() - 1)
    for g in range(N_Q_GROUPS):
        for rr in range(Q_GROUP_ROWS):
            first = _window_start(g * Q_GROUP_ROWS + rr) - _key_base(g)
            e = _block_shift(g, rr) + TABLE_PAD
            lo = (e - e % 2) * GRID_W
            in_window = (span_row >= first) & (span_row < first + WIN_ROWS)
            o_ref[0, g, rr * GRID_W:(rr + 1) * GRID_W, :] = jnp.where(
                in_window, tables[e % 2][:, lo:lo + KEY_SPAN], NEG_INF)


def _bias_table(rpb):
    nb = 2 * WIN_COLS - 1
    n_off = 2 * WIN_ROWS - 1
    by_row = jnp.transpose(rpb, (0, 2, 1))
    rpbx = []
    for p in range(2):
        spread = np.zeros((n_off, TABLE_LANES), np.float32)
        for t in range(TABLE_BLOCKS):
            if 0 <= t + p - TABLE_PAD < n_off:
                spread[t + p - TABLE_PAD, t * GRID_W:(t + 1) * GRID_W] = 1.0
        rpbx.append(jnp.einsum("hba,al->hbl", by_row, jnp.asarray(spread), precision=lax.Precision.HIGHEST))
    table_spec = pl.BlockSpec((1, nb, TABLE_LANES), lambda h: (h, 0, 0))
    return pl.pallas_call(
        _bias_kernel,
        grid=(NA_HEADS,),
        in_specs=[table_spec, table_spec],
        out_specs=pl.BlockSpec((1, N_Q_GROUPS, Q_GROUP, KEY_SPAN), lambda h: (h, 0, 0, 0)),
        out_shape=jax.ShapeDtypeStruct((NA_HEADS, N_Q_GROUPS, Q_GROUP, KEY_SPAN), f32),
        compiler_params=_params(1),
        name="bias_table",
    )(*rpbx)


def _qk(q, k):
    return lax.dot_general(q, k, (((1,), (1,)), ((), ())), preferred_element_type=f32)


def _attn_ctx_kernel(q_ref, k_ref, v_ref, o_ref):
    scale = HEAD_DIM ** -0.5
    for h in range(NA_HEADS):
        sl = slice(h * HEAD_DIM, (h + 1) * HEAD_DIM)
        s = _qk(q_ref[:, sl], k_ref[:, sl]) * scale
        p = jnp.exp(s - jnp.max(s, axis=-1, keepdims=True))
        l = jnp.sum(p, axis=-1, keepdims=True)
        o = jnp.dot(p.astype(bf16), v_ref[:, sl], preferred_element_type=f32) / l
        o_ref[:, sl] = o.astype(bf16)


def _attn_ctx(p_all):
    qcol = FOURIER_WIDTH // NA_WIDTH
    return pl.pallas_call(
        _attn_ctx_kernel,
        grid=(BATCH,),
        in_specs=[pl.BlockSpec((SEQ, NA_WIDTH), lambda b: (b, qcol)),
                  pl.BlockSpec((SEQ, NA_WIDTH), lambda b: (b, qcol + 1)),
                  pl.BlockSpec((SEQ, NA_WIDTH), lambda b: (b, qcol + 2))],
        out_specs=pl.BlockSpec((SEQ, NA_WIDTH), lambda b: (b, 0)),
        out_shape=jax.ShapeDtypeStruct((T_CTX, NA_WIDTH), bf16),
        compiler_params=_params(1),
        name="attn_ctx",
    )(p_all, p_all, p_all)


def _attn_lat_kernel(q_ref, k_ref, v_ref, kc_ref, vc_ref, bias_ref, o_ref):
    scale = HEAD_DIM ** -0.5
    head_rows = pl.ds(pl.program_id(0), PAST_LEN, stride=NA_HEADS)
    kcb = kc_ref[0, head_rows, :].astype(bf16)
    vcb = vc_ref[0, head_rows, :].astype(bf16)
    for g in range(N_Q_GROUPS):
        lo, hi = _visible_span(g)
        k0 = _key_base(g) * GRID_W + lo
        nk = hi - lo
        q = q_ref[g * Q_GROUP:(g + 1) * Q_GROUP, :]
        s_nb = _qk(q, k_ref[k0:k0 + nk, :]) * scale + bias_ref[0, g, :, lo:hi]
        s_cx = _qk(q, kcb) * scale
        m = jnp.maximum(jnp.max(s_nb, axis=-1, keepdims=True), jnp.max(s_cx, axis=-1, keepdims=True))
        p_nb = jnp.exp(s_nb - m)
        p_cx = jnp.exp(s_cx - m)
        l = jnp.sum(p_nb, axis=-1, keepdims=True) + jnp.sum(p_cx, axis=-1, keepdims=True)
        o = (jnp.dot(p_nb.astype(bf16), v_ref[k0:k0 + nk, :], preferred_element_type=f32)
             + jnp.dot(p_cx.astype(bf16), vcb, preferred_element_type=f32)) / l
        o_ref[g * Q_GROUP:(g + 1) * Q_GROUP, :] = o.astype(bf16)


def _attn_lat(p_all, ck, cv, bias):
    row0 = T_CTX // DEC_SEQ
    qcol = FOURIER_WIDTH // HEAD_DIM
    return pl.pallas_call(
        _attn_lat_kernel,
        grid=(NA_HEADS, DEC_BATCH),
        in_specs=[pl.BlockSpec((DEC_SEQ, HEAD_DIM), lambda h, b: (row0 + b, qcol + h)),
                  pl.BlockSpec((DEC_SEQ, HEAD_DIM), lambda h, b: (row0 + b, qcol + NA_HEADS + h)),
                  pl.BlockSpec((DEC_SEQ, HEAD_DIM), lambda h, b: (row0 + b, qcol + 2 * NA_HEADS + h)),
                  pl.BlockSpec((1, PAST_LEN * NA_HEADS, HEAD_DIM), lambda h, b: (b, 0, 0)),
                  pl.BlockSpec((1, PAST_LEN * NA_HEADS, HEAD_DIM), lambda h, b: (b, 0, 0)),
                  pl.BlockSpec((1, N_Q_GROUPS, Q_GROUP, KEY_SPAN), lambda h, b: (h, 0, 0, 0))],
        out_specs=pl.BlockSpec((DEC_SEQ, HEAD_DIM), lambda h, b: (b, h)),
        out_shape=jax.ShapeDtypeStruct((T_LAT, NA_WIDTH), bf16),
        compiler_params=_params(2),
        name="attn_lat",
    )(p_all, p_all, p_all, ck, cv, bias)


def _dft_consts(n):
    def cs(m):
        idx = (np.arange(m)[:, None] * np.arange(m)[None, :]) % m
        ang = 2.0 * np.pi * idx.astype(np.float64) / m
        return np.cos(ang), np.sin(ang)

    cc, sc = cs(FOURIER_GROUP_DIM)
    cn, sn = cs(n)
    chan = np.concatenate([cc, sc], axis=1) / np.sqrt(FOURIER_GROUP_DIM)
    pos = np.concatenate([cn, -sn], axis=1) / np.sqrt(n)
    return jnp.asarray(chan, f32).astype(bf16), jnp.asarray(pos, f32).astype(bf16)


def _fourier_kernel(u_ref, chan_ref, pos_ref, o_ref):
    gd = FOURIER_GROUP_DIM
    ys = [jnp.dot(u_ref[:, g * gd:(g + 1) * gd], chan_ref[...], preferred_element_type=f32)
          for g in range(FOURIER_GROUPS)]
    yc = jnp.concatenate([y[:, :gd] for y in ys], axis=1)
    ysn = jnp.concatenate([y[:, gd:] for y in ys], axis=1)
    stacked = jnp.concatenate([yc, ysn], axis=0).astype(bf16)
    o_ref[...] = jnp.dot(pos_ref[...], stacked, preferred_element_type=f32).astype(bf16)


def _fourier(p_all, n, n_req, row0):
    chan, pos = _dft_consts(n)
    return pl.pallas_call(
        _fourier_kernel,
        grid=(n_req,),
        in_specs=[pl.BlockSpec((n, FOURIER_WIDTH), lambda b: (row0 + b, 0)),
                  pl.BlockSpec((FOURIER_GROUP_DIM, 2 * FOURIER_GROUP_DIM), lambda b: (0, 0)),
                  pl.BlockSpec((n, 2 * n), lambda b: (0, 0))],
        out_specs=pl.BlockSpec((n, FOURIER_WIDTH), lambda b: (b, 0)),
        out_shape=jax.ShapeDtypeStruct((n_req * n, FOURIER_WIDTH), bf16),
        compiler_params=_params(1),
        name=f"fourier_{n}",
    )(p_all, chan, pos)


MERGE_TM = 256
ROUTE_LANES = LANES


def _class_of(logits):
    lane = lax.broadcasted_iota(jnp.int32, logits.shape, 1)
    big = jnp.int32(ROUTE_LANES)
    is_group = lane < N_GROUPS
    mg = jnp.max(jnp.where(is_group, logits, -jnp.inf), axis=-1, keepdims=True)
    gsel = jnp.min(jnp.where(is_group & (logits == mg), lane, big), axis=-1, keepdims=True)
    lo = N_GROUPS + EXPERTS_PER_GROUP * gsel
    in_group = (lane >= lo) & (lane < lo + EXPERTS_PER_GROUP)
    m1 = jnp.max(jnp.where(in_group, logits, -jnp.inf), axis=-1, keepdims=True)
    i1 = jnp.min(jnp.where(in_group & (logits == m1), lane, big), axis=-1, keepdims=True)
    rest = in_group & (lane != i1)
    m2 = jnp.max(jnp.where(rest, logits, -jnp.inf), axis=-1, keepdims=True)
    i2 = jnp.min(jnp.where(rest & (logits == m2), lane, big), axis=-1, keepdims=True)
    e_lo = jnp.minimum(i1, i2) - lo
    e_hi = jnp.maximum(i1, i2) - lo
    pair = jnp.zeros_like(e_lo)
    for idx in range(N_PAIRS):
        a, b = sorted((PAIR_A[idx], PAIR_B[idx]))
        pair = jnp.where((e_lo == a) & (e_hi == b), idx, pair)
    return gsel * N_PAIRS + pair


def _merge_kernel(fmc_ref, fml_ref, nac_ref, nal_ref, ga_ref, gb_ref, xp_ref, xs_ref, mods_ref, g2_ref,
                  wf_ref, wn_ref, wo_ref, wr_ref, br_ref,
                  x1_ref, h2c_ref, info_ref, cnt_ref, carry_ref, x1s_ref):
    s = pl.program_id(0)
    n_ctx = T_CTX // MERGE_TM
    n_all = T_ALL // MERGE_TM
    per_req = DEC_SEQ // MERGE_TM
    row = _mod_row(jnp.minimum(s, n_all - 1), n_ctx, per_req)
    prev_row = _mod_row(jnp.maximum(s - 1, 0), n_ctx, per_req)

    @pl.when(s == 0)
    def _():
        carry_ref[...] = jnp.zeros_like(carry_ref)
        x1s_ref[1] = jnp.zeros((MERGE_TM, D_MODEL), f32)

    def step(tile_refs):
        x1p = x1s_ref[(s + 1) % 2]
        if tile_refs:
            fm_ref, na_ref, x_ref = tile_refs
            a = jnp.dot(fm_ref[...], wf_ref[...], preferred_element_type=f32)

        h2 = _rms(x1p) * g2_ref[...] * (1.0 + _mod_vec(mods_ref, prev_row, 4)) + _mod_vec(mods_ref, prev_row, 3)
        if tile_refs:
            gate_a = jax.nn.sigmoid(ga_ref[...].astype(f32))
            b = jnp.dot(na_ref[...], wn_ref[...], preferred_element_type=f32)

        for k in range(CHUNKS):
            h2c_ref[pl.ds(k, MERGE_TM, stride=CHUNKS), :] = h2[:, k * LANES:(k + 1) * LANES]
        logits = jnp.dot(h2.astype(bf16), wr_ref[...], preferred_element_type=f32) + br_ref[...]
        if tile_refs:
            merged = gate_a * a + jax.nn.sigmoid(gb_ref[...].astype(f32)) * b
            out = jnp.dot(merged.astype(bf16), wo_ref[...], preferred_element_type=f32)

        cls = _class_of(logits)
        lane = lax.broadcasted_iota(jnp.int32, logits.shape, 1)
        onehot = ((lane == cls) & (s > 0)).astype(f32)
        r = lax.broadcasted_iota(jnp.int32, (MERGE_TM, MERGE_TM), 0)
        c = lax.broadcasted_iota(jnp.int32, (MERGE_TM, MERGE_TM), 1)
        before = (c < r).astype(bf16)
        prefix = jnp.dot(before, onehot.astype(bf16), preferred_element_type=f32) + carry_ref[...]
        rank = jnp.sum(onehot * prefix, axis=-1, keepdims=True)
        info_ref[...] = jnp.where(lane == 0, cls.astype(f32), jnp.where(lane == 1, rank, 0.0))
        carry_ref[...] = carry_ref[...] + jnp.sum(onehot, axis=0, keepdims=True)
        cnt_ref[...] = jnp.broadcast_to(carry_ref[...], cnt_ref.shape)
        if tile_refs:
            x1 = x_ref[...] + _mod_vec(mods_ref, row, 2) * out
            x1_ref[...] = x1
            x1s_ref[s % 2] = x1

    @pl.when(s < n_ctx)
    def _():
        step((fmc_ref, nac_ref, xp_ref))

    @pl.when((s >= n_ctx) & (s < n_all))
    def _():
        step((fml_ref, nal_ref, xs_ref))

    @pl.when(s == n_all)
    def _():
        step(None)


def _merge(fm_c, fm_l, na_c, na_l, p_all, xp, xs, mods, g2, wf, wn, wo, wr, br):
    n_ctx = T_CTX // MERGE_TM
    n_lat = T_LAT // MERGE_TM
    n_all = n_ctx + n_lat
    gcol = (FOURIER_WIDTH + 3 * NA_WIDTH) // D_MODEL
    const = lambda i: (0, 0)
    ctx_tile = lambda i: (jnp.minimum(i, n_ctx - 1), 0)
    lat_tile = lambda i: (jnp.clip(i - n_ctx, 0, n_lat - 1), 0)
    this_tile = lambda i: jnp.minimum(i, n_all - 1)
    prev_tile = lambda i: (jnp.maximum(i - 1, 0), 0)
    once = pl.Buffered(1)
    return pl.pallas_call(
        _merge_kernel,
        grid=(n_all + 1,),
        in_specs=[pl.BlockSpec((MERGE_TM, FOURIER_WIDTH), ctx_tile),
                  pl.BlockSpec((MERGE_TM, FOURIER_WIDTH), lat_tile),
                  pl.BlockSpec((MERGE_TM, NA_WIDTH), ctx_tile),
                  pl.BlockSpec((MERGE_TM, NA_WIDTH), lat_tile),
                  pl.BlockSpec((MERGE_TM, D_MODEL), lambda i: (this_tile(i), gcol)),
                  pl.BlockSpec((MERGE_TM, D_MODEL), lambda i: (this_tile(i), gcol + 1)),
                  pl.BlockSpec((MERGE_TM, D_MODEL), ctx_tile),
                  pl.BlockSpec((MERGE_TM, D_MODEL), lat_tile),
                  pl.BlockSpec((MOD_ROWS, N_MOD * D_MODEL), const, pipeline_mode=once),
                  pl.BlockSpec((1, D_MODEL), const, pipeline_mode=once),
                  pl.BlockSpec((FOURIER_WIDTH, D_MODEL), const, pipeline_mode=once),
                  pl.BlockSpec((NA_WIDTH, D_MODEL), const, pipeline_mode=once),
                  pl.BlockSpec((D_MODEL, D_MODEL), const, pipeline_mode=once),
                  pl.BlockSpec((D_MODEL, ROUTE_LANES), const, pipeline_mode=once),
                  pl.BlockSpec((1, ROUTE_LANES), const, pipeline_mode=once)],
        out_specs=[pl.BlockSpec((MERGE_TM, D_MODEL), lambda i: (this_tile(i), 0)),
                   pl.BlockSpec((MERGE_TM * CHUNKS, LANES), prev_tile),
                   pl.BlockSpec((MERGE_TM, ROUTE_LANES), prev_tile),
                   pl.BlockSpec((8, ROUTE_LANES), const)],
        out_shape=[jax.ShapeDtypeStruct((T_ALL, D_MODEL), f32),
                   jax.ShapeDtypeStruct((T_ALL * CHUNKS, LANES), f32),
                   jax.ShapeDtypeStruct((T_ALL, ROUTE_LANES), f32),
                   jax.ShapeDtypeStruct((8, ROUTE_LANES), f32)],
        scratch_shapes=[pltpu.VMEM((1, ROUTE_LANES), f32), pltpu.VMEM((2, MERGE_TM, D_MODEL), f32)],
        compiler_params=_params(1),
        name="merge",
    )(fm_c, fm_l, na_c, na_l, p_all, p_all, xp, xs, mods, g2, wf, wn, wo, wr, br)


def _perm_kernel(pos_ref, src_ref):
    def clear(p, carry):
        src_ref[p] = 0
        return carry

    lax.fori_loop(0, T_PAD, clear, 0, unroll=16)

    def place(t, carry):
        src_ref[pos_ref[t]] = t
        return carry

    lax.fori_loop(0, T_ALL, place, 0, unroll=16)


def _inverse_positions(pos):
    smem = pl.BlockSpec(memory_space=pltpu.SMEM)
    return pl.pallas_call(
        _perm_kernel,
        in_specs=[smem],
        out_specs=smem,
        out_shape=jax.ShapeDtypeStruct((T_PAD,), jnp.int32),
        name="permutation",
    )(pos)


def _row_copy(idx_ref, base, src_hbm, buf, sem, slot, k):
    tok = idx_ref[base + k]
    return pltpu.make_async_copy(src_hbm.at[pl.ds(pl.multiple_of(tok * CHUNKS, CHUNKS), CHUNKS), :],
                                 buf.at[slot, pl.ds(pl.multiple_of(k * CHUNKS, CHUNKS), CHUNKS), :],
                                 sem.at[slot])


def _start_gather(idx_ref, base, src_hbm, buf, sem, slot, rows, queues=(0, 1)):
    def body(k8, carry):
        for j in range(8):
            _row_copy(idx_ref, base, src_hbm, buf, sem, slot, k8 * 8 + j).start(priority=queues[j % len(queues)])
        return carry

    lax.fori_loop(0, rows // 8, body, 0)


def _wait_gather(buf, sem, slot):
    pltpu.make_async_copy(buf.at[slot], buf.at[slot], sem.at[slot]).wait()


def _gathered_rows(buf, slot, rows):
    return jnp.concatenate([buf[slot, pl.ds(s, rows, stride=CHUNKS), :] for s in range(CHUNKS)], axis=1)


def _moe_kernel(ea_ref, eb_ref, nt_ref, src_ref, h2c_hbm, wga_ref, wua_ref, wda_ref, wgb_ref, wub_ref, wdb_ref,
                wr_ref, br_ref, o_ref, buf, sem):
    i = pl.program_id(0)
    nt = nt_ref[0]

    @pl.when(i >= nt)
    def _():
        o_ref[...] = jnp.zeros_like(o_ref)

    def fetch(tile):
        _start_gather(src_ref, tile * MOE_TILE, h2c_hbm, buf, sem, tile % MOE_SLOTS, MOE_TILE, queues=(1,))

    @pl.when(i == 0)
    def _():
        for t in range(MOE_AHEAD):
            @pl.when(t < nt)
            def _(t=t):
                fetch(t)

    @pl.when(i + MOE_AHEAD < nt)
    def _():
        fetch(i + MOE_AHEAD)

    @pl.when(i < nt)
    def _():
        slot = i % MOE_SLOTS
        _wait_gather(buf, sem, slot)
        x = _gathered_rows(buf, slot, MOE_TILE).astype(bf16)
        logits = jnp.dot(x, wr_ref[...], preferred_element_type=f32) + br_ref[...]
        lane = lax.broadcasted_iota(jnp.int32, logits.shape, 1)
        ea = ea_ref[i]
        eb = eb_ref[i]

        def pick(col):
            return jnp.sum(jnp.where(lane == col, logits, 0.0), axis=-1, keepdims=True)

        lg = pick(ea // EXPERTS_PER_GROUP)
        p_group = 1.0 / jnp.sum(jnp.where(lane < N_GROUPS, jnp.exp(logits - lg), 0.0), axis=-1, keepdims=True)
        la = pick(N_GROUPS + ea)
        lb = pick(N_GROUPS + eb)
        w_a = p_group / (1.0 + jnp.exp(lb - la))
        w_b = p_group / (1.0 + jnp.exp(la - lb))

        def up(wg_ref, wu_ref):
            return (jnp.dot(x, wg_ref[0], preferred_element_type=f32),
                    jnp.dot(x, wu_ref[0], preferred_element_type=f32))

        def act(g, u):
            return (g * jax.nn.sigmoid(g) * u).astype(bf16)

        g_a, u_a = up(wga_ref, wua_ref)
        g_b, u_b = up(wgb_ref, wub_ref)
        hid_a = act(g_a, u_a)
        hid_b = act(g_b, u_b)
        per_chunk = MOE_OUT_CHUNK // LANES
        for c in range(D_MODEL // MOE_OUT_CHUNK):
            cols = slice(c * MOE_OUT_CHUNK, (c + 1) * MOE_OUT_CHUNK)
            y = (w_a * jnp.dot(hid_a, wda_ref[0, :, cols], preferred_element_type=f32)
                 + w_b * jnp.dot(hid_b, wdb_ref[0, :, cols], preferred_element_type=f32))
            for k in range(per_chunk):
                o_ref[pl.ds(c * per_chunk + k, MOE_TILE, stride=CHUNKS), :] = y[:, k * LANES:(k + 1) * LANES]


def _moe(tile_ea, tile_eb, ntiles, src, h2c, wg, wu, wd, wr, br):
    up = lambda tab: (lambda i, ea, eb, nt, src: (tab(ea, eb)[i], 0, 0))
    pick_a = lambda ea, eb: ea
    pick_b = lambda ea, eb: eb
    const = lambda i, ea, eb, nt, src: (0, 0)
    grid_spec = pltpu.PrefetchScalarGridSpec(
        num_scalar_prefetch=4,
        grid=(MOE_MAX_TILES,),
        in_specs=[pl.BlockSpec(memory_space=pl.ANY),
                  pl.BlockSpec((1, D_MODEL, EXPERT_HIDDEN), up(pick_a)),
                  pl.BlockSpec((1, D_MODEL, EXPERT_HIDDEN), up(pick_a)),
                  pl.BlockSpec((1, EXPERT_HIDDEN, D_MODEL), up(pick_a)),
                  pl.BlockSpec((1, D_MODEL, EXPERT_HIDDEN), up(pick_b)),
                  pl.BlockSpec((1, D_MODEL, EXPERT_HIDDEN), up(pick_b)),
                  pl.BlockSpec((1, EXPERT_HIDDEN, D_MODEL), up(pick_b)),
                  pl.BlockSpec((D_MODEL, ROUTE_LANES), const),
                  pl.BlockSpec((1, ROUTE_LANES), const)],
        out_specs=pl.BlockSpec((MOE_TILE * CHUNKS, LANES), lambda i, ea, eb, nt, src: (i, 0)),
        scratch_shapes=[pltpu.VMEM((MOE_SLOTS, MOE_TILE * CHUNKS, LANES), f32),
                        pltpu.SemaphoreType.DMA((MOE_SLOTS,))],
    )
    return pl.pallas_call(
        _moe_kernel,
        grid_spec=grid_spec,
        out_shape=jax.ShapeDtypeStruct((T_PAD * CHUNKS, LANES), f32),
        compiler_params=_params(1),
        name="moe",
    )(tile_ea, tile_eb, ntiles, src, h2c, wg, wu, wd, wg, wu, wd, wr, br)


def _final_kernel(pos_ref, ys_hbm, x1_ref, mods_ref, g_ref, o_ref, buf, sem, *, tile0, n_tiles, n_ctx_tiles):
    i = pl.program_id(0)

    def fetch(tile):
        _start_gather(pos_ref, (tile0 + tile) * FINAL_TILE, ys_hbm, buf, sem, tile % FINAL_SLOTS, FINAL_TILE)

    @pl.when(i == 0)
    def _():
        for t in range(min(FINAL_AHEAD, n_tiles)):
            fetch(t)

    @pl.when(i + FINAL_AHEAD < n_tiles)
    def _():
        fetch(i + FINAL_AHEAD)

    slot = i % FINAL_SLOTS
    _wait_gather(buf, sem, slot)
    y = _gathered_rows(buf, slot, FINAL_TILE)
    row = _mod_row(tile0 + i, n_ctx_tiles, DEC_SEQ // FINAL_TILE)
    x2 = x1_ref[...] + _mod_vec(mods_ref, row, 5) * y
    o_ref[...] = _rms(x2) * g_ref[...]


def _final(pos, ys, x1, mods, g, tile0, n_tiles):
    body = functools.partial(_final_kernel, tile0=tile0, n_tiles=n_tiles, n_ctx_tiles=T_CTX // FINAL_TILE)
    grid_spec = pltpu.PrefetchScalarGridSpec(
        num_scalar_prefetch=1,
        grid=(n_tiles,),
        in_specs=[pl.BlockSpec(memory_space=pl.ANY),
                  pl.BlockSpec((FINAL_TILE, D_MODEL), lambda i, pos: (tile0 + i, 0)),
                  pl.BlockSpec((MOD_ROWS, N_MOD * D_MODEL), lambda i, pos: (0, 0)),
                  pl.BlockSpec((1, D_MODEL), lambda i, pos: (0, 0))],
        out_specs=pl.BlockSpec((FINAL_TILE, D_MODEL), lambda i, pos: (i, 0)),
        scratch_shapes=[pltpu.VMEM((FINAL_SLOTS, FINAL_TILE * CHUNKS, LANES), f32),
                        pltpu.SemaphoreType.DMA((FINAL_SLOTS,))],
    )
    return pl.pallas_call(
        body,
        grid_spec=grid_spec,
        out_shape=jax.ShapeDtypeStruct((n_tiles * FINAL_TILE, D_MODEL), f32),
        compiler_params=_params(1),
        name=f"final_{tile0}",
    )(pos, ys, x1, mods, g)


def kernel(x_prompt, x_sample, cache_k_ctx, cache_v_ctx, c, c_ctx, norm1_g, norm2_g, w_ada, b_ada, w_in, b_in,
           w_fourier, w_na_o, rpb, w_out, w_router_group, b_router_group, w_router_expert, b_router_expert,
           w_exp_gate, w_exp_up, w_exp_down, final_norm_g):
    xp = x_prompt.reshape(T_CTX, D_MODEL)
    xs = x_sample.reshape(T_LAT, D_MODEL)
    c_all = jnp.concatenate([c_ctx[None, :], c, jnp.zeros((MOD_ROWS - 1 - DEC_BATCH, D_MODEL), f32)], axis=0)
    w_route = jnp.concatenate([w_router_group[0], w_router_expert[0]], axis=1)
    w_route = jnp.pad(w_route, ((0, 0), (0, ROUTE_LANES - w_route.shape[1]))).astype(bf16)
    b_route = jnp.concatenate([b_router_group[0], b_router_expert[0]])
    b_route = jnp.pad(b_route, (0, ROUTE_LANES - b_route.shape[0]))[None, :]

    mods = _ada(c_all, w_ada[0], b_ada[0][None, :])
    h1 = _modulate1(xp, xs, mods, norm1_g)
    p_all, k_new, v_new, (wg, wu, wd, wf, wn, wo) = _project(
        h1, w_in[0], b_in,
        (w_exp_gate.reshape(N_EXPERTS * D_MODEL, EXPERT_HIDDEN), w_exp_up.reshape(N_EXPERTS * D_MODEL, EXPERT_HIDDEN),
         w_exp_down.reshape(N_EXPERTS * EXPERT_HIDDEN, D_MODEL), w_fourier[0], w_na_o[0], w_out[0]))
    wg = wg.reshape(N_EXPERTS, D_MODEL, EXPERT_HIDDEN)
    wu = wu.reshape(N_EXPERTS, D_MODEL, EXPERT_HIDDEN)
    wd = wd.reshape(N_EXPERTS, EXPERT_HIDDEN, D_MODEL)

    tt = _bias_table(rpb[0])
    na_c = _attn_ctx(p_all)
    ck = cache_k_ctx.reshape(DEC_BATCH, PAST_LEN * NA_HEADS, HEAD_DIM)
    cv = cache_v_ctx.reshape(DEC_BATCH, PAST_LEN * NA_HEADS, HEAD_DIM)
    na_l = _attn_lat(p_all, ck, cv, tt)
    fm_c = _fourier(p_all, SEQ, BATCH, 0)
    fm_l = _fourier(p_all, DEC_SEQ, DEC_BATCH, T_CTX // DEC_SEQ)

    x1, h2c, info, counts = _merge(fm_c, fm_l, na_c, na_l, p_all, xp, xs, mods, norm2_g, wf, wn, wo,
                                   w_route, b_route)

    cnt = counts[0, :N_CLASSES].astype(jnp.int32)
    padded = ((cnt + MOE_TILE - 1) // MOE_TILE) * MOE_TILE
    off_end = jnp.cumsum(padded)
    off = off_end - padded
    ntiles = (off_end[-1] // MOE_TILE).astype(jnp.int32)
    tile_start = jnp.arange(MOE_MAX_TILES, dtype=jnp.int32) * MOE_TILE
    last_start = (ntiles - 1) * MOE_TILE
    tile_cls = jnp.sum(jnp.minimum(tile_start, last_start)[:, None] >= off_end[None, :], axis=1)
    pair = tile_cls % N_PAIRS
    group = tile_cls // N_PAIRS
    pair_hot = pair[:, None] == jnp.arange(N_PAIRS)[None, :]
    tile_ea = (group * EXPERTS_PER_GROUP + jnp.sum(pair_hot * jnp.asarray(PAIR_A), axis=1)).astype(jnp.int32)
    tile_eb = (group * EXPERTS_PER_GROUP + jnp.sum(pair_hot * jnp.asarray(PAIR_B), axis=1)).astype(jnp.int32)

    class_hot = info[:, 0:1] == jnp.arange(N_CLASSES, dtype=f32)[None, :]
    pos = (info[:, 1] + jnp.sum(jnp.where(class_hot, off.astype(f32)[None, :], 0.0), axis=1)).astype(jnp.int32)
    src = _inverse_positions(pos)
    ys = _moe(tile_ea, tile_eb, ntiles[None], src, h2c, wg, wu, wd, w_route, b_route)

    fg = final_norm_g[None, :]
    y_prompt = _final(pos, ys, x1, mods, fg, 0, T_CTX // FINAL_TILE)
    y_sample = _final(pos, ys, x1, mods, fg, T_CTX // FINAL_TILE, T_LAT // FINAL_TILE)

    shape_kv = (BATCH, 1, SEQ, NA_HEADS, HEAD_DIM)
    return (y_prompt.reshape(BATCH, SEQ, D_MODEL), y_sample.reshape(DEC_BATCH, DEC_SEQ, D_MODEL),
            k_new.reshape(shape_kv), v_new.reshape(shape_kv))
```

```python
import functools

import jax
import jax.numpy as jnp
import numpy as np
from jax import lax
from jax.experimental import pallas as pl
from jax.experimental.pallas import tpu as pltpu

D_MODEL = 2048
BATCH = 32
SEQ = 256
DEC_BATCH = 4
DEC_SEQ = 1024
PAST_LEN = 256
GRID_W = 64
GRID_ROWS = DEC_SEQ // GRID_W
NA_HEADS = 8
HEAD_DIM = 128
NA_WIDTH = NA_HEADS * HEAD_DIM
FOURIER_GROUPS = 4
FOURIER_GROUP_DIM = 256
FOURIER_WIDTH = FOURIER_GROUPS * FOURIER_GROUP_DIM
WIN_ROWS = 8
WIN_COLS = 16
N_GROUPS = 4
EXPERTS_PER_GROUP = 4
N_EXPERTS = N_GROUPS * EXPERTS_PER_GROUP
EXPERT_HIDDEN = 512
N_MOD = 6
IN_WIDTH = FOURIER_WIDTH + 3 * NA_WIDTH + 2 * D_MODEL
EPS = 1e-6
NEG_INF = -1e30

T_CTX = BATCH * SEQ
T_LAT = DEC_BATCH * DEC_SEQ
T_ALL = T_CTX + T_LAT

LANES = 128
CHUNKS = D_MODEL // LANES
MOD_ROWS = 8

PAIR_A = (0, 0, 0, 1, 1, 3)
PAIR_B = (1, 2, 3, 3, 2, 2)
N_PAIRS = len(PAIR_A)
N_CLASSES = N_GROUPS * N_PAIRS
MOE_TILE = 256
MOE_OUT_CHUNK = 512
MOE_AHEAD = 3
MOE_SLOTS = MOE_AHEAD + 1
MOE_MAX_TILES = -(-T_ALL // MOE_TILE) + N_CLASSES
FINAL_TILE = 256
FINAL_AHEAD = 3
FINAL_SLOTS = FINAL_AHEAD + 1
T_PAD = MOE_MAX_TILES * MOE_TILE

VMEM_LIMIT = 56 * 1024 * 1024

bf16 = jnp.bfloat16
f32 = jnp.float32


def _params(n_axes, vmem=VMEM_LIMIT):
    return pltpu.CompilerParams(dimension_semantics=("arbitrary",) * n_axes, vmem_limit_bytes=vmem)


def _mod_row(tile, n_ctx_tiles, tiles_per_request):
    return jnp.where(tile < n_ctx_tiles, 0, 1 + (tile - n_ctx_tiles) // tiles_per_request)


def _mod_vec(mods_ref, row, k):
    return mods_ref[pl.ds(row, 1), k * D_MODEL:(k + 1) * D_MODEL]


def _rms(x):
    return x * lax.rsqrt(jnp.mean(x * x, axis=-1, keepdims=True) + EPS)


def _ada_kernel(c_ref, w_ref, b_ref, o_ref):
    c = c_ref[...]
    s = (c * jax.nn.sigmoid(c)).astype(bf16)
    o_ref[...] = jnp.dot(s, w_ref[...].astype(bf16), preferred_element_type=f32) + b_ref[...]


def _ada(c_all, w_ada, b_ada):
    tn = 1024
    n = N_MOD * D_MODEL
    return pl.pallas_call(
        _ada_kernel,
        grid=(n // tn,),
        in_specs=[pl.BlockSpec((MOD_ROWS, D_MODEL), lambda j: (0, 0)),
                  pl.BlockSpec((D_MODEL, tn), lambda j: (0, j)),
                  pl.BlockSpec((1, tn), lambda j: (0, j))],
        out_specs=pl.BlockSpec((MOD_ROWS, tn), lambda j: (0, j)),
        out_shape=jax.ShapeDtypeStruct((MOD_ROWS, n), f32),
        compiler_params=_params(1),
        name="ada",
    )(c_all, w_ada, b_ada)


MOD_TILE = 1024
MOD_ROWS_PER_PASS = 128


def _mod_kernel(xp_ref, xs_ref, mods_ref, g_ref, o_ref):
    i = pl.program_id(0)
    n_ctx = T_CTX // MOD_TILE
    row = _mod_row(i, n_ctx, DEC_SEQ // MOD_TILE)
    sh = _mod_vec(mods_ref, row, 0)
    sc = _mod_vec(mods_ref, row, 1)

    gain = g_ref[...] * (1.0 + sc)

    def run(x_ref):
        def body(r, carry):
            rows = pl.ds(pl.multiple_of(r * MOD_ROWS_PER_PASS, MOD_ROWS_PER_PASS), MOD_ROWS_PER_PASS)
            o_ref[rows, :] = (_rms(x_ref[rows, :]) * gain + sh).astype(bf16)
            return carry

        lax.fori_loop(0, MOD_TILE // MOD_ROWS_PER_PASS, body, 0)

    @pl.when(i < n_ctx)
    def _():
        run(xp_ref)

    @pl.when(i >= n_ctx)
    def _():
        run(xs_ref)


def _modulate1(xp, xs, mods, g):
    n_ctx = T_CTX // MOD_TILE
    return pl.pallas_call(
        _mod_kernel,
        grid=(T_ALL // MOD_TILE,),
        in_specs=[pl.BlockSpec((MOD_TILE, D_MODEL), lambda i: (jnp.minimum(i, n_ctx - 1), 0)),
                  pl.BlockSpec((MOD_TILE, D_MODEL), lambda i: (jnp.maximum(i - n_ctx, 0), 0)),
                  pl.BlockSpec((MOD_ROWS, N_MOD * D_MODEL), lambda i: (0, 0)),
                  pl.BlockSpec((1, D_MODEL), lambda i: (0, 0))],
        out_specs=pl.BlockSpec((MOD_TILE, D_MODEL), lambda i: (i, 0)),
        out_shape=jax.ShapeDtypeStruct((T_ALL, D_MODEL), bf16),
        compiler_params=_params(1),
        name="modulate1",
    )(xp, xs, mods, g)


PROJ_TM = 1024
PROJ_TN = 1024
PROJ_CHUNK = 256
K_COL = (FOURIER_WIDTH + NA_WIDTH) // PROJ_TN
V_COL = (FOURIER_WIDTH + 2 * NA_WIDTH) // PROJ_TN


PROJ_STEPS = (IN_WIDTH // PROJ_TN) * (T_ALL // PROJ_TM)
CAST_SHAPES = ((N_EXPERTS * D_MODEL, EXPERT_HIDDEN, 512),
               (N_EXPERTS * D_MODEL, EXPERT_HIDDEN, 512),
               (N_EXPERTS * EXPERT_HIDDEN, D_MODEL, 128),
               (FOURIER_WIDTH, D_MODEL, 16),
               (NA_WIDTH, D_MODEL, 16),
               (D_MODEL, D_MODEL, 32))
N_CAST = len(CAST_SHAPES)


KV_ROWS = PROJ_TM * NA_HEADS


def _kv_copy(acc_ref, dst_hbm, tile, sem):
    rows = pl.ds(pl.multiple_of(tile * KV_ROWS, KV_ROWS), KV_ROWS)
    return pltpu.make_async_copy(acc_ref.at[tile % 2], dst_hbm.at[rows, :], sem)


def _proj_kernel(h_ref, w_ref, b_ref, *rest):
    cast_in = rest[:N_CAST]
    p_ref, k_hbm, v_hbm = rest[N_CAST:N_CAST + 3]
    cast_out = rest[N_CAST + 3:2 * N_CAST + 3]
    wb_ref, acc_ref, sem = rest[-3:]
    j = pl.program_id(0)
    i = pl.program_id(1)
    n_ctx = T_CTX // PROJ_TM
    is_k = (j == K_COL) & (i < n_ctx)
    is_v = (j == V_COL) & (i < n_ctx)

    @pl.when(i == 0)
    def _():
        wb_ref[...] = w_ref[...].astype(bf16)

    n_chunks = PROJ_TN // PROJ_CHUNK
    for c in range(n_chunks):
        cols = slice(c * PROJ_CHUNK, (c + 1) * PROJ_CHUNK)
        acc = jnp.dot(h_ref[...], wb_ref[:, cols], preferred_element_type=f32) + b_ref[:, cols]
        p_ref[:, cols] = acc.astype(bf16)
        for hh in range(PROJ_CHUNK // HEAD_DIM):
            head = c * (PROJ_CHUNK // HEAD_DIM) + hh
            acc_ref[i % 2, pl.ds(head, PROJ_TM, stride=NA_HEADS), :] = acc[:, hh * HEAD_DIM:(hh + 1) * HEAD_DIM]
        for src, dst in list(zip(cast_in, cast_out))[c::n_chunks]:
            dst[...] = src[...].astype(bf16)

    @pl.when(((j == K_COL) | (j == V_COL)) & (i >= 1) & (i <= n_ctx))
    def _():
        _kv_copy(acc_ref, k_hbm, i - 1, sem).wait()

    @pl.when(is_k)
    def _():
        _kv_copy(acc_ref, k_hbm, i, sem).start()

    @pl.when(is_v)
    def _():
        _kv_copy(acc_ref, v_hbm, i, sem).start()


def _cast_spec(rows, cols, rb):
    m = T_ALL // PROJ_TM
    return pl.BlockSpec((rb, cols), lambda j, i: (jnp.minimum(j * m + i, rows // rb - 1), 0))


def _project(h1, w_in, b_in, cast_weights):
    assert all(rows // rb <= PROJ_STEPS for rows, _, rb in CAST_SHAPES)
    assert T_CTX // PROJ_TM < T_ALL // PROJ_TM
    cast_specs = [_cast_spec(*s) for s in CAST_SHAPES]
    hbm = pl.BlockSpec(memory_space=pl.ANY)
    outs = pl.pallas_call(
        _proj_kernel,
        grid=(IN_WIDTH // PROJ_TN, T_ALL // PROJ_TM),
        in_specs=[pl.BlockSpec((PROJ_TM, D_MODEL), lambda j, i: (i, 0)),
                  pl.BlockSpec((D_MODEL, PROJ_TN), lambda j, i: (0, j)),
                  pl.BlockSpec((1, PROJ_TN), lambda j, i: (0, j))] + cast_specs,
        out_specs=[pl.BlockSpec((PROJ_TM, PROJ_TN), lambda j, i: (i, j)), hbm, hbm] + cast_specs,
        out_shape=[jax.ShapeDtypeStruct((T_ALL, IN_WIDTH), bf16),
                   jax.ShapeDtypeStruct((T_CTX * NA_HEADS, HEAD_DIM), f32),
                   jax.ShapeDtypeStruct((T_CTX * NA_HEADS, HEAD_DIM), f32)]
                  + [jax.ShapeDtypeStruct((rows, cols), bf16) for rows, cols, _ in CAST_SHAPES],
        scratch_shapes=[pltpu.VMEM((D_MODEL, PROJ_TN), bf16), pltpu.VMEM((2, KV_ROWS, HEAD_DIM), f32),
                        pltpu.SemaphoreType.DMA(())],
        compiler_params=_params(2),
        name="project",
    )(h1, w_in, b_in, *cast_weights)
    return outs[0], outs[1], outs[2], outs[3:]


Q_GROUP_ROWS = 4
N_Q_GROUPS = GRID_ROWS // Q_GROUP_ROWS
Q_GROUP = Q_GROUP_ROWS * GRID_W
KEY_SPAN_ROWS = 12
KEY_SPAN = KEY_SPAN_ROWS * GRID_W


def _window_start(r):
    return min(max(r - WIN_ROWS // 2, 0), GRID_ROWS - WIN_ROWS)


def _key_base(g):
    lo = _window_start(g * Q_GROUP_ROWS)
    hi = _window_start((g + 1) * Q_GROUP_ROWS - 1) + WIN_ROWS
    assert hi - lo <= KEY_SPAN_ROWS
    return min(lo, GRID_ROWS - KEY_SPAN_ROWS)


def _visible_span(g):
    base = _key_base(g)
    first = _window_start(g * Q_GROUP_ROWS) - base
    last = _window_start((g + 1) * Q_GROUP_ROWS - 1) + WIN_ROWS - base
    lo = (first // 2) * 2 * GRID_W
    hi = -(-last // 2) * 2 * GRID_W
    return lo, hi


def _block_shift(g, rr):
    return _key_base(g) - (g * Q_GROUP_ROWS + rr) + WIN_ROWS - 1


TABLE_PAD = -min(_block_shift(g, rr) for g in range(N_Q_GROUPS) for rr in range(Q_GROUP_ROWS))
TABLE_BLOCKS = 2 * (-(-(max(_block_shift(g, rr) for g in range(N_Q_GROUPS) for rr in range(Q_GROUP_ROWS))
                        + TABLE_PAD + KEY_SPAN_ROWS + 1) // 2))
TABLE_LANES = TABLE_BLOCKS * GRID_W


def _bias_kernel(r0_ref, r1_ref, o_ref):
    qc = lax.broadcasted_iota(jnp.int32, (GRID_W, TABLE_LANES), 0)
    lane = lax.broadcasted_iota(jnp.int32, (GRID_W, TABLE_LANES), 1)
    kc = lane & (GRID_W - 1)
    dc = kc - qc + (WIN_COLS - 1)
    start = jnp.clip(qc - WIN_COLS // 2, 0, GRID_W - WIN_COLS)
    ok = (kc >= start) & (kc < start + WIN_COLS)
    tables = []
    for r_ref in (r0_ref, r1_ref):
        acc = jnp.zeros((GRID_W, TABLE_LANES), f32)
        for b in range(2 * WIN_COLS - 1):
            acc = jnp.where(dc == b, r_ref[0, b:b + 1, :], acc)
        tables.append(jnp.where(ok, acc, NEG_INF))
    span_row = jnp.right_shift(lax.broadcasted_iota(jnp.int32, (GRID_W, KEY_SPAN), 1), GRID_W.bit_length() - 1)
    for g in range(N_Q_GROUPS):
        for rr in range(Q_GROUP_ROWS):
            first = _window_start(g * Q_GROUP_ROWS + rr) - _key_base(g)
            e = _block_shift(g, rr) + TABLE_PAD
            lo = (e - e % 2) * GRID_W
            in_window = (span_row >= first) & (span_row < first + WIN_ROWS)
            o_ref[0, g, rr * GRID_W:(rr + 1) * GRID_W, :] = jnp.where(
                in_window, tables[e % 2][:, lo:lo + KEY_SPAN], NEG_INF)


def _bias_table(rpb):
    nb = 2 * WIN_COLS - 1
    n_off = 2 * WIN_ROWS - 1
    by_row = jnp.transpose(rpb, (0, 2, 1))
    rpbx = []
    for p in range(2):
        spread = np.zeros((n_off, TABLE_LANES), np.float32)
        for t in range(TABLE_BLOCKS):
            if 0 <= t + p - TABLE_PAD < n_off:
                spread[t + p - TABLE_PAD, t * GRID_W:(t + 1) * GRID_W] = 1.0
        rpbx.append(jnp.einsum("hba,al->hbl", by_row, jnp.asarray(spread), precision=lax.Precision.HIGHEST))
    table_spec = pl.BlockSpec((1, nb, TABLE_LANES), lambda h: (h, 0, 0))
    return pl.pallas_call(
        _bias_kernel,
        grid=(NA_HEADS,),
        in_specs=[table_spec, table_spec],
        out_specs=pl.BlockSpec((1, N_Q_GROUPS, Q_GROUP, KEY_SPAN), lambda h: (h, 0, 0, 0)),
        out_shape=jax.ShapeDtypeStruct((NA_HEADS, N_Q_GROUPS, Q_GROUP, KEY_SPAN), f32),
        compiler_params=_params(1),
        name="bias_table",
    )(*rpbx)


def _qk(q, k):
    return lax.dot_general(q, k, (((1,), (1,)), ((), ())), preferred_element_type=f32)


def _attn_ctx_kernel(q_ref, k_ref, v_ref, o_ref):
    scale = HEAD_DIM ** -0.5
    for h in range(NA_HEADS):
        sl = slice(h * HEAD_DIM, (h + 1) * HEAD_DIM)
        s = _qk(q_ref[:, sl], k_ref[:, sl]) * scale
        p = jnp.exp(s - jnp.max(s, axis=-1, keepdims=True))
        l = jnp.sum(p, axis=-1, keepdims=True)
        o = jnp.dot(p.astype(bf16), v_ref[:, sl], preferred_element_type=f32) / l
        o_ref[:, sl] = o.astype(bf16)


def _attn_ctx(p_all):
    qcol = FOURIER_WIDTH // NA_WIDTH
    return pl.pallas_call(
        _attn_ctx_kernel,
        grid=(BATCH,),
        in_specs=[pl.BlockSpec((SEQ, NA_WIDTH), lambda b: (b, qcol)),
                  pl.BlockSpec((SEQ, NA_WIDTH), lambda b: (b, qcol + 1)),
                  pl.BlockSpec((SEQ, NA_WIDTH), lambda b: (b, qcol + 2))],
        out_specs=pl.BlockSpec((SEQ, NA_WIDTH), lambda b: (b, 0)),
        out_shape=jax.ShapeDtypeStruct((T_CTX, NA_WIDTH), bf16),
        compiler_params=_params(1),
        name="attn_ctx",
    )(p_all, p_all, p_all)


def _attn_lat_kernel(q_ref, k_ref, v_ref, kc_ref, vc_ref, bias_ref, o_ref):
    scale = HEAD_DIM ** -0.5
    head_rows = pl.ds(pl.program_id(0), PAST_LEN, stride=NA_HEADS)
    kcb = kc_ref[0, head_rows, :].astype(bf16)
    vcb = vc_ref[0, head_rows, :].astype(bf16)
    for g in range(N_Q_GROUPS):
        lo, hi = _visible_span(g)
        k0 = _key_base(g) * GRID_W + lo
        nk = hi - lo
        q = q_ref[g * Q_GROUP:(g + 1) * Q_GROUP, :]
        s_nb = _qk(q, k_ref[k0:k0 + nk, :]) * scale + bias_ref[0, g, :, lo:hi]
        s_cx = _qk(q, kcb) * scale
        m = jnp.maximum(jnp.max(s_nb, axis=-1, keepdims=True), jnp.max(s_cx, axis=-1, keepdims=True))
        p_nb = jnp.exp(s_nb - m)
        p_cx = jnp.exp(s_cx - m)
        l = jnp.sum(p_nb, axis=-1, keepdims=True) + jnp.sum(p_cx, axis=-1, keepdims=True)
        o = (jnp.dot(p_nb.astype(bf16), v_ref[k0:k0 + nk, :], preferred_element_type=f32)
             + jnp.dot(p_cx.astype(bf16), vcb, preferred_element_type=f32)) / l
        o_ref[g * Q_GROUP:(g + 1) * Q_GROUP, :] = o.astype(bf16)


def _attn_lat(p_all, ck, cv, bias):
    row0 = T_CTX // DEC_SEQ
    qcol = FOURIER_WIDTH // HEAD_DIM
    return pl.pallas_call(
        _attn_lat_kernel,
        grid=(NA_HEADS, DEC_BATCH),
        in_specs=[pl.BlockSpec((DEC_SEQ, HEAD_DIM), lambda h, b: (row0 + b, qcol + h)),
                  pl.BlockSpec((DEC_SEQ, HEAD_DIM), lambda h, b: (row0 + b, qcol + NA_HEADS + h)),
                  pl.BlockSpec((DEC_SEQ, HEAD_DIM), lambda h, b: (row0 + b, qcol + 2 * NA_HEADS + h)),
                  pl.BlockSpec((1, PAST_LEN * NA_HEADS, HEAD_DIM), lambda h, b: (b, 0, 0)),
                  pl.BlockSpec((1, PAST_LEN * NA_HEADS, HEAD_DIM), lambda h, b: (b, 0, 0)),
                  pl.BlockSpec((1, N_Q_GROUPS, Q_GROUP, KEY_SPAN), lambda h, b: (h, 0, 0, 0))],
        out_specs=pl.BlockSpec((DEC_SEQ, HEAD_DIM), lambda h, b: (b, h)),
        out_shape=jax.ShapeDtypeStruct((T_LAT, NA_WIDTH), bf16),
        compiler_params=_params(2),
        name="attn_lat",
    )(p_all, p_all, p_all, ck, cv, bias)


def _dft_consts(n):
    def cs(m):
        idx = (np.arange(m)[:, None] * np.arange(m)[None, :]) % m
        ang = 2.0 * np.pi * idx.astype(np.float64) / m
        return np.cos(ang), np.sin(ang)

    cc, sc = cs(FOURIER_GROUP_DIM)
    cn, sn = cs(n)
    chan = np.concatenate([cc, sc], axis=1) / np.sqrt(FOURIER_GROUP_DIM)
    pos = np.concatenate([cn, -sn], axis=1) / np.sqrt(n)
    return jnp.asarray(chan, f32).astype(bf16), jnp.asarray(pos, f32).astype(bf16)


def _fourier_kernel(u_ref, chan_ref, pos_ref, o_ref):
    gd = FOURIER_GROUP_DIM
    ys = [jnp.dot(u_ref[:, g * gd:(g + 1) * gd], chan_ref[...], preferred_element_type=f32)
          for g in range(FOURIER_GROUPS)]
    yc = jnp.concatenate([y[:, :gd] for y in ys], axis=1)
    ysn = jnp.concatenate([y[:, gd:] for y in ys], axis=1)
    stacked = jnp.concatenate([yc, ysn], axis=0).astype(bf16)
    o_ref[...] = jnp.dot(pos_ref[...], stacked, preferred_element_type=f32).astype(bf16)


def _fourier(p_all, n, n_req, row0):
    chan, pos = _dft_consts(n)
    return pl.pallas_call(
        _fourier_kernel,
        grid=(n_req,),
        in_specs=[pl.BlockSpec((n, FOURIER_WIDTH), lambda b: (row0 + b, 0)),
                  pl.BlockSpec((FOURIER_GROUP_DIM, 2 * FOURIER_GROUP_DIM), lambda b: (0, 0)),
                  pl.BlockSpec((n, 2 * n), lambda b: (0, 0))],
        out_specs=pl.BlockSpec((n, FOURIER_WIDTH), lambda b: (b, 0)),
        out_shape=jax.ShapeDtypeStruct((n_req * n, FOURIER_WIDTH), bf16),
        compiler_params=_params(1),
        name=f"fourier_{n}",
    )(p_all, chan, pos)


MERGE_TM = 256
ROUTE_LANES = LANES


def _class_of(logits):
    lane = lax.broadcasted_iota(jnp.int32, logits.shape, 1)
    big = jnp.int32(ROUTE_LANES)
    is_group = lane < N_GROUPS
    mg = jnp.max(jnp.where(is_group, logits, -jnp.inf), axis=-1, keepdims=True)
    gsel = jnp.min(jnp.where(is_group & (logits == mg), lane, big), axis=-1, keepdims=True)
    lo = N_GROUPS + EXPERTS_PER_GROUP * gsel
    in_group = (lane >= lo) & (lane < lo + EXPERTS_PER_GROUP)
    m1 = jnp.max(jnp.where(in_group, logits, -jnp.inf), axis=-1, keepdims=True)
    i1 = jnp.min(jnp.where(in_group & (logits == m1), lane, big), axis=-1, keepdims=True)
    rest = in_group & (lane != i1)
    m2 = jnp.max(jnp.where(rest, logits, -jnp.inf), axis=-1, keepdims=True)
    i2 = jnp.min(jnp.where(rest & (logits == m2), lane, big), axis=-1, keepdims=True)
    e_lo = jnp.minimum(i1, i2) - lo
    e_hi = jnp.maximum(i1, i2) - lo
    pair = jnp.zeros_like(e_lo)
    for idx in range(N_PAIRS):
        a, b = sorted((PAIR_A[idx], PAIR_B[idx]))
        pair = jnp.where((e_lo == a) & (e_hi == b), idx, pair)
    return gsel * N_PAIRS + pair


def _merge_kernel(fmc_ref, fml_ref, nac_ref, nal_ref, ga_ref, gb_ref, xp_ref, xs_ref, mods_ref, g2_ref,
                  wf_ref, wn_ref, wo_ref, wr_ref, br_ref,
                  x1_ref, h2c_ref, info_ref, cnt_ref, carry_ref, x1s_ref):
    s = pl.program_id(0)
    n_ctx = T_CTX // MERGE_TM
    n_all = T_ALL // MERGE_TM
    per_req = DEC_SEQ // MERGE_TM
    row = _mod_row(jnp.minimum(s, n_all - 1), n_ctx, per_req)
    prev_row = _mod_row(jnp.maximum(s - 1, 0), n_ctx, per_req)

    @pl.when(s == 0)
    def _():
        carry_ref[...] = jnp.zeros_like(carry_ref)
        x1s_ref[1] = jnp.zeros((MERGE_TM, D_MODEL), f32)

    def step(tile_refs):
        x1p = x1s_ref[(s + 1) % 2]
        if tile_refs:
            fm_ref, na_ref, x_ref = tile_refs
            a = jnp.dot(fm_ref[...], wf_ref[...], preferred_element_type=f32)

        h2 = _rms(x1p) * g2_ref[...] * (1.0 + _mod_vec(mods_ref, prev_row, 4)) + _mod_vec(mods_ref, prev_row, 3)
        if tile_refs:
            gate_a = jax.nn.sigmoid(ga_ref[...].astype(f32))
            b = jnp.dot(na_ref[...], wn_ref[...], preferred_element_type=f32)

        for k in range(CHUNKS):
            h2c_ref[pl.ds(k, MERGE_TM, stride=CHUNKS), :] = h2[:, k * LANES:(k + 1) * LANES]
        logits = jnp.dot(h2.astype(bf16), wr_ref[...], preferred_element_type=f32) + br_ref[...]
        if tile_refs:
            merged = gate_a * a + jax.nn.sigmoid(gb_ref[...].astype(f32)) * b
            out = jnp.dot(merged.astype(bf16), wo_ref[...], preferred_element_type=f32)

        cls = _class_of(logits)
        lane = lax.broadcasted_iota(jnp.int32, logits.shape, 1)
        onehot = ((lane == cls) & (s > 0)).astype(f32)
        r = lax.broadcasted_iota(jnp.int32, (MERGE_TM, MERGE_TM), 0)
        c = lax.broadcasted_iota(jnp.int32, (MERGE_TM, MERGE_TM), 1)
        before = (c < r).astype(bf16)
        prefix = jnp.dot(before, onehot.astype(bf16), preferred_element_type=f32) + carry_ref[...]
        rank = jnp.sum(onehot * prefix, axis=-1, keepdims=True)
        info_ref[...] = jnp.where(lane == 0, cls.astype(f32), jnp.where(lane == 1, rank, 0.0))
        carry_ref[...] = carry_ref[...] + jnp.sum(onehot, axis=0, keepdims=True)
        cnt_ref[...] = jnp.broadcast_to(carry_ref[...], cnt_ref.shape)
        if tile_refs:
            x1 = x_ref[...] + _mod_vec(mods_ref, row, 2) * out
            x1_ref[...] = x1
            x1s_ref[s % 2] = x1

    @pl.when(s < n_ctx)
    def _():
        step((fmc_ref, nac_ref, xp_ref))

    @pl.when((s >= n_ctx) & (s < n_all))
    def _():
        step((fml_ref, nal_ref, xs_ref))

    @pl.when(s == n_all)
    def _():
        step(None)


def _merge(fm_c, fm_l, na_c, na_l, p_all, xp, xs, mods, g2, wf, wn, wo, wr, br):
    n_ctx = T_CTX // MERGE_TM
    n_lat = T_LAT // MERGE_TM
    n_all = n_ctx + n_lat
    gcol = (FOURIER_WIDTH + 3 * NA_WIDTH) // D_MODEL
    const = lambda i: (0, 0)
    ctx_tile = lambda i: (jnp.minimum(i, n_ctx - 1), 0)
    lat_tile = lambda i: (jnp.clip(i - n_ctx, 0, n_lat - 1), 0)
    this_tile = lambda i: jnp.minimum(i, n_all - 1)
    prev_tile = lambda i: (jnp.maximum(i - 1, 0), 0)
    once = pl.Buffered(1)
    return pl.pallas_call(
        _merge_kernel,
        grid=(n_all + 1,),
        in_specs=[pl.BlockSpec((MERGE_TM, FOURIER_WIDTH), ctx_tile),
                  pl.BlockSpec((MERGE_TM, FOURIER_WIDTH), lat_tile),
                  pl.BlockSpec((MERGE_TM, NA_WIDTH), ctx_tile),
                  pl.BlockSpec((MERGE_TM, NA_WIDTH), lat_tile),
                  pl.BlockSpec((MERGE_TM, D_MODEL), lambda i: (this_tile(i), gcol)),
                  pl.BlockSpec((MERGE_TM, D_MODEL), lambda i: (this_tile(i), gcol + 1)),
                  pl.BlockSpec((MERGE_TM, D_MODEL), ctx_tile),
                  pl.BlockSpec((MERGE_TM, D_MODEL), lat_tile),
                  pl.BlockSpec((MOD_ROWS, N_MOD * D_MODEL), const, pipeline_mode=once),
                  pl.BlockSpec((1, D_MODEL), const, pipeline_mode=once),
                  pl.BlockSpec((FOURIER_WIDTH, D_MODEL), const, pipeline_mode=once),
                  pl.BlockSpec((NA_WIDTH, D_MODEL), const, pipeline_mode=once),
                  pl.BlockSpec((D_MODEL, D_MODEL), const, pipeline_mode=once),
                  pl.BlockSpec((D_MODEL, ROUTE_LANES), const, pipeline_mode=once),
                  pl.BlockSpec((1, ROUTE_LANES), const, pipeline_mode=once)],
        out_specs=[pl.BlockSpec((MERGE_TM, D_MODEL), lambda i: (this_tile(i), 0)),
                   pl.BlockSpec((MERGE_TM * CHUNKS, LANES), prev_tile),
                   pl.BlockSpec((MERGE_TM, ROUTE_LANES), prev_tile),
                   pl.BlockSpec((8, ROUTE_LANES), const)],
        out_shape=[jax.ShapeDtypeStruct((T_ALL, D_MODEL), f32),
                   jax.ShapeDtypeStruct((T_ALL * CHUNKS, LANES), f32),
                   jax.ShapeDtypeStruct((T_ALL, ROUTE_LANES), f32),
                   jax.ShapeDtypeStruct((8, ROUTE_LANES), f32)],
        scratch_shapes=[pltpu.VMEM((1, ROUTE_LANES), f32), pltpu.VMEM((2, MERGE_TM, D_MODEL), f32)],
        compiler_params=_params(1),
        name="merge",
    )(fm_c, fm_l, na_c, na_l, p_all, p_all, xp, xs, mods, g2, wf, wn, wo, wr, br)


def _perm_kernel(pos_ref, src_ref):
    def clear(p, carry):
        src_ref[p] = 0
        return carry

    lax.fori_loop(0, T_PAD, clear, 0, unroll=16)

    def place(t, carry):
        src_ref[pos_ref[t]] = t
        return carry

    lax.fori_loop(0, T_ALL, place, 0, unroll=16)


def _inverse_positions(pos):
    smem = pl.BlockSpec(memory_space=pltpu.SMEM)
    return pl.pallas_call(
        _perm_kernel,
        in_specs=[smem],
        out_specs=smem,
        out_shape=jax.ShapeDtypeStruct((T_PAD,), jnp.int32),
        name="permutation",
    )(pos)


def _row_copy(idx_ref, base, src_hbm, buf, sem, slot, k):
    tok = idx_ref[base + k]
    return pltpu.make_async_copy(src_hbm.at[pl.ds(pl.multiple_of(tok * CHUNKS, CHUNKS), CHUNKS), :],
                                 buf.at[slot, pl.ds(pl.multiple_of(k * CHUNKS, CHUNKS), CHUNKS), :],
                                 sem.at[slot])


def _start_gather(idx_ref, base, src_hbm, buf, sem, slot, rows, queues=(0, 1)):
    def body(k8, carry):
        for j in range(8):
            _row_copy(idx_ref, base, src_hbm, buf, sem, slot, k8 * 8 + j).start(priority=queues[j % len(queues)])
        return carry

    lax.fori_loop(0, rows // 8, body, 0)


def _wait_gather(buf, sem, slot):
    pltpu.make_async_copy(buf.at[slot], buf.at[slot], sem.at[slot]).wait()


def _gathered_rows(buf, slot, rows):
    return jnp.concatenate([buf[slot, pl.ds(s, rows, stride=CHUNKS), :] for s in range(CHUNKS)], axis=1)


def _moe_kernel(ea_ref, eb_ref, nt_ref, src_ref, h2c_hbm, wga_ref, wua_ref, wda_ref, wgb_ref, wub_ref, wdb_ref,
                wr_ref, br_ref, o_ref, buf, sem):
    i = pl.program_id(0)
    nt = nt_ref[0]

    @pl.when(i >= nt)
    def _():
        o_ref[...] = jnp.zeros_like(o_ref)

    def fetch(tile):
        _start_gather(src_ref, tile * MOE_TILE, h2c_hbm, buf, sem, tile % MOE_SLOTS, MOE_TILE, queues=(1,))

    @pl.when(i == 0)
    def _():
        for t in range(MOE_AHEAD):
            @pl.when(t < nt)
            def _(t=t):
                fetch(t)

    @pl.when(i + MOE_AHEAD < nt)
    def _():
        fetch(i + MOE_AHEAD)

    @pl.when(i < nt)
    def _():
        slot = i % MOE_SLOTS
        _wait_gather(buf, sem, slot)
        x = _gathered_rows(buf, slot, MOE_TILE).astype(bf16)
        logits = jnp.dot(x, wr_ref[...], preferred_element_type=f32) + br_ref[...]
        lane = lax.broadcasted_iota(jnp.int32, logits.shape, 1)
        ea = ea_ref[i]
        eb = eb_ref[i]

        def pick(col):
            return jnp.sum(jnp.where(lane == col, logits, 0.0), axis=-1, keepdims=True)

        lg = pick(ea // EXPERTS_PER_GROUP)
        p_group = 1.0 / jnp.sum(jnp.where(lane < N_GROUPS, jnp.exp(logits - lg), 0.0), axis=-1, keepdims=True)
        la = pick(N_GROUPS + ea)
        lb = pick(N_GROUPS + eb)
        w_a = p_group / (1.0 + jnp.exp(lb - la))
        w_b = p_group / (1.0 + jnp.exp(la - lb))

        def up(wg_ref, wu_ref):
            return (jnp.dot(x, wg_ref[0], preferred_element_type=f32),
                    jnp.dot(x, wu_ref[0], preferred_element_type=f32))

        def act(g, u):
            return (g * jax.nn.sigmoid(g) * u).astype(bf16)

        g_a, u_a = up(wga_ref, wua_ref)
        g_b, u_b = up(wgb_ref, wub_ref)
        hid_a = act(g_a, u_a)
        hid_b = act(g_b, u_b)
        per_chunk = MOE_OUT_CHUNK // LANES
        for c in range(D_MODEL // MOE_OUT_CHUNK):
            cols = slice(c * MOE_OUT_CHUNK, (c + 1) * MOE_OUT_CHUNK)
            y = (w_a * jnp.dot(hid_a, wda_ref[0, :, cols], preferred_element_type=f32)
                 + w_b * jnp.dot(hid_b, wdb_ref[0, :, cols], preferred_element_type=f32))
            for k in range(per_chunk):
                o_ref[pl.ds(c * per_chunk + k, MOE_TILE, stride=CHUNKS), :] = y[:, k * LANES:(k + 1) * LANES]


def _moe(tile_ea, tile_eb, ntiles, src, h2c, wg, wu, wd, wr, br):
    up = lambda tab: (lambda i, ea, eb, nt, src: (tab(ea, eb)[i], 0, 0))
    pick_a = lambda ea, eb: ea
    pick_b = lambda ea, eb: eb
    const = lambda i, ea, eb, nt, src: (0, 0)
    grid_spec = pltpu.PrefetchScalarGridSpec(
        num_scalar_prefetch=4,
        grid=(MOE_MAX_TILES,),
        in_specs=[pl.BlockSpec(memory_space=pl.ANY),
                  pl.BlockSpec((1, D_MODEL, EXPERT_HIDDEN), up(pick_a)),
                  pl.BlockSpec((1, D_MODEL, EXPERT_HIDDEN), up(pick_a)),
                  pl.BlockSpec((1, EXPERT_HIDDEN, D_MODEL), up(pick_a)),
                  pl.BlockSpec((1, D_MODEL, EXPERT_HIDDEN), up(pick_b)),
                  pl.BlockSpec((1, D_MODEL, EXPERT_HIDDEN), up(pick_b)),
                  pl.BlockSpec((1, EXPERT_HIDDEN, D_MODEL), up(pick_b)),
                  pl.BlockSpec((D_MODEL, ROUTE_LANES), const),
                  pl.BlockSpec((1, ROUTE_LANES), const)],
        out_specs=pl.BlockSpec((MOE_TILE * CHUNKS, LANES), lambda i, ea, eb, nt, src: (i, 0)),
        scratch_shapes=[pltpu.VMEM((MOE_SLOTS, MOE_TILE * CHUNKS, LANES), f32),
                        pltpu.SemaphoreType.DMA((MOE_SLOTS,))],
    )
    return pl.pallas_call(
        _moe_kernel,
        grid_spec=grid_spec,
        out_shape=jax.ShapeDtypeStruct((T_PAD * CHUNKS, LANES), f32),
        compiler_params=_params(1),
        name="moe",
    )(tile_ea, tile_eb, ntiles, src, h2c, wg, wu, wd, wg, wu, wd, wr, br)


def _final_kernel(pos_ref, ys_hbm, x1_ref, mods_ref, g_ref, o_ref, buf, sem, *, tile0, n_tiles, n_ctx_tiles):
    i = pl.program_id(0)

    def fetch(tile):
        _start_gather(pos_ref, (tile0 + tile) * FINAL_TILE, ys_hbm, buf, sem, tile % FINAL_SLOTS, FINAL_TILE)

    @pl.when(i == 0)
    def _():
        for t in range(min(FINAL_AHEAD, n_tiles)):
            fetch(t)

    @pl.when(i + FINAL_AHEAD < n_tiles)
    def _():
        fetch(i + FINAL_AHEAD)

    slot = i % FINAL_SLOTS
    _wait_gather(buf, sem, slot)
    y = _gathered_rows(buf, slot, FINAL_TILE)
    row = _mod_row(tile0 + i, n_ctx_tiles, DEC_SEQ // FINAL_TILE)
    x2 = x1_ref[...] + _mod_vec(mods_ref, row, 5) * y
    o_ref[...] = _rms(x2) * g_ref[...]


def _final(pos, ys, x1, mods, g, tile0, n_tiles):
    body = functools.partial(_final_kernel, tile0=tile0, n_tiles=n_tiles, n_ctx_tiles=T_CTX // FINAL_TILE)
    grid_spec = pltpu.PrefetchScalarGridSpec(
        num_scalar_prefetch=1,
        grid=(n_tiles,),
        in_specs=[pl.BlockSpec(memory_space=pl.ANY),
                  pl.BlockSpec((FINAL_TILE, D_MODEL), lambda i, pos: (tile0 + i, 0)),
                  pl.BlockSpec((MOD_ROWS, N_MOD * D_MODEL), lambda i, pos: (0, 0)),
                  pl.BlockSpec((1, D_MODEL), lambda i, pos: (0, 0))],
        out_specs=pl.BlockSpec((FINAL_TILE, D_MODEL), lambda i, pos: (i, 0)),
        scratch_shapes=[pltpu.VMEM((FINAL_SLOTS, FINAL_TILE * CHUNKS, LANES), f32),
                        pltpu.SemaphoreType.DMA((FINAL_SLOTS,))],
    )
    return pl.pallas_call(
        body,
        grid_spec=grid_spec,
        out_shape=jax.ShapeDtypeStruct((n_tiles * FINAL_TILE, D_MODEL), f32),
        compiler_params=_params(1),
        name=f"final_{tile0}",
    )(pos, ys, x1, mods, g)


def kernel(x_prompt, x_sample, cache_k_ctx, cache_v_ctx, c, c_ctx, norm1_g, norm2_g, w_ada, b_ada, w_in, b_in,
           w_fourier, w_na_o, rpb, w_out, w_router_group, b_router_group, w_router_expert, b_router_expert,
           w_exp_gate, w_exp_up, w_exp_down, final_norm_g):
    xp = x_prompt.reshape(T_CTX, D_MODEL)
    xs = x_sample.reshape(T_LAT, D_MODEL)
    c_all = jnp.concatenate([c_ctx[None, :], c, jnp.zeros((MOD_ROWS - 1 - DEC_BATCH, D_MODEL), f32)], axis=0)
    w_route = jnp.concatenate([w_router_group[0], w_router_expert[0]], axis=1)
    w_route = jnp.pad(w_route, ((0, 0), (0, ROUTE_LANES - w_route.shape[1]))).astype(bf16)
    b_route = jnp.concatenate([b_router_group[0], b_router_expert[0]])
    b_route = jnp.pad(b_route, (0, ROUTE_LANES - b_route.shape[0]))[None, :]

    mods = _ada(c_all, w_ada[0], b_ada[0][None, :])
    h1 = _modulate1(xp, xs, mods, norm1_g)
    p_all, k_new, v_new, (wg, wu, wd, wf, wn, wo) = _project(
        h1, w_in[0], b_in,
        (w_exp_gate.reshape(N_EXPERTS * D_MODEL, EXPERT_HIDDEN), w_exp_up.reshape(N_EXPERTS * D_MODEL, EXPERT_HIDDEN),
         w_exp_down.reshape(N_EXPERTS * EXPERT_HIDDEN, D_MODEL), w_fourier[0], w_na_o[0], w_out[0]))
    wg = wg.reshape(N_EXPERTS, D_MODEL, EXPERT_HIDDEN)
    wu = wu.reshape(N_EXPERTS, D_MODEL, EXPERT_HIDDEN)
    wd = wd.reshape(N_EXPERTS, EXPERT_HIDDEN, D_MODEL)

    tt = _bias_table(rpb[0])
    na_c = _attn_ctx(p_all)
    ck = cache_k_ctx.reshape(DEC_BATCH, PAST_LEN * NA_HEADS, HEAD_DIM)
    cv = cache_v_ctx.reshape(DEC_BATCH, PAST_LEN * NA_HEADS, HEAD_DIM)
    na_l = _attn_lat(p_all, ck, cv, tt)
    fm_c = _fourier(p_all, SEQ, BATCH, 0)
    fm_l = _fourier(p_all, DEC_SEQ, DEC_BATCH, T_CTX // DEC_SEQ)

    x1, h2c, info, counts = _merge(fm_c, fm_l, na_c, na_l, p_all, xp, xs, mods, norm2_g, wf, wn, wo,
                                   w_route, b_route)

    cnt = counts[0, :N_CLASSES].astype(jnp.int32)
    padded = ((cnt + MOE_TILE - 1) // MOE_TILE) * MOE_TILE
    off_end = jnp.cumsum(padded)
    off = off_end - padded
    ntiles = (off_end[-1] // MOE_TILE).astype(jnp.int32)
    tile_start = jnp.arange(MOE_MAX_TILES, dtype=jnp.int32) * MOE_TILE
    last_start = (ntiles - 1) * MOE_TILE
    tile_cls = jnp.sum(jnp.minimum(tile_start, last_start)[:, None] >= off_end[None, :], axis=1)
    pair = tile_cls % N_PAIRS
    group = tile_cls // N_PAIRS
    pair_hot = pair[:, None] == jnp.arange(N_PAIRS)[None, :]
    tile_ea = (group * EXPERTS_PER_GROUP + jnp.sum(pair_hot * jnp.asarray(PAIR_A), axis=1)).astype(jnp.int32)
    tile_eb = (group * EXPERTS_PER_GROUP + jnp.sum(pair_hot * jnp.asarray(PAIR_B), axis=1)).astype(jnp.int32)

    class_hot = info[:, 0:1] == jnp.arange(N_CLASSES, dtype=f32)[None, :]
    pos = (info[:, 1] + jnp.sum(jnp.where(class_hot, off.astype(f32)[None, :], 0.0), axis=1)).astype(jnp.int32)
    src = _inverse_positions(pos)
    ys = _moe(tile_ea, tile_eb, ntiles[None], src, h2c, wg, wu, wd, w_route, b_route)

    fg = final_norm_g[None, :]
    y_prompt = _final(pos, ys, x1, mods, fg, 0, T_CTX // FINAL_TILE)
    y_sample = _final(pos, ys, x1, mods, fg, T_CTX // FINAL_TILE, T_LAT // FINAL_TILE)

    shape_kv = (BATCH, 1, SEQ, NA_HEADS, HEAD_DIM)
    return (y_prompt.reshape(BATCH, SEQ, D_MODEL), y_sample.reshape(DEC_BATCH, DEC_SEQ, D_MODEL),
            k_new.reshape(shape_kv), v_new.reshape(shape_kv))
```

```python
import functools

import jax
import jax.numpy as jnp
import numpy as np
from jax import lax
from jax.experimental import pallas as pl
from jax.experimental.pallas import tpu as pltpu

D_MODEL = 2048
BATCH = 32
SEQ = 256
DEC_BATCH = 4
DEC_SEQ = 1024
PAST_LEN = 256
GRID_W = 64
GRID_ROWS = DEC_SEQ // GRID_W
NA_HEADS = 8
HEAD_DIM = 128
NA_WIDTH = NA_HEADS * HEAD_DIM
FOURIER_GROUPS = 4
FOURIER_GROUP_DIM = 256
FOURIER_WIDTH = FOURIER_GROUPS * FOURIER_GROUP_DIM
WIN_ROWS = 8
WIN_COLS = 16
N_GROUPS = 4
EXPERTS_PER_GROUP = 4
N_EXPERTS = N_GROUPS * EXPERTS_PER_GROUP
EXPERT_HIDDEN = 512
N_MOD = 6
IN_WIDTH = FOURIER_WIDTH + 3 * NA_WIDTH + 2 * D_MODEL
EPS = 1e-6
NEG_INF = -1e30

T_CTX = BATCH * SEQ
T_LAT = DEC_BATCH * DEC_SEQ
T_ALL = T_CTX + T_LAT

LANES = 128
CHUNKS = D_MODEL // LANES
MOD_ROWS = 8

PAIR_A = (0, 0, 0, 1, 1, 3)
PAIR_B = (1, 2, 3, 3, 2, 2)
N_PAIRS = len(PAIR_A)
N_CLASSES = N_GROUPS * N_PAIRS
MOE_TILE = 256
MOE_OUT_CHUNK = 512
MOE_AHEAD = 3
MOE_SLOTS = MOE_AHEAD + 1
MOE_MAX_TILES = (T_ALL + N_CLASSES * (MOE_TILE - 1)) // MOE_TILE
FINAL_TILE = 256
FINAL_AHEAD = 3
FINAL_SLOTS = FINAL_AHEAD + 1
T_PAD = MOE_MAX_TILES * MOE_TILE

VMEM_LIMIT = 56 * 1024 * 1024

bf16 = jnp.bfloat16
f32 = jnp.float32


def _params(n_axes, vmem=VMEM_LIMIT):
    return pltpu.CompilerParams(dimension_semantics=("arbitrary",) * n_axes, vmem_limit_bytes=vmem)


def _mod_row(tile, n_ctx_tiles, tiles_per_request):
    return jnp.where(tile < n_ctx_tiles, 0, 1 + (tile - n_ctx_tiles) // tiles_per_request)


def _mod_vec(mods_ref, row, k):
    return mods_ref[pl.ds(row, 1), k * D_MODEL:(k + 1) * D_MODEL]


def _rms(x):
    return x * lax.rsqrt(jnp.mean(x * x, axis=-1, keepdims=True) + EPS)


def _ada_kernel(c_ref, w_ref, b_ref, o_ref):
    c = c_ref[...]
    s = (c * jax.nn.sigmoid(c)).astype(bf16)
    o_ref[...] = jnp.dot(s, w_ref[...].astype(bf16), preferred_element_type=f32) + b_ref[...]


def _ada(c_all, w_ada, b_ada):
    tn = 1024
    n = N_MOD * D_MODEL
    return pl.pallas_call(
        _ada_kernel,
        grid=(n // tn,),
        in_specs=[pl.BlockSpec((MOD_ROWS, D_MODEL), lambda j: (0, 0)),
                  pl.BlockSpec((D_MODEL, tn), lambda j: (0, j)),
                  pl.BlockSpec((1, tn), lambda j: (0, j))],
        out_specs=pl.BlockSpec((MOD_ROWS, tn), lambda j: (0, j)),
        out_shape=jax.ShapeDtypeStruct((MOD_ROWS, n), f32),
        compiler_params=_params(1),
        name="ada",
    )(c_all, w_ada, b_ada)


MOD_TILE = 1024
MOD_ROWS_PER_PASS = 128


def _mod_kernel(xp_ref, xs_ref, mods_ref, g_ref, o_ref):
    i = pl.program_id(0)
    n_ctx = T_CTX // MOD_TILE
    row = _mod_row(i, n_ctx, DEC_SEQ // MOD_TILE)
    sh = _mod_vec(mods_ref, row, 0)
    sc = _mod_vec(mods_ref, row, 1)

    gain = g_ref[...] * (1.0 + sc)

    def run(x_ref):
        def body(r, carry):
            rows = pl.ds(pl.multiple_of(r * MOD_ROWS_PER_PASS, MOD_ROWS_PER_PASS), MOD_ROWS_PER_PASS)
            o_ref[rows, :] = (_rms(x_ref[rows, :]) * gain + sh).astype(bf16)
            return carry

        lax.fori_loop(0, MOD_TILE // MOD_ROWS_PER_PASS, body, 0)

    @pl.when(i < n_ctx)
    def _():
        run(xp_ref)

    @pl.when(i >= n_ctx)
    def _():
        run(xs_ref)


def _modulate1(xp, xs, mods, g):
    n_ctx = T_CTX // MOD_TILE
    return pl.pallas_call(
        _mod_kernel,
        grid=(T_ALL // MOD_TILE,),
        in_specs=[pl.BlockSpec((MOD_TILE, D_MODEL), lambda i: (jnp.minimum(i, n_ctx - 1), 0)),
                  pl.BlockSpec((MOD_TILE, D_MODEL), lambda i: (jnp.maximum(i - n_ctx, 0), 0)),
                  pl.BlockSpec((MOD_ROWS, N_MOD * D_MODEL), lambda i: (0, 0)),
                  pl.BlockSpec((1, D_MODEL), lambda i: (0, 0))],
        out_specs=pl.BlockSpec((MOD_TILE, D_MODEL), lambda i: (i, 0)),
        out_shape=jax.ShapeDtypeStruct((T_ALL, D_MODEL), bf16),
        compiler_params=_params(1),
        name="modulate1",
    )(xp, xs, mods, g)


PROJ_TM = 1024
PROJ_TN = 1024
PROJ_CHUNK = 256
K_COL = (FOURIER_WIDTH + NA_WIDTH) // PROJ_TN
V_COL = (FOURIER_WIDTH + 2 * NA_WIDTH) // PROJ_TN


PROJ_STEPS = (IN_WIDTH // PROJ_TN) * (T_ALL // PROJ_TM)
CAST_SHAPES = ((N_EXPERTS * D_MODEL, EXPERT_HIDDEN, 512),
               (N_EXPERTS * D_MODEL, EXPERT_HIDDEN, 512),
               (N_EXPERTS * EXPERT_HIDDEN, D_MODEL, 128),
               (FOURIER_WIDTH, D_MODEL, 16),
               (NA_WIDTH, D_MODEL, 16),
               (D_MODEL, D_MODEL, 32))
N_CAST = len(CAST_SHAPES)


KV_ROWS = PROJ_TM * NA_HEADS


def _kv_copy(acc_ref, dst_hbm, tile, sem):
    rows = pl.ds(pl.multiple_of(tile * KV_ROWS, KV_ROWS), KV_ROWS)
    return pltpu.make_async_copy(acc_ref.at[tile % 2], dst_hbm.at[rows, :], sem)


def _proj_kernel(h_ref, w_ref, b_ref, *rest):
    cast_in = rest[:N_CAST]
    p_ref, k_hbm, v_hbm = rest[N_CAST:N_CAST + 3]
    cast_out = rest[N_CAST + 3:2 * N_CAST + 3]
    wb_ref, acc_ref, sem = rest[-3:]
    j = pl.program_id(0)
    i = pl.program_id(1)
    n_ctx = T_CTX // PROJ_TM
    is_k = (j == K_COL) & (i < n_ctx)
    is_v = (j == V_COL) & (i < n_ctx)

    @pl.when(i == 0)
    def _():
        wb_ref[...] = w_ref[...].astype(bf16)

    n_chunks = PROJ_TN // PROJ_CHUNK
    for c in range(n_chunks):
        cols = slice(c * PROJ_CHUNK, (c + 1) * PROJ_CHUNK)
        acc = jnp.dot(h_ref[...], wb_ref[:, cols], preferred_element_type=f32) + b_ref[:, cols]
        p_ref[:, cols] = acc.astype(bf16)
        for hh in range(PROJ_CHUNK // HEAD_DIM):
            head = c * (PROJ_CHUNK // HEAD_DIM) + hh
            acc_ref[i % 2, pl.ds(head, PROJ_TM, stride=NA_HEADS), :] = acc[:, hh * HEAD_DIM:(hh + 1) * HEAD_DIM]
        for src, dst in list(zip(cast_in, cast_out))[c::n_chunks]:
            dst[...] = src[...].astype(bf16)

    @pl.when(((j == K_COL) | (j == V_COL)) & (i >= 1) & (i <= n_ctx))
    def _():
        _kv_copy(acc_ref, k_hbm, i - 1, sem).wait()

    @pl.when(is_k)
    def _():
        _kv_copy(acc_ref, k_hbm, i, sem).start()

    @pl.when(is_v)
    def _():
        _kv_copy(acc_ref, v_hbm, i, sem).start()


def _cast_spec(rows, cols, rb):
    m = T_ALL // PROJ_TM
    return pl.BlockSpec((rb, cols), lambda j, i: (jnp.minimum(j * m + i, rows // rb - 1), 0))


def _project(h1, w_in, b_in, cast_weights):
    assert all(rows // rb <= PROJ_STEPS for rows, _, rb in CAST_SHAPES)
    assert T_CTX // PROJ_TM < T_ALL // PROJ_TM
    cast_specs = [_cast_spec(*s) for s in CAST_SHAPES]
    hbm = pl.BlockSpec(memory_space=pl.ANY)
    outs = pl.pallas_call(
        _proj_kernel,
        grid=(IN_WIDTH // PROJ_TN, T_ALL // PROJ_TM),
        in_specs=[pl.BlockSpec((PROJ_TM, D_MODEL), lambda j, i: (i, 0)),
                  pl.BlockSpec((D_MODEL, PROJ_TN), lambda j, i: (0, j)),
                  pl.BlockSpec((1, PROJ_TN), lambda j, i: (0, j))] + cast_specs,
        out_specs=[pl.BlockSpec((PROJ_TM, PROJ_TN), lambda j, i: (i, j)), hbm, hbm] + cast_specs,
        out_shape=[jax.ShapeDtypeStruct((T_ALL, IN_WIDTH), bf16),
                   jax.ShapeDtypeStruct((T_CTX * NA_HEADS, HEAD_DIM), f32),
                   jax.ShapeDtypeStruct((T_CTX * NA_HEADS, HEAD_DIM), f32)]
                  + [jax.ShapeDtypeStruct((rows, cols), bf16) for rows, cols, _ in CAST_SHAPES],
        scratch_shapes=[pltpu.VMEM((D_MODEL, PROJ_TN), bf16), pltpu.VMEM((2, KV_ROWS, HEAD_DIM), f32),
                        pltpu.SemaphoreType.DMA(())],
        compiler_params=_params(2),
        name="project",
    )(h1, w_in, b_in, *cast_weights)
    return outs[0], outs[1], outs[2], outs[3:]


Q_GROUP_ROWS = 4
N_Q_GROUPS = GRID_ROWS // Q_GROUP_ROWS
Q_GROUP = Q_GROUP_ROWS * GRID_W
KEY_SPAN_ROWS = 12
KEY_SPAN = KEY_SPAN_ROWS * GRID_W


def _window_start(r):
    return min(max(r - WIN_ROWS // 2, 0), GRID_ROWS - WIN_ROWS)


def _key_base(g):
    lo = _window_start(g * Q_GROUP_ROWS)
    hi = _window_start((g + 1) * Q_GROUP_ROWS - 1) + WIN_ROWS
    assert hi - lo <= KEY_SPAN_ROWS
    return min(lo, GRID_ROWS - KEY_SPAN_ROWS)


def _visible_span(g):
    base = _key_base(g)
    first = _window_start(g * Q_GROUP_ROWS) - base
    last = _window_start((g + 1) * Q_GROUP_ROWS - 1) + WIN_ROWS - base
    lo = (first // 2) * 2 * GRID_W
    hi = -(-last // 2) * 2 * GRID_W
    return lo, hi


def _block_shift(g, rr):
    return _key_base(g) - (g * Q_GROUP_ROWS + rr) + WIN_ROWS - 1


TABLE_PAD = -min(_block_shift(g, rr) for g in range(N_Q_GROUPS) for rr in range(Q_GROUP_ROWS))
TABLE_BLOCKS = 2 * (-(-(max(_block_shift(g, rr) for g in range(N_Q_GROUPS) for rr in range(Q_GROUP_ROWS))
                        + TABLE_PAD + KEY_SPAN_ROWS + 1) // 2))
TABLE_LANES = TABLE_BLOCKS * GRID_W


def _bias_kernel(r0_ref, r1_ref, o_ref):
    qc = lax.broadcasted_iota(jnp.int32, (GRID_W, TABLE_LANES), 0)
    lane = lax.broadcasted_iota(jnp.int32, (GRID_W, TABLE_LANES), 1)
    kc = lane & (GRID_W - 1)
    dc = kc - qc + (WIN_COLS - 1)
    start = jnp.clip(qc - WIN_COLS // 2, 0, GRID_W - WIN_COLS)
    ok = (kc >= start) & (kc < start + WIN_COLS)
    tables = []
    for r_ref in (r0_ref, r1_ref):
        acc = jnp.zeros((GRID_W, TABLE_LANES), f32)
        for b in range(2 * WIN_COLS - 1):
            acc = jnp.where(dc == b, r_ref[0, b:b + 1, :], acc)
        tables.append(jnp.where(ok, acc, NEG_INF))
    span_row = jnp.right_shift(lax.broadcasted_iota(jnp.int32, (GRID_W, KEY_SPAN), 1), GRID_W.bit_length() - 1)
    for g in range(N_Q_GROUPS):
        for rr in range(Q_GROUP_ROWS):
            first = _window_start(g * Q_GROUP_ROWS + rr) - _key_base(g)
            e = _block_shift(g, rr) + TABLE_PAD
            lo = (e - e % 2) * GRID_W
            in_window = (span_row >= first) & (span_row < first + WIN_ROWS)
            o_ref[0, g, rr * GRID_W:(rr + 1) * GRID_W, :] = jnp.where(
                in_window, tables[e % 2][:, lo:lo + KEY_SPAN], NEG_INF)


def _bias_table(rpb):
    nb = 2 * WIN_COLS - 1
    n_off = 2 * WIN_ROWS - 1
    by_row = jnp.transpose(rpb, (0, 2, 1))
    rpbx = []
    for p in range(2):
        spread = np.zeros((n_off, TABLE_LANES), np.float32)
        for t in range(TABLE_BLOCKS):
            if 0 <= t + p - TABLE_PAD < n_off:
                spread[t + p - TABLE_PAD, t * GRID_W:(t + 1) * GRID_W] = 1.0
        rpbx.append(jnp.einsum("hba,al->hbl", by_row, jnp.asarray(spread), precision=lax.Precision.HIGHEST))
    table_spec = pl.BlockSpec((1, nb, TABLE_LANES), lambda h: (h, 0, 0))
    return pl.pallas_call(
        _bias_kernel,
        grid=(NA_HEADS,),
        in_specs=[table_spec, table_spec],
        out_specs=pl.BlockSpec((1, N_Q_GROUPS, Q_GROUP, KEY_SPAN), lambda h: (h, 0, 0, 0)),
        out_shape=jax.ShapeDtypeStruct((NA_HEADS, N_Q_GROUPS, Q_GROUP, KEY_SPAN), f32),
        compiler_params=_params(1),
        name="bias_table",
    )(*rpbx)


def _qk(q, k):
    return lax.dot_general(q, k, (((1,), (1,)), ((), ())), preferred_element_type=f32)


def _ctx_mix_kernel(q_ref, k_ref, v_ref, u_ref, chan_ref, pos_ref, na_ref, fm_ref):
    scale = HEAD_DIM ** -0.5
    gd = FOURIER_GROUP_DIM

    def head(h):
        sl = slice(h * HEAD_DIM, (h + 1) * HEAD_DIM)
        s = _qk(q_ref[:, sl], k_ref[:, sl]) * scale
        p = jnp.exp(s - jnp.max(s, axis=-1, keepdims=True))
        l = jnp.sum(p, axis=-1, keepdims=True)
        o = jnp.dot(p.astype(bf16), v_ref[:, sl], preferred_element_type=f32) / l
        na_ref[:, sl] = o.astype(bf16)

    ys = []
    for g in range(FOURIER_GROUPS):
        head(g)
        ys.append(jnp.dot(u_ref[:, g * gd:(g + 1) * gd], chan_ref[...], preferred_element_type=f32))
    yc = jnp.concatenate([y[:, :gd] for y in ys], axis=1)
    ysn = jnp.concatenate([y[:, gd:] for y in ys], axis=1)
    stacked = jnp.concatenate([yc, ysn], axis=0).astype(bf16)
    for h in range(FOURIER_GROUPS, NA_HEADS):
        head(h)
        if h == FOURIER_GROUPS:
            fm_ref[...] = jnp.dot(pos_ref[...], stacked, preferred_element_type=f32).astype(bf16)


def _ctx_mix(p_all):
    qcol = FOURIER_WIDTH // NA_WIDTH
    chan, pos = _dft_consts(SEQ)
    return pl.pallas_call(
        _ctx_mix_kernel,
        grid=(BATCH,),
        in_specs=[pl.BlockSpec((SEQ, NA_WIDTH), lambda b: (b, qcol)),
                  pl.BlockSpec((SEQ, NA_WIDTH), lambda b: (b, qcol + 1)),
                  pl.BlockSpec((SEQ, NA_WIDTH), lambda b: (b, qcol + 2)),
                  pl.BlockSpec((SEQ, FOURIER_WIDTH), lambda b: (b, 0)),
                  pl.BlockSpec((FOURIER_GROUP_DIM, 2 * FOURIER_GROUP_DIM), lambda b: (0, 0)),
                  pl.BlockSpec((SEQ, 2 * SEQ), lambda b: (0, 0))],
        out_specs=[pl.BlockSpec((SEQ, NA_WIDTH), lambda b: (b, 0)),
                   pl.BlockSpec((SEQ, FOURIER_WIDTH), lambda b: (b, 0))],
        out_shape=[jax.ShapeDtypeStruct((T_CTX, NA_WIDTH), bf16),
                   jax.ShapeDtypeStruct((T_CTX, FOURIER_WIDTH), bf16)],
        compiler_params=_params(1),
        name="ctx_mix",
    )(p_all, p_all, p_all, p_all, chan, pos)


def _attn_lat_kernel(q_ref, k_ref, v_ref, kc_ref, vc_ref, bias_ref, o_ref):
    scale = HEAD_DIM ** -0.5
    head_rows = pl.ds(pl.program_id(0), PAST_LEN, stride=NA_HEADS)
    kcb = kc_ref[0, head_rows, :].astype(bf16)
    vcb = vc_ref[0, head_rows, :].astype(bf16)
    for g in range(N_Q_GROUPS):
        lo, hi = _visible_span(g)
        k0 = _key_base(g) * GRID_W + lo
        nk = hi - lo
        q = q_ref[g * Q_GROUP:(g + 1) * Q_GROUP, :]
        s_nb = _qk(q, k_ref[k0:k0 + nk, :]) * scale + bias_ref[0, g, :, lo:hi]
        s_cx = _qk(q, kcb) * scale
        m = jnp.maximum(jnp.max(s_nb, axis=-1, keepdims=True), jnp.max(s_cx, axis=-1, keepdims=True))
        p_nb = jnp.exp(s_nb - m)
        p_cx = jnp.exp(s_cx - m)
        l = jnp.sum(p_nb, axis=-1, keepdims=True) + jnp.sum(p_cx, axis=-1, keepdims=True)
        o = (jnp.dot(p_nb.astype(bf16), v_ref[k0:k0 + nk, :], preferred_element_type=f32)
             + jnp.dot(p_cx.astype(bf16), vcb, preferred_element_type=f32)) / l
        o_ref[g * Q_GROUP:(g + 1) * Q_GROUP, :] = o.astype(bf16)


def _attn_lat(p_all, ck, cv, bias):
    row0 = T_CTX // DEC_SEQ
    qcol = FOURIER_WIDTH // HEAD_DIM
    return pl.pallas_call(
        _attn_lat_kernel,
        grid=(NA_HEADS, DEC_BATCH),
        in_specs=[pl.BlockSpec((DEC_SEQ, HEAD_DIM), lambda h, b: (row0 + b, qcol + h)),
                  pl.BlockSpec((DEC_SEQ, HEAD_DIM), lambda h, b: (row0 + b, qcol + NA_HEADS + h)),
                  pl.BlockSpec((DEC_SEQ, HEAD_DIM), lambda h, b: (row0 + b, qcol + 2 * NA_HEADS + h)),
                  pl.BlockSpec((1, PAST_LEN * NA_HEADS, HEAD_DIM), lambda h, b: (b, 0, 0)),
                  pl.BlockSpec((1, PAST_LEN * NA_HEADS, HEAD_DIM), lambda h, b: (b, 0, 0)),
                  pl.BlockSpec((1, N_Q_GROUPS, Q_GROUP, KEY_SPAN), lambda h, b: (h, 0, 0, 0))],
        out_specs=pl.BlockSpec((DEC_SEQ, HEAD_DIM), lambda h, b: (b, h)),
        out_shape=jax.ShapeDtypeStruct((T_LAT, NA_WIDTH), bf16),
        compiler_params=_params(2),
        name="attn_lat",
    )(p_all, p_all, p_all, ck, cv, bias)


def _dft_consts(n):
    def cs(m):
        idx = (np.arange(m)[:, None] * np.arange(m)[None, :]) % m
        ang = 2.0 * np.pi * idx.astype(np.float64) / m
        return np.cos(ang), np.sin(ang)

    cc, sc = cs(FOURIER_GROUP_DIM)
    cn, sn = cs(n)
    chan = np.concatenate([cc, sc], axis=1) / np.sqrt(FOURIER_GROUP_DIM)
    pos = np.concatenate([cn, -sn], axis=1) / np.sqrt(n)
    return jnp.asarray(chan, f32).astype(bf16), jnp.asarray(pos, f32).astype(bf16)


def _fourier_kernel(u_ref, chan_ref, pos_ref, o_ref):
    gd = FOURIER_GROUP_DIM
    ys = [jnp.dot(u_ref[:, g * gd:(g + 1) * gd], chan_ref[...], preferred_element_type=f32)
          for g in range(FOURIER_GROUPS)]
    yc = jnp.concatenate([y[:, :gd] for y in ys], axis=1)
    ysn = jnp.concatenate([y[:, gd:] for y in ys], axis=1)
    stacked = jnp.concatenate([yc, ysn], axis=0).astype(bf16)
    o_ref[...] = jnp.dot(pos_ref[...], stacked, preferred_element_type=f32).astype(bf16)


def _fourier(p_all, n, n_req, row0):
    chan, pos = _dft_consts(n)
    return pl.pallas_call(
        _fourier_kernel,
        grid=(n_req,),
        in_specs=[pl.BlockSpec((n, FOURIER_WIDTH), lambda b: (row0 + b, 0)),
                  pl.BlockSpec((FOURIER_GROUP_DIM, 2 * FOURIER_GROUP_DIM), lambda b: (0, 0)),
                  pl.BlockSpec((n, 2 * n), lambda b: (0, 0))],
        out_specs=pl.BlockSpec((n, FOURIER_WIDTH), lambda b: (b, 0)),
        out_shape=jax.ShapeDtypeStruct((n_req * n, FOURIER_WIDTH), bf16),
        compiler_params=_params(1),
        name=f"fourier_{n}",
    )(p_all, chan, pos)


MERGE_TM = 256
ROUTE_LANES = LANES


def _class_of(logits):
    lane = lax.broadcasted_iota(jnp.int32, logits.shape, 1)
    big = jnp.int32(ROUTE_LANES)
    is_group = lane < N_GROUPS
    mg = jnp.max(jnp.where(is_group, logits, -jnp.inf), axis=-1, keepdims=True)
    gsel = jnp.min(jnp.where(is_group & (logits == mg), lane, big), axis=-1, keepdims=True)
    lo = N_GROUPS + EXPERTS_PER_GROUP * gsel
    in_group = (lane >= lo) & (lane < lo + EXPERTS_PER_GROUP)
    m1 = jnp.max(jnp.where(in_group, logits, -jnp.inf), axis=-1, keepdims=True)
    i1 = jnp.min(jnp.where(in_group & (logits == m1), lane, big), axis=-1, keepdims=True)
    rest = in_group & (lane != i1)
    m2 = jnp.max(jnp.where(rest, logits, -jnp.inf), axis=-1, keepdims=True)
    i2 = jnp.min(jnp.where(rest & (logits == m2), lane, big), axis=-1, keepdims=True)
    e_lo = jnp.minimum(i1, i2) - lo
    e_hi = jnp.maximum(i1, i2) - lo
    pair = jnp.zeros_like(e_lo)
    for idx in range(N_PAIRS):
        a, b = sorted((PAIR_A[idx], PAIR_B[idx]))
        pair = jnp.where((e_lo == a) & (e_hi == b), idx, pair)
    return gsel * N_PAIRS + pair


def _merge_kernel(fmc_ref, fml_ref, nac_ref, nal_ref, ga_ref, gb_ref, xp_ref, xs_ref, mods_ref, g2_ref,
                  wf_ref, wn_ref, wo_ref, wr_ref, br_ref,
                  x1_ref, h2c_ref, info_ref, cnt_ref, carry_ref, x1s_ref):
    s = pl.program_id(0)
    n_ctx = T_CTX // MERGE_TM
    n_all = T_ALL // MERGE_TM
    per_req = DEC_SEQ // MERGE_TM
    row = _mod_row(jnp.minimum(s, n_all - 1), n_ctx, per_req)
    prev_row = _mod_row(jnp.maximum(s - 1, 0), n_ctx, per_req)

    @pl.when(s == 0)
    def _():
        carry_ref[...] = jnp.zeros_like(carry_ref)
        x1s_ref[1] = jnp.zeros((MERGE_TM, D_MODEL), f32)

    def step(tile_refs):
        x1p = x1s_ref[(s + 1) % 2]
        if tile_refs:
            fm_ref, na_ref, x_ref = tile_refs
            a = jnp.dot(fm_ref[...], wf_ref[...], preferred_element_type=f32)

        h2 = _rms(x1p) * g2_ref[...] * (1.0 + _mod_vec(mods_ref, prev_row, 4)) + _mod_vec(mods_ref, prev_row, 3)
        if tile_refs:
            gate_a = jax.nn.sigmoid(ga_ref[...].astype(f32))
            b = jnp.dot(na_ref[...], wn_ref[...], preferred_element_type=f32)

        for k in range(CHUNKS):
            h2c_ref[pl.ds(k, MERGE_TM, stride=CHUNKS), :] = h2[:, k * LANES:(k + 1) * LANES]
        logits = jnp.dot(h2.astype(bf16), wr_ref[...], preferred_element_type=f32) + br_ref[...]
        if tile_refs:
            merged = gate_a * a + jax.nn.sigmoid(gb_ref[...].astype(f32)) * b
            out = jnp.dot(merged.astype(bf16), wo_ref[...], preferred_element_type=f32)

        cls = _class_of(logits)
        lane = lax.broadcasted_iota(jnp.int32, logits.shape, 1)
        onehot = ((lane == cls) & (s > 0)).astype(f32)
        r = lax.broadcasted_iota(jnp.int32, (MERGE_TM, MERGE_TM), 0)
        c = lax.broadcasted_iota(jnp.int32, (MERGE_TM, MERGE_TM), 1)
        before = (c < r).astype(bf16)
        prefix = jnp.dot(before, onehot.astype(bf16), preferred_element_type=f32) + carry_ref[...]
        rank = jnp.sum(onehot * prefix, axis=-1, keepdims=True)
        info_ref[...] = jnp.where(lane == 0, cls.astype(f32), jnp.where(lane == 1, rank, 0.0))
        carry_ref[...] = carry_ref[...] + jnp.sum(onehot, axis=0, keepdims=True)
        cnt_ref[...] = jnp.broadcast_to(carry_ref[...], cnt_ref.shape)
        if tile_refs:
            x1 = x_ref[...] + _mod_vec(mods_ref, row, 2) * out
            x1_ref[...] = x1
            x1s_ref[s % 2] = x1

    @pl.when(s < n_ctx)
    def _():
        step((fmc_ref, nac_ref, xp_ref))

    @pl.when((s >= n_ctx) & (s < n_all))
    def _():
        step((fml_ref, nal_ref, xs_ref))

    @pl.when(s == n_all)
    def _():
        step(None)


def _merge(fm_c, fm_l, na_c, na_l, p_all, xp, xs, mods, g2, wf, wn, wo, wr, br):
    n_ctx = T_CTX // MERGE_TM
    n_lat = T_LAT // MERGE_TM
    n_all = n_ctx + n_lat
    gcol = (FOURIER_WIDTH + 3 * NA_WIDTH) // D_MODEL
    const = lambda i: (0, 0)
    ctx_tile = lambda i: (jnp.minimum(i, n_ctx - 1), 0)
    lat_tile = lambda i: (jnp.clip(i - n_ctx, 0, n_lat - 1), 0)
    this_tile = lambda i: jnp.minimum(i, n_all - 1)
    prev_tile = lambda i: (jnp.maximum(i - 1, 0), 0)
    once = pl.Buffered(1)
    return pl.pallas_call(
        _merge_kernel,
        grid=(n_all + 1,),
        in_specs=[pl.BlockSpec((MERGE_TM, FOURIER_WIDTH), ctx_tile),
                  pl.BlockSpec((MERGE_TM, FOURIER_WIDTH), lat_tile),
                  pl.BlockSpec((MERGE_TM, NA_WIDTH), ctx_tile),
                  pl.BlockSpec((MERGE_TM, NA_WIDTH), lat_tile),
                  pl.BlockSpec((MERGE_TM, D_MODEL), lambda i: (this_tile(i), gcol)),
                  pl.BlockSpec((MERGE_TM, D_MODEL), lambda i: (this_tile(i), gcol + 1)),
                  pl.BlockSpec((MERGE_TM, D_MODEL), ctx_tile),
                  pl.BlockSpec((MERGE_TM, D_MODEL), lat_tile),
                  pl.BlockSpec((MOD_ROWS, N_MOD * D_MODEL), const, pipeline_mode=once),
                  pl.BlockSpec((1, D_MODEL), const, pipeline_mode=once),
                  pl.BlockSpec((FOURIER_WIDTH, D_MODEL), const, pipeline_mode=once),
                  pl.BlockSpec((NA_WIDTH, D_MODEL), const, pipeline_mode=once),
                  pl.BlockSpec((D_MODEL, D_MODEL), const, pipeline_mode=once),
                  pl.BlockSpec((D_MODEL, ROUTE_LANES), const, pipeline_mode=once),
                  pl.BlockSpec((1, ROUTE_LANES), const, pipeline_mode=once)],
        out_specs=[pl.BlockSpec((MERGE_TM, D_MODEL), lambda i: (this_tile(i), 0)),
                   pl.BlockSpec((MERGE_TM * CHUNKS, LANES), prev_tile),
                   pl.BlockSpec((MERGE_TM, ROUTE_LANES), prev_tile),
                   pl.BlockSpec((8, ROUTE_LANES), const)],
        out_shape=[jax.ShapeDtypeStruct((T_ALL, D_MODEL), f32),
                   jax.ShapeDtypeStruct((T_ALL * CHUNKS, LANES), f32),
                   jax.ShapeDtypeStruct((T_ALL, ROUTE_LANES), f32),
                   jax.ShapeDtypeStruct((8, ROUTE_LANES), f32)],
        scratch_shapes=[pltpu.VMEM((1, ROUTE_LANES), f32), pltpu.VMEM((2, MERGE_TM, D_MODEL), f32)],
        compiler_params=_params(1),
        name="merge",
    )(fm_c, fm_l, na_c, na_l, p_all, p_all, xp, xs, mods, g2, wf, wn, wo, wr, br)


def _perm_kernel(pos_ref, src_ref):
    def clear(p, carry):
        src_ref[p] = 0
        return carry

    lax.fori_loop(0, T_PAD, clear, 0, unroll=16)

    def place(t, carry):
        src_ref[pos_ref[t]] = t
        return carry

    lax.fori_loop(0, T_ALL, place, 0, unroll=16)


def _inverse_positions(pos):
    smem = pl.BlockSpec(memory_space=pltpu.SMEM)
    return pl.pallas_call(
        _perm_kernel,
        in_specs=[smem],
        out_specs=smem,
        out_shape=jax.ShapeDtypeStruct((T_PAD,), jnp.int32),
        name="permutation",
    )(pos)


def _row_copy(idx_ref, base, src_hbm, buf, sem, slot, k):
    tok = idx_ref[base + k]
    return pltpu.make_async_copy(src_hbm.at[pl.ds(pl.multiple_of(tok * CHUNKS, CHUNKS), CHUNKS), :],
                                 buf.at[slot, pl.ds(pl.multiple_of(k * CHUNKS, CHUNKS), CHUNKS), :],
                                 sem.at[slot])


def _start_gather(idx_ref, base, src_hbm, buf, sem, slot, rows, queues=(0, 1)):
    def body(k8, carry):
        for j in range(8):
            _row_copy(idx_ref, base, src_hbm, buf, sem, slot, k8 * 8 + j).start(priority=queues[j % len(queues)])
        return carry

    lax.fori_loop(0, rows // 8, body, 0)


def _wait_gather(buf, sem, slot):
    pltpu.make_async_copy(buf.at[slot], buf.at[slot], sem.at[slot]).wait()


def _gathered_rows(buf, slot, rows):
    return jnp.concatenate([buf[slot, pl.ds(s, rows, stride=CHUNKS), :] for s in range(CHUNKS)], axis=1)


def _moe_kernel(ea_ref, eb_ref, nt_ref, src_ref, h2c_hbm, wga_ref, wua_ref, wda_ref, wgb_ref, wub_ref, wdb_ref,
                wr_ref, br_ref, o_ref, buf, sem):
    i = pl.program_id(0)
    nt = nt_ref[0]

    @pl.when(i >= nt)
    def _():
        o_ref[...] = jnp.zeros_like(o_ref)

    def fetch(tile):
        _start_gather(src_ref, tile * MOE_TILE, h2c_hbm, buf, sem, tile % MOE_SLOTS, MOE_TILE, queues=(1,))

    @pl.when(i == 0)
    def _():
        for t in range(MOE_AHEAD):
            @pl.when(t < nt)
            def _(t=t):
                fetch(t)

    @pl.when(i + MOE_AHEAD < nt)
    def _():
        fetch(i + MOE_AHEAD)

    @pl.when(i < nt)
    def _():
        slot = i % MOE_SLOTS
        _wait_gather(buf, sem, slot)
        x = _gathered_rows(buf, slot, MOE_TILE).astype(bf16)
        logits = jnp.dot(x, wr_ref[...], preferred_element_type=f32) + br_ref[...]
        lane = lax.broadcasted_iota(jnp.int32, logits.shape, 1)
        ea = ea_ref[i]
        eb = eb_ref[i]

        def pick(col):
            return jnp.sum(jnp.where(lane == col, logits, 0.0), axis=-1, keepdims=True)

        lg = pick(ea // EXPERTS_PER_GROUP)
        p_group = 1.0 / jnp.sum(jnp.where(lane < N_GROUPS, jnp.exp(logits - lg), 0.0), axis=-1, keepdims=True)
        la = pick(N_GROUPS + ea)
        lb = pick(N_GROUPS + eb)
        w_a = p_group / (1.0 + jnp.exp(lb - la))
        w_b = p_group / (1.0 + jnp.exp(la - lb))

        def up(wg_ref, wu_ref):
            return (jnp.dot(x, wg_ref[0], preferred_element_type=f32),
                    jnp.dot(x, wu_ref[0], preferred_element_type=f32))

        def act(g, u):
            return (g * jax.nn.sigmoid(g) * u).astype(bf16)

        g_a, u_a = up(wga_ref, wua_ref)
        g_b, u_b = up(wgb_ref, wub_ref)
        hid_a = act(g_a, u_a)
        hid_b = act(g_b, u_b)
        per_chunk = MOE_OUT_CHUNK // LANES
        for c in range(D_MODEL // MOE_OUT_CHUNK):
            cols = slice(c * MOE_OUT_CHUNK, (c + 1) * MOE_OUT_CHUNK)
            y = (w_a * jnp.dot(hid_a, wda_ref[0, :, cols], preferred_element_type=f32)
                 + w_b * jnp.dot(hid_b, wdb_ref[0, :, cols], preferred_element_type=f32))
            for k in range(per_chunk):
                o_ref[pl.ds(c * per_chunk + k, MOE_TILE, stride=CHUNKS), :] = y[:, k * LANES:(k + 1) * LANES]


def _moe(tile_ea, tile_eb, ntiles, src, h2c, wg, wu, wd, wr, br):
    up = lambda tab: (lambda i, ea, eb, nt, src: (tab(ea, eb)[i], 0, 0))
    pick_a = lambda ea, eb: ea
    pick_b = lambda ea, eb: eb
    const = lambda i, ea, eb, nt, src: (0, 0)
    grid_spec = pltpu.PrefetchScalarGridSpec(
        num_scalar_prefetch=4,
        grid=(MOE_MAX_TILES,),
        in_specs=[pl.BlockSpec(memory_space=pl.ANY),
                  pl.BlockSpec((1, D_MODEL, EXPERT_HIDDEN), up(pick_a)),
                  pl.BlockSpec((1, D_MODEL, EXPERT_HIDDEN), up(pick_a)),
                  pl.BlockSpec((1, EXPERT_HIDDEN, D_MODEL), up(pick_a)),
                  pl.BlockSpec((1, D_MODEL, EXPERT_HIDDEN), up(pick_b)),
                  pl.BlockSpec((1, D_MODEL, EXPERT_HIDDEN), up(pick_b)),
                  pl.BlockSpec((1, EXPERT_HIDDEN, D_MODEL), up(pick_b)),
                  pl.BlockSpec((D_MODEL, ROUTE_LANES), const),
                  pl.BlockSpec((1, ROUTE_LANES), const)],
        out_specs=pl.BlockSpec((MOE_TILE * CHUNKS, LANES), lambda i, ea, eb, nt, src: (i, 0)),
        scratch_shapes=[pltpu.VMEM((MOE_SLOTS, MOE_TILE * CHUNKS, LANES), f32),
                        pltpu.SemaphoreType.DMA((MOE_SLOTS,))],
    )
    return pl.pallas_call(
        _moe_kernel,
        grid_spec=grid_spec,
        out_shape=jax.ShapeDtypeStruct((T_PAD * CHUNKS, LANES), f32),
        compiler_params=_params(1),
        name="moe",
    )(tile_ea, tile_eb, ntiles, src, h2c, wg, wu, wd, wg, wu, wd, wr, br)


def _final_kernel(pos_ref, ys_hbm, x1_ref, mods_ref, g_ref, o_ref, buf, sem, *, tile0, n_tiles, n_ctx_tiles):
    i = pl.program_id(0)

    def fetch(tile):
        _start_gather(pos_ref, (tile0 + tile) * FINAL_TILE, ys_hbm, buf, sem, tile % FINAL_SLOTS, FINAL_TILE)

    @pl.when(i == 0)
    def _():
        for t in range(min(FINAL_AHEAD, n_tiles)):
            fetch(t)

    @pl.when(i + FINAL_AHEAD < n_tiles)
    def _():
        fetch(i + FINAL_AHEAD)

    slot = i % FINAL_SLOTS
    _wait_gather(buf, sem, slot)
    y = _gathered_rows(buf, slot, FINAL_TILE)
    row = _mod_row(tile0 + i, n_ctx_tiles, DEC_SEQ // FINAL_TILE)
    x2 = x1_ref[...] + _mod_vec(mods_ref, row, 5) * y
    o_ref[...] = _rms(x2) * g_ref[...]


def _final(pos, ys, x1, mods, g, tile0, n_tiles):
    body = functools.partial(_final_kernel, tile0=tile0, n_tiles=n_tiles, n_ctx_tiles=T_CTX // FINAL_TILE)
    grid_spec = pltpu.PrefetchScalarGridSpec(
        num_scalar_prefetch=1,
        grid=(n_tiles,),
        in_specs=[pl.BlockSpec(memory_space=pl.ANY),
                  pl.BlockSpec((FINAL_TILE, D_MODEL), lambda i, pos: (tile0 + i, 0)),
                  pl.BlockSpec((MOD_ROWS, N_MOD * D_MODEL), lambda i, pos: (0, 0)),
                  pl.BlockSpec((1, D_MODEL), lambda i, pos: (0, 0))],
        out_specs=pl.BlockSpec((FINAL_TILE, D_MODEL), lambda i, pos: (i, 0)),
        scratch_shapes=[pltpu.VMEM((FINAL_SLOTS, FINAL_TILE * CHUNKS, LANES), f32),
                        pltpu.SemaphoreType.DMA((FINAL_SLOTS,))],
    )
    return pl.pallas_call(
        body,
        grid_spec=grid_spec,
        out_shape=jax.ShapeDtypeStruct((n_tiles * FINAL_TILE, D_MODEL), f32),
        compiler_params=_params(1),
        name=f"final_{tile0}",
    )(pos, ys, x1, mods, g)


def kernel(x_prompt, x_sample, cache_k_ctx, cache_v_ctx, c, c_ctx, norm1_g, norm2_g, w_ada, b_ada, w_in, b_in,
           w_fourier, w_na_o, rpb, w_out, w_router_group, b_router_group, w_router_expert, b_router_expert,
           w_exp_gate, w_exp_up, w_exp_down, final_norm_g):
    xp = x_prompt.reshape(T_CTX, D_MODEL)
    xs = x_sample.reshape(T_LAT, D_MODEL)
    c_all = jnp.concatenate([c_ctx[None, :], c, jnp.zeros((MOD_ROWS - 1 - DEC_BATCH, D_MODEL), f32)], axis=0)
    w_route = jnp.concatenate([w_router_group[0], w_router_expert[0]], axis=1)
    w_route = jnp.pad(w_route, ((0, 0), (0, ROUTE_LANES - w_route.shape[1]))).astype(bf16)
    b_route = jnp.concatenate([b_router_group[0], b_router_expert[0]])
    b_route = jnp.pad(b_route, (0, ROUTE_LANES - b_route.shape[0]))[None, :]

    mods = _ada(c_all, w_ada[0], b_ada[0][None, :])
    h1 = _modulate1(xp, xs, mods, norm1_g)
    p_all, k_new, v_new, (wg, wu, wd, wf, wn, wo) = _project(
        h1, w_in[0], b_in,
        (w_exp_gate.reshape(N_EXPERTS * D_MODEL, EXPERT_HIDDEN), w_exp_up.reshape(N_EXPERTS * D_MODEL, EXPERT_HIDDEN),
         w_exp_down.reshape(N_EXPERTS * EXPERT_HIDDEN, D_MODEL), w_fourier[0], w_na_o[0], w_out[0]))
    wg = wg.reshape(N_EXPERTS, D_MODEL, EXPERT_HIDDEN)
    wu = wu.reshape(N_EXPERTS, D_MODEL, EXPERT_HIDDEN)
    wd = wd.reshape(N_EXPERTS, EXPERT_HIDDEN, D_MODEL)

    tt = _bias_table(rpb[0])
    na_c, fm_c = _ctx_mix(p_all)
    ck = cache_k_ctx.reshape(DEC_BATCH, PAST_LEN * NA_HEADS, HEAD_DIM)
    cv = cache_v_ctx.reshape(DEC_BATCH, PAST_LEN * NA_HEADS, HEAD_DIM)
    na_l = _attn_lat(p_all, ck, cv, tt)
    fm_l = _fourier(p_all, DEC_SEQ, DEC_BATCH, T_CTX // DEC_SEQ)

    x1, h2c, info, counts = _merge(fm_c, fm_l, na_c, na_l, p_all, xp, xs, mods, norm2_g, wf, wn, wo,
                                   w_route, b_route)

    cnt = counts[0, :N_CLASSES].astype(jnp.int32)
    padded = ((cnt + MOE_TILE - 1) // MOE_TILE) * MOE_TILE
    off_end = jnp.cumsum(padded)
    off = off_end - padded
    ntiles = (off_end[-1] // MOE_TILE).astype(jnp.int32)
    tile_start = jnp.arange(MOE_MAX_TILES, dtype=jnp.int32) * MOE_TILE
    last_start = (ntiles - 1) * MOE_TILE
    tile_cls = jnp.sum(jnp.minimum(tile_start, last_start)[:, None] >= off_end[None, :], axis=1)
    pair = tile_cls % N_PAIRS
    group = tile_cls // N_PAIRS
    pair_hot = pair[:, None] == jnp.arange(N_PAIRS)[None, :]
    tile_ea = (group * EXPERTS_PER_GROUP + jnp.sum(pair_hot * jnp.asarray(PAIR_A), axis=1)).astype(jnp.int32)
    tile_eb = (group * EXPERTS_PER_GROUP + jnp.sum(pair_hot * jnp.asarray(PAIR_B), axis=1)).astype(jnp.int32)

    class_hot = info[:, 0:1] == jnp.arange(N_CLASSES, dtype=f32)[None, :]
    pos = (info[:, 1] + jnp.sum(jnp.where(class_hot, off.astype(f32)[None, :], 0.0), axis=1)).astype(jnp.int32)
    src = _inverse_positions(pos)
    ys = _moe(tile_ea, tile_eb, ntiles[None], src, h2c, wg, wu, wd, w_route, b_route)

    fg = final_norm_g[None, :]
    y_prompt = _final(pos, ys, x1, mods, fg, 0, T_CTX // FINAL_TILE)
    y_sample = _final(pos, ys, x1, mods, fg, T_CTX // FINAL_TILE, T_LAT // FINAL_TILE)

    shape_kv = (BATCH, 1, SEQ, NA_HEADS, HEAD_DIM)
    return (y_prompt.reshape(BATCH, SEQ, D_MODEL), y_sample.reshape(DEC_BATCH, DEC_SEQ, D_MODEL),
            k_new.reshape(shape_kv), v_new.reshape(shape_kv))
```

```python
import functools

import jax
import jax.numpy as jnp
import numpy as np
from jax import lax
from jax.experimental import pallas as pl
from jax.experimental.pallas import tpu as pltpu

D_MODEL = 2048
BATCH = 32
SEQ = 256
DEC_BATCH = 4
DEC_SEQ = 1024
PAST_LEN = 256
GRID_W = 64
GRID_ROWS = DEC_SEQ // GRID_W
NA_HEADS = 8
HEAD_DIM = 128
NA_WIDTH = NA_HEADS * HEAD_DIM
FOURIER_GROUPS = 4
FOURIER_GROUP_DIM = 256
FOURIER_WIDTH = FOURIER_GROUPS * FOURIER_GROUP_DIM
WIN_ROWS = 8
WIN_COLS = 16
N_GROUPS = 4
EXPERTS_PER_GROUP = 4
N_EXPERTS = N_GROUPS * EXPERTS_PER_GROUP
EXPERT_HIDDEN = 512
N_MOD = 6
IN_WIDTH = FOURIER_WIDTH + 3 * NA_WIDTH + 2 * D_MODEL
EPS = 1e-6
NEG_INF = -1e30

T_CTX = BATCH * SEQ
T_LAT = DEC_BATCH * DEC_SEQ
T_ALL = T_CTX + T_LAT

LANES = 128
CHUNKS = D_MODEL // LANES
MOD_ROWS = 8

PAIR_A = (0, 0, 0, 1, 1, 3)
PAIR_B = (1, 2, 3, 3, 2, 2)
N_PAIRS = len(PAIR_A)
N_CLASSES = N_GROUPS * N_PAIRS
MOE_TILE = 256
MOE_OUT_CHUNK = 512
MOE_AHEAD = 3
MOE_SLOTS = MOE_AHEAD + 1
MOE_MAX_TILES = (T_ALL + N_CLASSES * (MOE_TILE - 1)) // MOE_TILE
FINAL_TILE = 256
FINAL_AHEAD = 3
FINAL_SLOTS = FINAL_AHEAD + 1
T_PAD = MOE_MAX_TILES * MOE_TILE

VMEM_LIMIT = 56 * 1024 * 1024

bf16 = jnp.bfloat16
f32 = jnp.float32


def _params(n_axes, vmem=VMEM_LIMIT):
    return pltpu.CompilerParams(dimension_semantics=("arbitrary",) * n_axes, vmem_limit_bytes=vmem)


def _mod_row(tile, n_ctx_tiles, tiles_per_request):
    return jnp.where(tile < n_ctx_tiles, 0, 1 + (tile - n_ctx_tiles) // tiles_per_request)


def _mod_vec(mods_ref, row, k):
    return mods_ref[pl.ds(row, 1), k * D_MODEL:(k + 1) * D_MODEL]


def _rms(x):
    return x * lax.rsqrt(jnp.mean(x * x, axis=-1, keepdims=True) + EPS)


def _ada_kernel(c_ref, w_ref, b_ref, o_ref):
    c = c_ref[...]
    s = (c * jax.nn.sigmoid(c)).astype(bf16)
    o_ref[...] = jnp.dot(s, w_ref[...].astype(bf16), preferred_element_type=f32) + b_ref[...]


def _ada(c_all, w_ada, b_ada):
    tn = 1024
    n = N_MOD * D_MODEL
    return pl.pallas_call(
        _ada_kernel,
        grid=(n // tn,),
        in_specs=[pl.BlockSpec((MOD_ROWS, D_MODEL), lambda j: (0, 0)),
                  pl.BlockSpec((D_MODEL, tn), lambda j: (0, j)),
                  pl.BlockSpec((1, tn), lambda j: (0, j))],
        out_specs=pl.BlockSpec((MOD_ROWS, tn), lambda j: (0, j)),
        out_shape=jax.ShapeDtypeStruct((MOD_ROWS, n), f32),
        compiler_params=_params(1),
        name="ada",
    )(c_all, w_ada, b_ada)


MOD_TILE = 1024
MOD_ROWS_PER_PASS = 128


def _mod_kernel(xp_ref, xs_ref, mods_ref, g_ref, o_ref):
    i = pl.program_id(0)
    n_ctx = T_CTX // MOD_TILE
    row = _mod_row(i, n_ctx, DEC_SEQ // MOD_TILE)
    sh = _mod_vec(mods_ref, row, 0)
    sc = _mod_vec(mods_ref, row, 1)

    gain = g_ref[...] * (1.0 + sc)

    def run(x_ref):
        def body(r, carry):
            rows = pl.ds(pl.multiple_of(r * MOD_ROWS_PER_PASS, MOD_ROWS_PER_PASS), MOD_ROWS_PER_PASS)
            o_ref[rows, :] = (_rms(x_ref[rows, :]) * gain + sh).astype(bf16)
            return carry

        lax.fori_loop(0, MOD_TILE // MOD_ROWS_PER_PASS, body, 0)

    @pl.when(i < n_ctx)
    def _():
        run(xp_ref)

    @pl.when(i >= n_ctx)
    def _():
        run(xs_ref)


def _modulate1(xp, xs, mods, g):
    n_ctx = T_CTX // MOD_TILE
    return pl.pallas_call(
        _mod_kernel,
        grid=(T_ALL // MOD_TILE,),
        in_specs=[pl.BlockSpec((MOD_TILE, D_MODEL), lambda i: (jnp.minimum(i, n_ctx - 1), 0)),
                  pl.BlockSpec((MOD_TILE, D_MODEL), lambda i: (jnp.maximum(i - n_ctx, 0), 0)),
                  pl.BlockSpec((MOD_ROWS, N_MOD * D_MODEL), lambda i: (0, 0)),
                  pl.BlockSpec((1, D_MODEL), lambda i: (0, 0))],
        out_specs=pl.BlockSpec((MOD_TILE, D_MODEL), lambda i: (i, 0)),
        out_shape=jax.ShapeDtypeStruct((T_ALL, D_MODEL), bf16),
        compiler_params=_params(1),
        name="modulate1",
    )(xp, xs, mods, g)


PROJ_TM = 1024
PROJ_TN = 1024
PROJ_CHUNK = 256
K_COL = (FOURIER_WIDTH + NA_WIDTH) // PROJ_TN
V_COL = (FOURIER_WIDTH + 2 * NA_WIDTH) // PROJ_TN


PROJ_STEPS = (IN_WIDTH // PROJ_TN) * (T_ALL // PROJ_TM)
CAST_SHAPES = ((N_EXPERTS * D_MODEL, EXPERT_HIDDEN, 512),
               (N_EXPERTS * D_MODEL, EXPERT_HIDDEN, 512),
               (N_EXPERTS * EXPERT_HIDDEN, D_MODEL, 128),
               (FOURIER_WIDTH, D_MODEL, 16),
               (NA_WIDTH, D_MODEL, 16),
               (D_MODEL, D_MODEL, 32))
N_CAST = len(CAST_SHAPES)


KV_ROWS = PROJ_TM * NA_HEADS


def _kv_copy(acc_ref, dst_hbm, tile, sem):
    rows = pl.ds(pl.multiple_of(tile * KV_ROWS, KV_ROWS), KV_ROWS)
    return pltpu.make_async_copy(acc_ref.at[tile % 2], dst_hbm.at[rows, :], sem)


def _proj_kernel(h_ref, w_ref, b_ref, *rest):
    cast_in = rest[:N_CAST]
    p_ref, k_hbm, v_hbm = rest[N_CAST:N_CAST + 3]
    cast_out = rest[N_CAST + 3:2 * N_CAST + 3]
    wb_ref, acc_ref, sem = rest[-3:]
    j = pl.program_id(0)
    i = pl.program_id(1)
    n_ctx = T_CTX // PROJ_TM
    is_k = (j == K_COL) & (i < n_ctx)
    is_v = (j == V_COL) & (i < n_ctx)

    @pl.when(i == 0)
    def _():
        wb_ref[...] = w_ref[...].astype(bf16)

    n_chunks = PROJ_TN // PROJ_CHUNK
    for c in range(n_chunks):
        cols = slice(c * PROJ_CHUNK, (c + 1) * PROJ_CHUNK)
        acc = jnp.dot(h_ref[...], wb_ref[:, cols], preferred_element_type=f32) + b_ref[:, cols]
        p_ref[:, cols] = acc.astype(bf16)
        for hh in range(PROJ_CHUNK // HEAD_DIM):
            head = c * (PROJ_CHUNK // HEAD_DIM) + hh
            acc_ref[i % 2, pl.ds(head, PROJ_TM, stride=NA_HEADS), :] = acc[:, hh * HEAD_DIM:(hh + 1) * HEAD_DIM]
        for src, dst in list(zip(cast_in, cast_out))[c::n_chunks]:
            dst[...] = src[...].astype(bf16)

    @pl.when(((j == K_COL) | (j == V_COL)) & (i >= 1) & (i <= n_ctx))
    def _():
        _kv_copy(acc_ref, k_hbm, i - 1, sem).wait()

    @pl.when(is_k)
    def _():
        _kv_copy(acc_ref, k_hbm, i, sem).start()

    @pl.when(is_v)
    def _():
        _kv_copy(acc_ref, v_hbm, i, sem).start()


def _cast_spec(rows, cols, rb):
    m = T_ALL // PROJ_TM
    return pl.BlockSpec((rb, cols), lambda j, i: (jnp.minimum(j * m + i, rows // rb - 1), 0))


def _project(h1, w_in, b_in, cast_weights):
    assert all(rows // rb <= PROJ_STEPS for rows, _, rb in CAST_SHAPES)
    assert T_CTX // PROJ_TM < T_ALL // PROJ_TM
    cast_specs = [_cast_spec(*s) for s in CAST_SHAPES]
    hbm = pl.BlockSpec(memory_space=pl.ANY)
    outs = pl.pallas_call(
        _proj_kernel,
        grid=(IN_WIDTH // PROJ_TN, T_ALL // PROJ_TM),
        in_specs=[pl.BlockSpec((PROJ_TM, D_MODEL), lambda j, i: (i, 0)),
                  pl.BlockSpec((D_MODEL, PROJ_TN), lambda j, i: (0, j)),
                  pl.BlockSpec((1, PROJ_TN), lambda j, i: (0, j))] + cast_specs,
        out_specs=[pl.BlockSpec((PROJ_TM, PROJ_TN), lambda j, i: (i, j)), hbm, hbm] + cast_specs,
        out_shape=[jax.ShapeDtypeStruct((T_ALL, IN_WIDTH), bf16),
                   jax.ShapeDtypeStruct((T_CTX * NA_HEADS, HEAD_DIM), f32),
                   jax.ShapeDtypeStruct((T_CTX * NA_HEADS, HEAD_DIM), f32)]
                  + [jax.ShapeDtypeStruct((rows, cols), bf16) for rows, cols, _ in CAST_SHAPES],
        scratch_shapes=[pltpu.VMEM((D_MODEL, PROJ_TN), bf16), pltpu.VMEM((2, KV_ROWS, HEAD_DIM), f32),
                        pltpu.SemaphoreType.DMA(())],
        compiler_params=_params(2),
        name="project",
    )(h1, w_in, b_in, *cast_weights)
    return outs[0], outs[1], outs[2], outs[3:]


Q_GROUP_ROWS = 4
N_Q_GROUPS = GRID_ROWS // Q_GROUP_ROWS
Q_GROUP = Q_GROUP_ROWS * GRID_W
KEY_SPAN_ROWS = 12
KEY_SPAN = KEY_SPAN_ROWS * GRID_W


def _window_start(r):
    return min(max(r - WIN_ROWS // 2, 0), GRID_ROWS - WIN_ROWS)


def _key_base(g):
    lo = _window_start(g * Q_GROUP_ROWS)
    hi = _window_start((g + 1) * Q_GROUP_ROWS - 1) + WIN_ROWS
    assert hi - lo <= KEY_SPAN_ROWS
    return min(lo, GRID_ROWS - KEY_SPAN_ROWS)


def _visible_span(g):
    base = _key_base(g)
    first = _window_start(g * Q_GROUP_ROWS) - base
    last = _window_start((g + 1) * Q_GROUP_ROWS - 1) + WIN_ROWS - base
    lo = (first // 2) * 2 * GRID_W
    hi = -(-last // 2) * 2 * GRID_W
    return lo, hi


def _block_shift(g, rr):
    return _key_base(g) - (g * Q_GROUP_ROWS + rr) + WIN_ROWS - 1


TABLE_PAD = -min(_block_shift(g, rr) for g in range(N_Q_GROUPS) for rr in range(Q_GROUP_ROWS))
TABLE_BLOCKS = 2 * (-(-(max(_block_shift(g, rr) for g in range(N_Q_GROUPS) for rr in range(Q_GROUP_ROWS))
                        + TABLE_PAD + KEY_SPAN_ROWS + 1) // 2))
TABLE_LANES = TABLE_BLOCKS * GRID_W


def _bias_kernel(r0_ref, r1_ref, o_ref):
    qc = lax.broadcasted_iota(jnp.int32, (GRID_W, TABLE_LANES), 0)
    lane = lax.broadcasted_iota(jnp.int32, (GRID_W, TABLE_LANES), 1)
    kc = lane & (GRID_W - 1)
    dc = kc - qc + (WIN_COLS - 1)
    start = jnp.clip(qc - WIN_COLS // 2, 0, GRID_W - WIN_COLS)
    ok = (kc >= start) & (kc < start + WIN_COLS)
    tables = []
    for r_ref in (r0_ref, r1_ref):
        acc = jnp.zeros((GRID_W, TABLE_LANES), f32)
        for b in range(2 * WIN_COLS - 1):
            acc = jnp.where(dc == b, r_ref[0, b:b + 1, :], acc)
        tables.append(jnp.where(ok, acc, NEG_INF))
    span_row = jnp.right_shift(lax.broadcasted_iota(jnp.int32, (GRID_W, KEY_SPAN), 1), GRID_W.bit_length() - 1)
    for g in range(N_Q_GROUPS):
        for rr in range(Q_GROUP_ROWS):
            first = _window_start(g * Q_GROUP_ROWS + rr) - _key_base(g)
            e = _block_shift(g, rr) + TABLE_PAD
            lo = (e - e % 2) * GRID_W
            in_window = (span_row >= first) & (span_row < first + WIN_ROWS)
            o_ref[0, g, rr * GRID_W:(rr + 1) * GRID_W, :] = jnp.where(
                in_window, tables[e % 2][:, lo:lo + KEY_SPAN], NEG_INF)


def _bias_table(rpb):
    nb = 2 * WIN_COLS - 1
    n_off = 2 * WIN_ROWS - 1
    by_row = jnp.transpose(rpb, (0, 2, 1))
    rpbx = []
    for p in range(2):
        spread = np.zeros((n_off, TABLE_LANES), np.float32)
        for t in range(TABLE_BLOCKS):
            if 0 <= t + p - TABLE_PAD < n_off:
                spread[t + p - TABLE_PAD, t * GRID_W:(t + 1) * GRID_W] = 1.0
        rpbx.append(jnp.einsum("hba,al->hbl", by_row, jnp.asarray(spread), precision=lax.Precision.HIGHEST))
    table_spec = pl.BlockSpec((1, nb, TABLE_LANES), lambda h: (h, 0, 0))
    return pl.pallas_call(
        _bias_kernel,
        grid=(NA_HEADS,),
        in_specs=[table_spec, table_spec],
        out_specs=pl.BlockSpec((1, N_Q_GROUPS, Q_GROUP, KEY_SPAN), lambda h: (h, 0, 0, 0)),
        out_shape=jax.ShapeDtypeStruct((NA_HEADS, N_Q_GROUPS, Q_GROUP, KEY_SPAN), f32),
        compiler_params=_params(1),
        name="bias_table",
    )(*rpbx)


def _qk(q, k):
    return lax.dot_general(q, k, (((1,), (1,)), ((), ())), preferred_element_type=f32)


def _ctx_mix_kernel(q_ref, k_ref, v_ref, u_ref, chan_ref, pos_ref, na_ref, fm_ref):
    scale = HEAD_DIM ** -0.5
    gd = FOURIER_GROUP_DIM

    def head(h):
        sl = slice(h * HEAD_DIM, (h + 1) * HEAD_DIM)
        s = _qk(q_ref[:, sl], k_ref[:, sl]) * scale
        p = jnp.exp(s - jnp.max(s, axis=-1, keepdims=True))
        l = jnp.sum(p, axis=-1, keepdims=True)
        o = jnp.dot(p.astype(bf16), v_ref[:, sl], preferred_element_type=f32) / l
        na_ref[:, sl] = o.astype(bf16)

    ys = []
    for g in range(FOURIER_GROUPS):
        head(g)
        ys.append(jnp.dot(u_ref[:, g * gd:(g + 1) * gd], chan_ref[...], preferred_element_type=f32))
    yc = jnp.concatenate([y[:, :gd] for y in ys], axis=1)
    ysn = jnp.concatenate([y[:, gd:] for y in ys], axis=1)
    stacked = jnp.concatenate([yc, ysn], axis=0).astype(bf16)
    for h in range(FOURIER_GROUPS, NA_HEADS):
        head(h)
        if h == FOURIER_GROUPS:
            fm_ref[...] = jnp.dot(pos_ref[...], stacked, preferred_element_type=f32).astype(bf16)


def _ctx_mix(p_all):
    qcol = FOURIER_WIDTH // NA_WIDTH
    chan, pos = _dft_consts(SEQ)
    return pl.pallas_call(
        _ctx_mix_kernel,
        grid=(BATCH,),
        in_specs=[pl.BlockSpec((SEQ, NA_WIDTH), lambda b: (b, qcol)),
                  pl.BlockSpec((SEQ, NA_WIDTH), lambda b: (b, qcol + 1)),
                  pl.BlockSpec((SEQ, NA_WIDTH), lambda b: (b, qcol + 2)),
                  pl.BlockSpec((SEQ, FOURIER_WIDTH), lambda b: (b, 0)),
                  pl.BlockSpec((FOURIER_GROUP_DIM, 2 * FOURIER_GROUP_DIM), lambda b: (0, 0)),
                  pl.BlockSpec((SEQ, 2 * SEQ), lambda b: (0, 0))],
        out_specs=[pl.BlockSpec((SEQ, NA_WIDTH), lambda b: (b, 0)),
                   pl.BlockSpec((SEQ, FOURIER_WIDTH), lambda b: (b, 0))],
        out_shape=[jax.ShapeDtypeStruct((T_CTX, NA_WIDTH), bf16),
                   jax.ShapeDtypeStruct((T_CTX, FOURIER_WIDTH), bf16)],
        compiler_params=_params(1),
        name="ctx_mix",
    )(p_all, p_all, p_all, p_all, chan, pos)


def _attn_lat_kernel(q_ref, k_ref, v_ref, kc_ref, vc_ref, bias_ref, o_ref):
    scale = HEAD_DIM ** -0.5
    head_rows = pl.ds(pl.program_id(0), PAST_LEN, stride=NA_HEADS)
    kcb = kc_ref[0, head_rows, :].astype(bf16)
    vcb = vc_ref[0, head_rows, :].astype(bf16)
    for g in range(N_Q_GROUPS):
        lo, hi = _visible_span(g)
        k0 = _key_base(g) * GRID_W + lo
        nk = hi - lo
        q = q_ref[g * Q_GROUP:(g + 1) * Q_GROUP, :]
        s_nb = _qk(q, k_ref[k0:k0 + nk, :]) * scale + bias_ref[0, g, :, lo:hi]
        s_cx = _qk(q, kcb) * scale
        m = jnp.maximum(jnp.max(s_nb, axis=-1, keepdims=True), jnp.max(s_cx, axis=-1, keepdims=True))
        p_nb = jnp.exp(s_nb - m)
        p_cx = jnp.exp(s_cx - m)
        l = jnp.sum(p_nb, axis=-1, keepdims=True) + jnp.sum(p_cx, axis=-1, keepdims=True)
        o = (jnp.dot(p_nb.astype(bf16), v_ref[k0:k0 + nk, :], preferred_element_type=f32)
             + jnp.dot(p_cx.astype(bf16), vcb, preferred_element_type=f32)) / l
        o_ref[g * Q_GROUP:(g + 1) * Q_GROUP, :] = o.astype(bf16)


def _attn_lat(p_all, ck, cv, bias):
    row0 = T_CTX // DEC_SEQ
    qcol = FOURIER_WIDTH // HEAD_DIM
    return pl.pallas_call(
        _attn_lat_kernel,
        grid=(NA_HEADS, DEC_BATCH),
        in_specs=[pl.BlockSpec((DEC_SEQ, HEAD_DIM), lambda h, b: (row0 + b, qcol + h)),
                  pl.BlockSpec((DEC_SEQ, HEAD_DIM), lambda h, b: (row0 + b, qcol + NA_HEADS + h)),
                  pl.BlockSpec((DEC_SEQ, HEAD_DIM), lambda h, b: (row0 + b, qcol + 2 * NA_HEADS + h)),
                  pl.BlockSpec((1, PAST_LEN * NA_HEADS, HEAD_DIM), lambda h, b: (b, 0, 0)),
                  pl.BlockSpec((1, PAST_LEN * NA_HEADS, HEAD_DIM), lambda h, b: (b, 0, 0)),
                  pl.BlockSpec((1, N_Q_GROUPS, Q_GROUP, KEY_SPAN), lambda h, b: (h, 0, 0, 0))],
        out_specs=pl.BlockSpec((DEC_SEQ, HEAD_DIM), lambda h, b: (b, h)),
        out_shape=jax.ShapeDtypeStruct((T_LAT, NA_WIDTH), bf16),
        compiler_params=_params(2),
        name="attn_lat",
    )(p_all, p_all, p_all, ck, cv, bias)


def _dft_consts(n):
    def cs(m):
        idx = (np.arange(m)[:, None] * np.arange(m)[None, :]) % m
        ang = 2.0 * np.pi * idx.astype(np.float64) / m
        return np.cos(ang), np.sin(ang)

    cc, sc = cs(FOURIER_GROUP_DIM)
    cn, sn = cs(n)
    chan = np.concatenate([cc, sc], axis=1) / np.sqrt(FOURIER_GROUP_DIM)
    pos = np.concatenate([cn, -sn], axis=1) / np.sqrt(n)
    return jnp.asarray(chan, f32).astype(bf16), jnp.asarray(pos, f32).astype(bf16)


def _fourier_kernel(u_ref, chan_ref, pos_ref, o_ref):
    gd = FOURIER_GROUP_DIM
    ys = [jnp.dot(u_ref[:, g * gd:(g + 1) * gd], chan_ref[...], preferred_element_type=f32)
          for g in range(FOURIER_GROUPS)]
    yc = jnp.concatenate([y[:, :gd] for y in ys], axis=1)
    ysn = jnp.concatenate([y[:, gd:] for y in ys], axis=1)
    stacked = jnp.concatenate([yc, ysn], axis=0).astype(bf16)
    o_ref[...] = jnp.dot(pos_ref[...], stacked, preferred_element_type=f32).astype(bf16)


def _fourier(p_all, n, n_req, row0):
    chan, pos = _dft_consts(n)
    return pl.pallas_call(
        _fourier_kernel,
        grid=(n_req,),
        in_specs=[pl.BlockSpec((n, FOURIER_WIDTH), lambda b: (row0 + b, 0)),
                  pl.BlockSpec((FOURIER_GROUP_DIM, 2 * FOURIER_GROUP_DIM), lambda b: (0, 0)),
                  pl.BlockSpec((n, 2 * n), lambda b: (0, 0))],
        out_specs=pl.BlockSpec((n, FOURIER_WIDTH), lambda b: (b, 0)),
        out_shape=jax.ShapeDtypeStruct((n_req * n, FOURIER_WIDTH), bf16),
        compiler_params=_params(1),
        name=f"fourier_{n}",
    )(p_all, chan, pos)


MERGE_TM = 256
ROUTE_LANES = LANES


def _class_of(logits):
    lane = lax.broadcasted_iota(jnp.int32, logits.shape, 1)
    big = jnp.int32(ROUTE_LANES)
    is_group = lane < N_GROUPS
    mg = jnp.max(jnp.where(is_group, logits, -jnp.inf), axis=-1, keepdims=True)
    gsel = jnp.min(jnp.where(is_group & (logits == mg), lane, big), axis=-1, keepdims=True)
    lo = N_GROUPS + EXPERTS_PER_GROUP * gsel
    in_group = (lane >= lo) & (lane < lo + EXPERTS_PER_GROUP)
    m1 = jnp.max(jnp.where(in_group, logits, -jnp.inf), axis=-1, keepdims=True)
    i1 = jnp.min(jnp.where(in_group & (logits == m1), lane, big), axis=-1, keepdims=True)
    rest = in_group & (lane != i1)
    m2 = jnp.max(jnp.where(rest, logits, -jnp.inf), axis=-1, keepdims=True)
    i2 = jnp.min(jnp.where(rest & (logits == m2), lane, big), axis=-1, keepdims=True)
    e_lo = jnp.minimum(i1, i2) - lo
    e_hi = jnp.maximum(i1, i2) - lo
    pair = jnp.zeros_like(e_lo)
    for idx in range(N_PAIRS):
        a, b = sorted((PAIR_A[idx], PAIR_B[idx]))
        pair = jnp.where((e_lo == a) & (e_hi == b), idx, pair)
    return gsel * N_PAIRS + pair


def _merge_kernel(fmc_ref, fml_ref, nac_ref, nal_ref, ga_ref, gb_ref, xp_ref, xs_ref, mods_ref, g2_ref,
                  wf_ref, wn_ref, wo_ref, wr_ref, br_ref,
                  x1_ref, h2c_ref, info_ref, cnt_ref, carry_ref, x1s_ref):
    s = pl.program_id(0)
    n_ctx = T_CTX // MERGE_TM
    n_all = T_ALL // MERGE_TM
    per_req = DEC_SEQ // MERGE_TM
    row = _mod_row(jnp.minimum(s, n_all - 1), n_ctx, per_req)
    prev_row = _mod_row(jnp.maximum(s - 1, 0), n_ctx, per_req)

    @pl.when(s == 0)
    def _():
        carry_ref[...] = jnp.zeros_like(carry_ref)
        x1s_ref[1] = jnp.zeros((MERGE_TM, D_MODEL), f32)

    def step(tile_refs):
        x1p = x1s_ref[(s + 1) % 2]
        if tile_refs:
            fm_ref, na_ref, x_ref = tile_refs
            a = jnp.dot(fm_ref[...], wf_ref[...], preferred_element_type=f32)

        h2 = _rms(x1p) * g2_ref[...] * (1.0 + _mod_vec(mods_ref, prev_row, 4)) + _mod_vec(mods_ref, prev_row, 3)
        if tile_refs:
            gate_a = jax.nn.sigmoid(ga_ref[...].astype(f32))
            b = jnp.dot(na_ref[...], wn_ref[...], preferred_element_type=f32)

        for k in range(CHUNKS):
            h2c_ref[pl.ds(k, MERGE_TM, stride=CHUNKS), :] = h2[:, k * LANES:(k + 1) * LANES]
        logits = jnp.dot(h2.astype(bf16), wr_ref[...], preferred_element_type=f32) + br_ref[...]
        if tile_refs:
            merged = gate_a * a + jax.nn.sigmoid(gb_ref[...].astype(f32)) * b
            out = jnp.dot(merged.astype(bf16), wo_ref[...], preferred_element_type=f32)

        cls = _class_of(logits)
        lane = lax.broadcasted_iota(jnp.int32, logits.shape, 1)
        onehot = ((lane == cls) & (s > 0)).astype(f32)
        r = lax.broadcasted_iota(jnp.int32, (MERGE_TM, MERGE_TM), 0)
        c = lax.broadcasted_iota(jnp.int32, (MERGE_TM, MERGE_TM), 1)
        before = (c < r).astype(bf16)
        prefix = jnp.dot(before, onehot.astype(bf16), preferred_element_type=f32) + carry_ref[...]
        rank = jnp.sum(onehot * prefix, axis=-1, keepdims=True)
        info_ref[...] = jnp.where(lane == 0, cls.astype(f32), jnp.where(lane == 1, rank, 0.0))
        carry_ref[...] = carry_ref[...] + jnp.sum(onehot, axis=0, keepdims=True)
        cnt_ref[...] = jnp.broadcast_to(carry_ref[...], cnt_ref.shape)
        if tile_refs:
            x1 = x_ref[...] + _mod_vec(mods_ref, row, 2) * out
            x1_ref[...] = x1
            x1s_ref[s % 2] = x1

    @pl.when(s < n_ctx)
    def _():
        step((fmc_ref, nac_ref, xp_ref))

    @pl.when((s >= n_ctx) & (s < n_all))
    def _():
        step((fml_ref, nal_ref, xs_ref))

    @pl.when(s == n_all)
    def _():
        step(None)


def _merge(fm_c, fm_l, na_c, na_l, p_all, xp, xs, mods, g2, wf, wn, wo, wr, br):
    n_ctx = T_CTX // MERGE_TM
    n_lat = T_LAT // MERGE_TM
    n_all = n_ctx + n_lat
    gcol = (FOURIER_WIDTH + 3 * NA_WIDTH) // D_MODEL
    const = lambda i: (0, 0)
    ctx_tile = lambda i: (jnp.minimum(i, n_ctx - 1), 0)
    lat_tile = lambda i: (jnp.clip(i - n_ctx, 0, n_lat - 1), 0)
    this_tile = lambda i: jnp.minimum(i, n_all - 1)
    prev_tile = lambda i: (jnp.maximum(i - 1, 0), 0)
    once = pl.Buffered(1)
    return pl.pallas_call(
        _merge_kernel,
        grid=(n_all + 1,),
        in_specs=[pl.BlockSpec((MERGE_TM, FOURIER_WIDTH), ctx_tile),
                  pl.BlockSpec((MERGE_TM, FOURIER_WIDTH), lat_tile),
                  pl.BlockSpec((MERGE_TM, NA_WIDTH), ctx_tile),
                  pl.BlockSpec((MERGE_TM, NA_WIDTH), lat_tile),
                  pl.BlockSpec((MERGE_TM, D_MODEL), lambda i: (this_tile(i), gcol)),
                  pl.BlockSpec((MERGE_TM, D_MODEL), lambda i: (this_tile(i), gcol + 1)),
                  pl.BlockSpec((MERGE_TM, D_MODEL), ctx_tile),
                  pl.BlockSpec((MERGE_TM, D_MODEL), lat_tile),
                  pl.BlockSpec((MOD_ROWS, N_MOD * D_MODEL), const, pipeline_mode=once),
                  pl.BlockSpec((1, D_MODEL), const, pipeline_mode=once),
                  pl.BlockSpec((FOURIER_WIDTH, D_MODEL), const, pipeline_mode=once),
                  pl.BlockSpec((NA_WIDTH, D_MODEL), const, pipeline_mode=once),
                  pl.BlockSpec((D_MODEL, D_MODEL), const, pipeline_mode=once),
                  pl.BlockSpec((D_MODEL, ROUTE_LANES), const, pipeline_mode=once),
                  pl.BlockSpec((1, ROUTE_LANES), const, pipeline_mode=once)],
        out_specs=[pl.BlockSpec((MERGE_TM, D_MODEL), lambda i: (this_tile(i), 0)),
                   pl.BlockSpec((MERGE_TM * CHUNKS, LANES), prev_tile),
                   pl.BlockSpec((MERGE_TM, ROUTE_LANES), prev_tile),
                   pl.BlockSpec((8, ROUTE_LANES), const)],
        out_shape=[jax.ShapeDtypeStruct((T_ALL, D_MODEL), f32),
                   jax.ShapeDtypeStruct((T_ALL * CHUNKS, LANES), f32),
                   jax.ShapeDtypeStruct((T_ALL, ROUTE_LANES), f32),
                   jax.ShapeDtypeStruct((8, ROUTE_LANES), f32)],
        scratch_shapes=[pltpu.VMEM((1, ROUTE_LANES), f32), pltpu.VMEM((2, MERGE_TM, D_MODEL), f32)],
        compiler_params=_params(1),
        name="merge",
    )(fm_c, fm_l, na_c, na_l, p_all, p_all, xp, xs, mods, g2, wf, wn, wo, wr, br)


def _perm_kernel(pos_ref, zeros_hbm, src_ref):
    pltpu.sync_copy(zeros_hbm, src_ref)

    def place(t, carry):
        src_ref[pos_ref[t]] = t
        return carry

    lax.fori_loop(0, T_ALL, place, 0, unroll=16)


def _inverse_positions(pos):
    smem = pl.BlockSpec(memory_space=pltpu.SMEM)
    return pl.pallas_call(
        _perm_kernel,
        in_specs=[smem, pl.BlockSpec(memory_space=pl.ANY)],
        out_specs=smem,
        out_shape=jax.ShapeDtypeStruct((T_PAD,), jnp.int32),
        name="permutation",
    )(pos, jnp.zeros((T_PAD,), jnp.int32))


def _row_copy(idx_ref, base, src_hbm, buf, sem, slot, k):
    tok = idx_ref[base + k]
    return pltpu.make_async_copy(src_hbm.at[pl.ds(pl.multiple_of(tok * CHUNKS, CHUNKS), CHUNKS), :],
                                 buf.at[slot, pl.ds(pl.multiple_of(k * CHUNKS, CHUNKS), CHUNKS), :],
                                 sem.at[slot])


def _start_gather(idx_ref, base, src_hbm, buf, sem, slot, rows, queues=(0, 1)):
    def body(k8, carry):
        for j in range(8):
            _row_copy(idx_ref, base, src_hbm, buf, sem, slot, k8 * 8 + j).start(priority=queues[j % len(queues)])
        return carry

    lax.fori_loop(0, rows // 8, body, 0)


def _wait_gather(buf, sem, slot):
    pltpu.make_async_copy(buf.at[slot], buf.at[slot], sem.at[slot]).wait()


def _gathered_rows(buf, slot, rows):
    return jnp.concatenate([buf[slot, pl.ds(s, rows, stride=CHUNKS), :] for s in range(CHUNKS)], axis=1)


def _moe_kernel(ea_ref, eb_ref, nt_ref, src_ref, h2c_hbm, wga_ref, wua_ref, wda_ref, wgb_ref, wub_ref, wdb_ref,
                wr_ref, br_ref, o_ref, buf, sem):
    i = pl.program_id(0)
    nt = nt_ref[0]

    @pl.when(i >= nt)
    def _():
        o_ref[...] = jnp.zeros_like(o_ref)

    def fetch(tile):
        _start_gather(src_ref, tile * MOE_TILE, h2c_hbm, buf, sem, tile % MOE_SLOTS, MOE_TILE, queues=(1,))

    @pl.when(i == 0)
    def _():
        for t in range(MOE_AHEAD):
            @pl.when(t < nt)
            def _(t=t):
                fetch(t)

    @pl.when(i + MOE_AHEAD < nt)
    def _():
        fetch(i + MOE_AHEAD)

    @pl.when(i < nt)
    def _():
        slot = i % MOE_SLOTS
        _wait_gather(buf, sem, slot)
        x = _gathered_rows(buf, slot, MOE_TILE).astype(bf16)
        logits = jnp.dot(x, wr_ref[...], preferred_element_type=f32) + br_ref[...]
        lane = lax.broadcasted_iota(jnp.int32, logits.shape, 1)
        ea = ea_ref[i]
        eb = eb_ref[i]

        def pick(col):
            return jnp.sum(jnp.where(lane == col, logits, 0.0), axis=-1, keepdims=True)

        lg = pick(ea // EXPERTS_PER_GROUP)
        p_group = 1.0 / jnp.sum(jnp.where(lane < N_GROUPS, jnp.exp(logits - lg), 0.0), axis=-1, keepdims=True)
        la = pick(N_GROUPS + ea)
        lb = pick(N_GROUPS + eb)
        w_a = p_group / (1.0 + jnp.exp(lb - la))
        w_b = p_group / (1.0 + jnp.exp(la - lb))

        def up(wg_ref, wu_ref):
            return (jnp.dot(x, wg_ref[0], preferred_element_type=f32),
                    jnp.dot(x, wu_ref[0], preferred_element_type=f32))

        def act(g, u):
            return (g * jax.nn.sigmoid(g) * u).astype(bf16)

        g_a, u_a = up(wga_ref, wua_ref)
        g_b, u_b = up(wgb_ref, wub_ref)
        hid_a = act(g_a, u_a)
        hid_b = act(g_b, u_b)
        per_chunk = MOE_OUT_CHUNK // LANES
        for c in range(D_MODEL // MOE_OUT_CHUNK):
            cols = slice(c * MOE_OUT_CHUNK, (c + 1) * MOE_OUT_CHUNK)
            y = (w_a * jnp.dot(hid_a, wda_ref[0, :, cols], preferred_element_type=f32)
                 + w_b * jnp.dot(hid_b, wdb_ref[0, :, cols], preferred_element_type=f32))
            for k in range(per_chunk):
                o_ref[pl.ds(c * per_chunk + k, MOE_TILE, stride=CHUNKS), :] = y[:, k * LANES:(k + 1) * LANES]


def _moe(tile_ea, tile_eb, ntiles, src, h2c, wg, wu, wd, wr, br):
    up = lambda tab: (lambda i, ea, eb, nt, src: (tab(ea, eb)[i], 0, 0))
    pick_a = lambda ea, eb: ea
    pick_b = lambda ea, eb: eb
    const = lambda i, ea, eb, nt, src: (0, 0)
    grid_spec = pltpu.PrefetchScalarGridSpec(
        num_scalar_prefetch=4,
        grid=(MOE_MAX_TILES,),
        in_specs=[pl.BlockSpec(memory_space=pl.ANY),
                  pl.BlockSpec((1, D_MODEL, EXPERT_HIDDEN), up(pick_a)),
                  pl.BlockSpec((1, D_MODEL, EXPERT_HIDDEN), up(pick_a)),
                  pl.BlockSpec((1, EXPERT_HIDDEN, D_MODEL), up(pick_a)),
                  pl.BlockSpec((1, D_MODEL, EXPERT_HIDDEN), up(pick_b)),
                  pl.BlockSpec((1, D_MODEL, EXPERT_HIDDEN), up(pick_b)),
                  pl.BlockSpec((1, EXPERT_HIDDEN, D_MODEL), up(pick_b)),
                  pl.BlockSpec((D_MODEL, ROUTE_LANES), const),
                  pl.BlockSpec((1, ROUTE_LANES), const)],
        out_specs=pl.BlockSpec((MOE_TILE * CHUNKS, LANES), lambda i, ea, eb, nt, src: (i, 0)),
        scratch_shapes=[pltpu.VMEM((MOE_SLOTS, MOE_TILE * CHUNKS, LANES), f32),
                        pltpu.SemaphoreType.DMA((MOE_SLOTS,))],
    )
    return pl.pallas_call(
        _moe_kernel,
        grid_spec=grid_spec,
        out_shape=jax.ShapeDtypeStruct((T_PAD * CHUNKS, LANES), f32),
        compiler_params=_params(1),
        name="moe",
    )(tile_ea, tile_eb, ntiles, src, h2c, wg, wu, wd, wg, wu, wd, wr, br)


def _final_kernel(pos_ref, ys_hbm, x1_ref, mods_ref, g_ref, o_ref, buf, sem, *, tile0, n_tiles, n_ctx_tiles):
    i = pl.program_id(0)

    def fetch(tile):
        _start_gather(pos_ref, (tile0 + tile) * FINAL_TILE, ys_hbm, buf, sem, tile % FINAL_SLOTS, FINAL_TILE)

    @pl.when(i == 0)
    def _():
        for t in range(min(FINAL_AHEAD, n_tiles)):
            fetch(t)

    @pl.when(i + FINAL_AHEAD < n_tiles)
    def _():
        fetch(i + FINAL_AHEAD)

    slot = i % FINAL_SLOTS
    _wait_gather(buf, sem, slot)
    y = _gathered_rows(buf, slot, FINAL_TILE)
    row = _mod_row(tile0 + i, n_ctx_tiles, DEC_SEQ // FINAL_TILE)
    x2 = x1_ref[...] + _mod_vec(mods_ref, row, 5) * y
    o_ref[...] = _rms(x2) * g_ref[...]


def _final(pos, ys, x1, mods, g, tile0, n_tiles):
    body = functools.partial(_final_kernel, tile0=tile0, n_tiles=n_tiles, n_ctx_tiles=T_CTX // FINAL_TILE)
    grid_spec = pltpu.PrefetchScalarGridSpec(
        num_scalar_prefetch=1,
        grid=(n_tiles,),
        in_specs=[pl.BlockSpec(memory_space=pl.ANY),
                  pl.BlockSpec((FINAL_TILE, D_MODEL), lambda i, pos: (tile0 + i, 0)),
                  pl.BlockSpec((MOD_ROWS, N_MOD * D_MODEL), lambda i, pos: (0, 0)),
                  pl.BlockSpec((1, D_MODEL), lambda i, pos: (0, 0))],
        out_specs=pl.BlockSpec((FINAL_TILE, D_MODEL), lambda i, pos: (i, 0)),
        scratch_shapes=[pltpu.VMEM((FINAL_SLOTS, FINAL_TILE * CHUNKS, LANES), f32),
                        pltpu.SemaphoreType.DMA((FINAL_SLOTS,))],
    )
    return pl.pallas_call(
        body,
        grid_spec=grid_spec,
        out_shape=jax.ShapeDtypeStruct((n_tiles * FINAL_TILE, D_MODEL), f32),
        compiler_params=_params(1),
        name=f"final_{tile0}",
    )(pos, ys, x1, mods, g)


def kernel(x_prompt, x_sample, cache_k_ctx, cache_v_ctx, c, c_ctx, norm1_g, norm2_g, w_ada, b_ada, w_in, b_in,
           w_fourier, w_na_o, rpb, w_out, w_router_group, b_router_group, w_router_expert, b_router_expert,
           w_exp_gate, w_exp_up, w_exp_down, final_norm_g):
    xp = x_prompt.reshape(T_CTX, D_MODEL)
    xs = x_sample.reshape(T_LAT, D_MODEL)
    c_all = jnp.concatenate([c_ctx[None, :], c, jnp.zeros((MOD_ROWS - 1 - DEC_BATCH, D_MODEL), f32)], axis=0)
    w_route = jnp.concatenate([w_router_group[0], w_router_expert[0]], axis=1)
    w_route = jnp.pad(w_route, ((0, 0), (0, ROUTE_LANES - w_route.shape[1]))).astype(bf16)
    b_route = jnp.concatenate([b_router_group[0], b_router_expert[0]])
    b_route = jnp.pad(b_route, (0, ROUTE_LANES - b_route.shape[0]))[None, :]

    mods = _ada(c_all, w_ada[0], b_ada[0][None, :])
    h1 = _modulate1(xp, xs, mods, norm1_g)
    p_all, k_new, v_new, (wg, wu, wd, wf, wn, wo) = _project(
        h1, w_in[0], b_in,
        (w_exp_gate.reshape(N_EXPERTS * D_MODEL, EXPERT_HIDDEN), w_exp_up.reshape(N_EXPERTS * D_MODEL, EXPERT_HIDDEN),
         w_exp_down.reshape(N_EXPERTS * EXPERT_HIDDEN, D_MODEL), w_fourier[0], w_na_o[0], w_out[0]))
    wg = wg.reshape(N_EXPERTS, D_MODEL, EXPERT_HIDDEN)
    wu = wu.reshape(N_EXPERTS, D_MODEL, EXPERT_HIDDEN)
    wd = wd.reshape(N_EXPERTS, EXPERT_HIDDEN, D_MODEL)

    tt = _bias_table(rpb[0])
    na_c, fm_c = _ctx_mix(p_all)
    ck = cache_k_ctx.reshape(DEC_BATCH, PAST_LEN * NA_HEADS, HEAD_DIM)
    cv = cache_v_ctx.reshape(DEC_BATCH, PAST_LEN * NA_HEADS, HEAD_DIM)
    na_l = _attn_lat(p_all, ck, cv, tt)
    fm_l = _fourier(p_all, DEC_SEQ, DEC_BATCH, T_CTX // DEC_SEQ)

    x1, h2c, info, counts = _merge(fm_c, fm_l, na_c, na_l, p_all, xp, xs, mods, norm2_g, wf, wn, wo,
                                   w_route, b_route)

    cnt = counts[0, :N_CLASSES].astype(jnp.int32)
    padded = ((cnt + MOE_TILE - 1) // MOE_TILE) * MOE_TILE
    off_end = jnp.cumsum(padded)
    off = off_end - padded
    ntiles = (off_end[-1] // MOE_TILE).astype(jnp.int32)
    tile_start = jnp.arange(MOE_MAX_TILES, dtype=jnp.int32) * MOE_TILE
    last_start = (ntiles - 1) * MOE_TILE
    tile_cls = jnp.sum(jnp.minimum(tile_start, last_start)[:, None] >= off_end[None, :], axis=1)
    pair = tile_cls % N_PAIRS
    group = tile_cls // N_PAIRS
    pair_hot = pair[:, None] == jnp.arange(N_PAIRS)[None, :]
    tile_ea = (group * EXPERTS_PER_GROUP + jnp.sum(pair_hot * jnp.asarray(PAIR_A), axis=1)).astype(jnp.int32)
    tile_eb = (group * EXPERTS_PER_GROUP + jnp.sum(pair_hot * jnp.asarray(PAIR_B), axis=1)).astype(jnp.int32)

    class_hot = info[:, 0:1] == jnp.arange(N_CLASSES, dtype=f32)[None, :]
    pos = (info[:, 1] + jnp.sum(jnp.where(class_hot, off.astype(f32)[None, :], 0.0), axis=1)).astype(jnp.int32)
    src = _inverse_positions(pos)
    ys = _moe(tile_ea, tile_eb, ntiles[None], src, h2c, wg, wu, wd, w_route, b_route)

    fg = final_norm_g[None, :]
    y_prompt = _final(pos, ys, x1, mods, fg, 0, T_CTX // FINAL_TILE)
    y_sample = _final(pos, ys, x1, mods, fg, T_CTX // FINAL_TILE, T_LAT // FINAL_TILE)

    shape_kv = (BATCH, 1, SEQ, NA_HEADS, HEAD_DIM)
    return (y_prompt.reshape(BATCH, SEQ, D_MODEL), y_sample.reshape(DEC_BATCH, DEC_SEQ, D_MODEL),
            k_new.reshape(shape_kv), v_new.reshape(shape_kv))
```

```python
import functools

import jax
import jax.numpy as jnp
import numpy as np
from jax import lax
from jax.experimental import pallas as pl
from jax.experimental.pallas import tpu as pltpu

D_MODEL = 2048
BATCH = 32
SEQ = 256
DEC_BATCH = 4
DEC_SEQ = 1024
PAST_LEN = 256
GRID_W = 64
GRID_ROWS = DEC_SEQ // GRID_W
NA_HEADS = 8
HEAD_DIM = 128
NA_WIDTH = NA_HEADS * HEAD_DIM
FOURIER_GROUPS = 4
FOURIER_GROUP_DIM = 256
FOURIER_WIDTH = FOURIER_GROUPS * FOURIER_GROUP_DIM
WIN_ROWS = 8
WIN_COLS = 16
N_GROUPS = 4
EXPERTS_PER_GROUP = 4
N_EXPERTS = N_GROUPS * EXPERTS_PER_GROUP
EXPERT_HIDDEN = 512
N_MOD = 6
IN_WIDTH = FOURIER_WIDTH + 3 * NA_WIDTH + 2 * D_MODEL
EPS = 1e-6
NEG_INF = -1e30

T_CTX = BATCH * SEQ
T_LAT = DEC_BATCH * DEC_SEQ
T_ALL = T_CTX + T_LAT

LANES = 128
CHUNKS = D_MODEL // LANES
MOD_ROWS = 8

PAIR_A = (0, 0, 0, 1, 1, 3)
PAIR_B = (1, 2, 3, 3, 2, 2)
N_PAIRS = len(PAIR_A)
N_CLASSES = N_GROUPS * N_PAIRS
MOE_TILE = 256
MOE_OUT_CHUNK = 2048
MOE_AHEAD = 3
MOE_SLOTS = MOE_AHEAD + 1
MOE_MAX_TILES = (T_ALL + N_CLASSES * (MOE_TILE - 1)) // MOE_TILE
FINAL_TILE = 256
FINAL_AHEAD = 3
FINAL_SLOTS = FINAL_AHEAD + 1
T_PAD = MOE_MAX_TILES * MOE_TILE

VMEM_LIMIT = 56 * 1024 * 1024

bf16 = jnp.bfloat16
f32 = jnp.float32


def _params(n_axes, vmem=VMEM_LIMIT):
    return pltpu.CompilerParams(dimension_semantics=("arbitrary",) * n_axes, vmem_limit_bytes=vmem)


def _mod_row(tile, n_ctx_tiles, tiles_per_request):
    return jnp.where(tile < n_ctx_tiles, 0, 1 + (tile - n_ctx_tiles) // tiles_per_request)


def _mod_vec(mods_ref, row, k):
    return mods_ref[pl.ds(row, 1), k * D_MODEL:(k + 1) * D_MODEL]


def _rms(x):
    return x * lax.rsqrt(jnp.mean(x * x, axis=-1, keepdims=True) + EPS)


def _ada_kernel(c_ref, w_ref, b_ref, o_ref):
    c = c_ref[...]
    s = (c * jax.nn.sigmoid(c)).astype(bf16)
    o_ref[...] = jnp.dot(s, w_ref[...].astype(bf16), preferred_element_type=f32) + b_ref[...]


def _ada(c_all, w_ada, b_ada):
    tn = 1024
    n = N_MOD * D_MODEL
    return pl.pallas_call(
        _ada_kernel,
        grid=(n // tn,),
        in_specs=[pl.BlockSpec((MOD_ROWS, D_MODEL), lambda j: (0, 0)),
                  pl.BlockSpec((D_MODEL, tn), lambda j: (0, j)),
                  pl.BlockSpec((1, tn), lambda j: (0, j))],
        out_specs=pl.BlockSpec((MOD_ROWS, tn), lambda j: (0, j)),
        out_shape=jax.ShapeDtypeStruct((MOD_ROWS, n), f32),
        compiler_params=_params(1),
        name="ada",
    )(c_all, w_ada, b_ada)


MOD_TILE = 1024
MOD_ROWS_PER_PASS = 128


def _mod_kernel(xp_ref, xs_ref, mods_ref, g_ref, o_ref):
    i = pl.program_id(0)
    n_ctx = T_CTX // MOD_TILE
    row = _mod_row(i, n_ctx, DEC_SEQ // MOD_TILE)
    sh = _mod_vec(mods_ref, row, 0)
    sc = _mod_vec(mods_ref, row, 1)

    gain = g_ref[...] * (1.0 + sc)

    def run(x_ref):
        def body(r, carry):
            rows = pl.ds(pl.multiple_of(r * MOD_ROWS_PER_PASS, MOD_ROWS_PER_PASS), MOD_ROWS_PER_PASS)
            o_ref[rows, :] = (_rms(x_ref[rows, :]) * gain + sh).astype(bf16)
            return carry

        lax.fori_loop(0, MOD_TILE // MOD_ROWS_PER_PASS, body, 0)

    @pl.when(i < n_ctx)
    def _():
        run(xp_ref)

    @pl.when(i >= n_ctx)
    def _():
        run(xs_ref)


def _modulate1(xp, xs, mods, g):
    n_ctx = T_CTX // MOD_TILE
    return pl.pallas_call(
        _mod_kernel,
        grid=(T_ALL // MOD_TILE,),
        in_specs=[pl.BlockSpec((MOD_TILE, D_MODEL), lambda i: (jnp.minimum(i, n_ctx - 1), 0)),
                  pl.BlockSpec((MOD_TILE, D_MODEL), lambda i: (jnp.maximum(i - n_ctx, 0), 0)),
                  pl.BlockSpec((MOD_ROWS, N_MOD * D_MODEL), lambda i: (0, 0)),
                  pl.BlockSpec((1, D_MODEL), lambda i: (0, 0))],
        out_specs=pl.BlockSpec((MOD_TILE, D_MODEL), lambda i: (i, 0)),
        out_shape=jax.ShapeDtypeStruct((T_ALL, D_MODEL), bf16),
        compiler_params=_params(1),
        name="modulate1",
    )(xp, xs, mods, g)


PROJ_TM = 1024
PROJ_TN = 1024
PROJ_CHUNK = 256
K_COL = (FOURIER_WIDTH + NA_WIDTH) // PROJ_TN
V_COL = (FOURIER_WIDTH + 2 * NA_WIDTH) // PROJ_TN


PROJ_STEPS = (IN_WIDTH // PROJ_TN) * (T_ALL // PROJ_TM)
CAST_SHAPES = ((N_EXPERTS * D_MODEL, EXPERT_HIDDEN, 512),
               (N_EXPERTS * D_MODEL, EXPERT_HIDDEN, 512),
               (N_EXPERTS * EXPERT_HIDDEN, D_MODEL, 128),
               (FOURIER_WIDTH, D_MODEL, 16),
               (NA_WIDTH, D_MODEL, 16),
               (D_MODEL, D_MODEL, 32))
N_CAST = len(CAST_SHAPES)


KV_ROWS = PROJ_TM * NA_HEADS


def _kv_copy(kv_ref, dst_hbm, tile, sem):
    rows = pl.ds(pl.multiple_of(tile * KV_ROWS, KV_ROWS), KV_ROWS)
    return pltpu.make_async_copy(kv_ref, dst_hbm.at[rows, :], sem)


def _proj_kernel(h_ref, w_ref, b_ref, *rest):
    cast_in = rest[:N_CAST]
    p_ref, k_hbm, v_hbm = rest[N_CAST:N_CAST + 3]
    cast_out = rest[N_CAST + 3:2 * N_CAST + 3]
    wb_ref, acc_ref, kv_ref, sem = rest[-4:]
    j = pl.program_id(0)
    i = pl.program_id(1)
    n_ctx = T_CTX // PROJ_TM
    is_k = (j == K_COL) & (i < n_ctx)
    is_v = (j == V_COL) & (i < n_ctx)

    @pl.when(i == 0)
    def _():
        wb_ref[...] = w_ref[...].astype(bf16)

    n_chunks = PROJ_TN // PROJ_CHUNK
    for c in range(n_chunks):
        cols = slice(c * PROJ_CHUNK, (c + 1) * PROJ_CHUNK)
        acc = jnp.dot(h_ref[...], wb_ref[:, cols], preferred_element_type=f32) + b_ref[:, cols]
        p_ref[:, cols] = acc.astype(bf16)
        acc_ref[:, cols] = acc
        for src, dst in list(zip(cast_in, cast_out))[c::n_chunks]:
            dst[...] = src[...].astype(bf16)

    @pl.when(is_k | is_v)
    def _():
        @pl.when(~((j == K_COL) & (i == 0)))
        def _():
            _kv_copy(kv_ref, k_hbm, 0, sem).wait()

        for head in range(NA_HEADS):
            kv_ref[pl.ds(head, PROJ_TM, stride=NA_HEADS), :] = acc_ref[:, head * HEAD_DIM:(head + 1) * HEAD_DIM]

    @pl.when(is_k)
    def _():
        _kv_copy(kv_ref, k_hbm, i, sem).start()

    @pl.when(is_v)
    def _():
        _kv_copy(kv_ref, v_hbm, i, sem).start()

    @pl.when((j == V_COL) & (i == n_ctx))
    def _():
        _kv_copy(kv_ref, v_hbm, 0, sem).wait()


def _cast_spec(rows, cols, rb):
    m = T_ALL // PROJ_TM
    return pl.BlockSpec((rb, cols), lambda j, i: (jnp.minimum(j * m + i, rows // rb - 1), 0))


def _project(h1, w_in, b_in, cast_weights):
    assert all(rows // rb <= PROJ_STEPS for rows, _, rb in CAST_SHAPES)
    assert T_CTX // PROJ_TM < T_ALL // PROJ_TM
    cast_specs = [_cast_spec(*s) for s in CAST_SHAPES]
    hbm = pl.BlockSpec(memory_space=pl.ANY)
    outs = pl.pallas_call(
        _proj_kernel,
        grid=(IN_WIDTH // PROJ_TN, T_ALL // PROJ_TM),
        in_specs=[pl.BlockSpec((PROJ_TM, D_MODEL), lambda j, i: (i, 0)),
                  pl.BlockSpec((D_MODEL, PROJ_TN), lambda j, i: (0, j)),
                  pl.BlockSpec((1, PROJ_TN), lambda j, i: (0, j))] + cast_specs,
        out_specs=[pl.BlockSpec((PROJ_TM, PROJ_TN), lambda j, i: (i, j)), hbm, hbm] + cast_specs,
        out_shape=[jax.ShapeDtypeStruct((T_ALL, IN_WIDTH), bf16),
                   jax.ShapeDtypeStruct((T_CTX * NA_HEADS, HEAD_DIM), f32),
                   jax.ShapeDtypeStruct((T_CTX * NA_HEADS, HEAD_DIM), f32)]
                  + [jax.ShapeDtypeStruct((rows, cols), bf16) for rows, cols, _ in CAST_SHAPES],
        scratch_shapes=[pltpu.VMEM((D_MODEL, PROJ_TN), bf16), pltpu.VMEM((PROJ_TM, PROJ_TN), f32),
                        pltpu.VMEM((KV_ROWS, HEAD_DIM), f32), pltpu.SemaphoreType.DMA(())],
        compiler_params=_params(2),
        name="project",
    )(h1, w_in, b_in, *cast_weights)
    return outs[0], outs[1], outs[2], outs[3:]


Q_GROUP_ROWS = 4
N_Q_GROUPS = GRID_ROWS // Q_GROUP_ROWS
Q_GROUP = Q_GROUP_ROWS * GRID_W
KEY_SPAN_ROWS = 12
KEY_SPAN = KEY_SPAN_ROWS * GRID_W


def _window_start(r):
    return min(max(r - WIN_ROWS // 2, 0), GRID_ROWS - WIN_ROWS)


def _key_base(g):
    lo = _window_start(g * Q_GROUP_ROWS)
    hi = _window_start((g + 1) * Q_GROUP_ROWS - 1) + WIN_ROWS
    assert hi - lo <= KEY_SPAN_ROWS
    return min(lo, GRID_ROWS - KEY_SPAN_ROWS)


def _visible_span(g):
    base = _key_base(g)
    first = _window_start(g * Q_GROUP_ROWS) - base
    last = _window_start((g + 1) * Q_GROUP_ROWS - 1) + WIN_ROWS - base
    lo = (first // 2) * 2 * GRID_W
    hi = -(-last // 2) * 2 * GRID_W
    return lo, hi


def _block_shift(g, rr):
    return _key_base(g) - (g * Q_GROUP_ROWS + rr) + WIN_ROWS - 1


TABLE_PAD = -min(_block_shift(g, rr) for g in range(N_Q_GROUPS) for rr in range(Q_GROUP_ROWS))
TABLE_BLOCKS = 2 * (-(-(max(_block_shift(g, rr) for g in range(N_Q_GROUPS) for rr in range(Q_GROUP_ROWS))
                        + TABLE_PAD + KEY_SPAN_ROWS + 1) // 2))
TABLE_LANES = TABLE_BLOCKS * GRID_W


def _bias_kernel(r0_ref, r1_ref, o_ref):
    qc = lax.broadcasted_iota(jnp.int32, (GRID_W, TABLE_LANES), 0)
    lane = lax.broadcasted_iota(jnp.int32, (GRID_W, TABLE_LANES), 1)
    kc = lane & (GRID_W - 1)
    dc = kc - qc + (WIN_COLS - 1)
    start = jnp.clip(qc - WIN_COLS // 2, 0, GRID_W - WIN_COLS)
    ok = (kc >= start) & (kc < start + WIN_COLS)
    tables = []
    for r_ref in (r0_ref, r1_ref):
        acc = jnp.zeros((GRID_W, TABLE_LANES), f32)
        for b in range(2 * WIN_COLS - 1):
            acc = jnp.where(dc == b, r_ref[0, b:b + 1, :], acc)
        tables.append(jnp.where(ok, acc, NEG_INF))
    span_row = jnp.right_shift(lax.broadcasted_iota(jnp.int32, (GRID_W, KEY_SPAN), 1), GRID_W.bit_length() - 1)
    for g in range(N_Q_GROUPS):
        for rr in range(Q_GROUP_ROWS):
            first = _window_start(g * Q_GROUP_ROWS + rr) - _key_base(g)
            e = _block_shift(g, rr) + TABLE_PAD
            lo = (e - e % 2) * GRID_W
            in_window = (span_row >= first) & (span_row < first + WIN_ROWS)
            o_ref[0, g, rr * GRID_W:(rr + 1) * GRID_W, :] = jnp.where(
                in_window, tables[e % 2][:, lo:lo + KEY_SPAN], NEG_INF)


def _bias_table(rpb):
    nb = 2 * WIN_COLS - 1
    n_off = 2 * WIN_ROWS - 1
    by_row = jnp.transpose(rpb, (0, 2, 1))
    rpbx = []
    for p in range(2):
        spread = np.zeros((n_off, TABLE_LANES), np.float32)
        for t in range(TABLE_BLOCKS):
            if 0 <= t + p - TABLE_PAD < n_off:
                spread[t + p - TABLE_PAD, t * GRID_W:(t + 1) * GRID_W] = 1.0
        rpbx.append(jnp.einsum("hba,al->hbl", by_row, jnp.asarray(spread), precision=lax.Precision.HIGHEST))
    table_spec = pl.BlockSpec((1, nb, TABLE_LANES), lambda h: (h, 0, 0))
    return pl.pallas_call(
        _bias_kernel,
        grid=(NA_HEADS,),
        in_specs=[table_spec, table_spec],
        out_specs=pl.BlockSpec((1, N_Q_GROUPS, Q_GROUP, KEY_SPAN), lambda h: (h, 0, 0, 0)),
        out_shape=jax.ShapeDtypeStruct((NA_HEADS, N_Q_GROUPS, Q_GROUP, KEY_SPAN), f32),
        compiler_params=_params(1),
        name="bias_table",
    )(*rpbx)


def _qk(q, k):
    return lax.dot_general(q, k, (((1,), (1,)), ((), ())), preferred_element_type=f32)


def _ctx_mix_kernel(q_ref, k_ref, v_ref, u_ref, chan_ref, pos_ref, na_ref, fm_ref):
    scale = HEAD_DIM ** -0.5
    gd = FOURIER_GROUP_DIM

    def head(h):
        sl = slice(h * HEAD_DIM, (h + 1) * HEAD_DIM)
        s = _qk(q_ref[:, sl], k_ref[:, sl]) * scale
        p = jnp.exp(s - jnp.max(s, axis=-1, keepdims=True))
        l = jnp.sum(p, axis=-1, keepdims=True)
        o = jnp.dot(p.astype(bf16), v_ref[:, sl], preferred_element_type=f32) / l
        na_ref[:, sl] = o.astype(bf16)

    ys = []
    for g in range(FOURIER_GROUPS):
        head(g)
        ys.append(jnp.dot(u_ref[:, g * gd:(g + 1) * gd], chan_ref[...], preferred_element_type=f32))
    yc = jnp.concatenate([y[:, :gd] for y in ys], axis=1)
    ysn = jnp.concatenate([y[:, gd:] for y in ys], axis=1)
    stacked = jnp.concatenate([yc, ysn], axis=0).astype(bf16)
    for h in range(FOURIER_GROUPS, NA_HEADS):
        head(h)
        if h == FOURIER_GROUPS:
            fm_ref[...] = jnp.dot(pos_ref[...], stacked, preferred_element_type=f32).astype(bf16)


def _ctx_mix(p_all):
    qcol = FOURIER_WIDTH // NA_WIDTH
    chan, pos = _dft_consts(SEQ)
    return pl.pallas_call(
        _ctx_mix_kernel,
        grid=(BATCH,),
        in_specs=[pl.BlockSpec((SEQ, NA_WIDTH), lambda b: (b, qcol)),
                  pl.BlockSpec((SEQ, NA_WIDTH), lambda b: (b, qcol + 1)),
                  pl.BlockSpec((SEQ, NA_WIDTH), lambda b: (b, qcol + 2)),
                  pl.BlockSpec((SEQ, FOURIER_WIDTH), lambda b: (b, 0)),
                  pl.BlockSpec((FOURIER_GROUP_DIM, 2 * FOURIER_GROUP_DIM), lambda b: (0, 0)),
                  pl.BlockSpec((SEQ, 2 * SEQ), lambda b: (0, 0))],
        out_specs=[pl.BlockSpec((SEQ, NA_WIDTH), lambda b: (b, 0)),
                   pl.BlockSpec((SEQ, FOURIER_WIDTH), lambda b: (b, 0))],
        out_shape=[jax.ShapeDtypeStruct((T_CTX, NA_WIDTH), bf16),
                   jax.ShapeDtypeStruct((T_CTX, FOURIER_WIDTH), bf16)],
        compiler_params=_params(1),
        name="ctx_mix",
    )(p_all, p_all, p_all, p_all, chan, pos)


def _attn_lat_kernel(q_ref, k_ref, v_ref, kc_ref, vc_ref, bias_ref, o_ref):
    scale = HEAD_DIM ** -0.5
    head_rows = pl.ds(pl.program_id(0), PAST_LEN, stride=NA_HEADS)
    kcb = kc_ref[0, head_rows, :].astype(bf16)
    vcb = vc_ref[0, head_rows, :].astype(bf16)
    for g in range(N_Q_GROUPS):
        lo, hi = _visible_span(g)
        k0 = _key_base(g) * GRID_W + lo
        nk = hi - lo
        q = q_ref[g * Q_GROUP:(g + 1) * Q_GROUP, :]
        s_nb = _qk(q, k_ref[k0:k0 + nk, :]) * scale + bias_ref[0, g, :, lo:hi]
        s_cx = _qk(q, kcb) * scale
        m = jnp.maximum(jnp.max(s_nb, axis=-1, keepdims=True), jnp.max(s_cx, axis=-1, keepdims=True))
        p_nb = jnp.exp(s_nb - m)
        p_cx = jnp.exp(s_cx - m)
        l = jnp.sum(p_nb, axis=-1, keepdims=True) + jnp.sum(p_cx, axis=-1, keepdims=True)
        o = (jnp.dot(p_nb.astype(bf16), v_ref[k0:k0 + nk, :], preferred_element_type=f32)
             + jnp.dot(p_cx.astype(bf16), vcb, preferred_element_type=f32)) / l
        o_ref[g * Q_GROUP:(g + 1) * Q_GROUP, :] = o.astype(bf16)


def _attn_lat(p_all, ck, cv, bias):
    row0 = T_CTX // DEC_SEQ
    qcol = FOURIER_WIDTH // HEAD_DIM
    return pl.pallas_call(
        _attn_lat_kernel,
        grid=(NA_HEADS, DEC_BATCH),
        in_specs=[pl.BlockSpec((DEC_SEQ, HEAD_DIM), lambda h, b: (row0 + b, qcol + h)),
                  pl.BlockSpec((DEC_SEQ, HEAD_DIM), lambda h, b: (row0 + b, qcol + NA_HEADS + h)),
                  pl.BlockSpec((DEC_SEQ, HEAD_DIM), lambda h, b: (row0 + b, qcol + 2 * NA_HEADS + h)),
                  pl.BlockSpec((1, PAST_LEN * NA_HEADS, HEAD_DIM), lambda h, b: (b, 0, 0)),
                  pl.BlockSpec((1, PAST_LEN * NA_HEADS, HEAD_DIM), lambda h, b: (b, 0, 0)),
                  pl.BlockSpec((1, N_Q_GROUPS, Q_GROUP, KEY_SPAN), lambda h, b: (h, 0, 0, 0))],
        out_specs=pl.BlockSpec((DEC_SEQ, HEAD_DIM), lambda h, b: (b, h)),
        out_shape=jax.ShapeDtypeStruct((T_LAT, NA_WIDTH), bf16),
        compiler_params=_params(2),
        name="attn_lat",
    )(p_all, p_all, p_all, ck, cv, bias)


def _dft_consts(n):
    def cs(m):
        idx = (np.arange(m)[:, None] * np.arange(m)[None, :]) % m
        ang = 2.0 * np.pi * idx.astype(np.float64) / m
        return np.cos(ang), np.sin(ang)

    cc, sc = cs(FOURIER_GROUP_DIM)
    cn, sn = cs(n)
    chan = np.concatenate([cc, sc], axis=1) / np.sqrt(FOURIER_GROUP_DIM)
    pos = np.concatenate([cn, -sn], axis=1) / np.sqrt(n)
    return jnp.asarray(chan, f32).astype(bf16), jnp.asarray(pos, f32).astype(bf16)


def _fourier_kernel(u_ref, chan_ref, pos_ref, o_ref):
    gd = FOURIER_GROUP_DIM
    ys = [jnp.dot(u_ref[:, g * gd:(g + 1) * gd], chan_ref[...], preferred_element_type=f32)
          for g in range(FOURIER_GROUPS)]
    yc = jnp.concatenate([y[:, :gd] for y in ys], axis=1)
    ysn = jnp.concatenate([y[:, gd:] for y in ys], axis=1)
    stacked = jnp.concatenate([yc, ysn], axis=0).astype(bf16)
    o_ref[...] = jnp.dot(pos_ref[...], stacked, preferred_element_type=f32).astype(bf16)


def _fourier(p_all, n, n_req, row0):
    chan, pos = _dft_consts(n)
    return pl.pallas_call(
        _fourier_kernel,
        grid=(n_req,),
        in_specs=[pl.BlockSpec((n, FOURIER_WIDTH), lambda b: (row0 + b, 0)),
                  pl.BlockSpec((FOURIER_GROUP_DIM, 2 * FOURIER_GROUP_DIM), lambda b: (0, 0)),
                  pl.BlockSpec((n, 2 * n), lambda b: (0, 0))],
        out_specs=pl.BlockSpec((n, FOURIER_WIDTH), lambda b: (b, 0)),
        out_shape=jax.ShapeDtypeStruct((n_req * n, FOURIER_WIDTH), bf16),
        compiler_params=_params(1),
        name=f"fourier_{n}",
    )(p_all, chan, pos)


MERGE_TM = 256
ROUTE_LANES = LANES


def _class_of(logits):
    lane = lax.broadcasted_iota(jnp.int32, logits.shape, 1)
    big = jnp.int32(ROUTE_LANES)
    is_group = lane < N_GROUPS
    mg = jnp.max(jnp.where(is_group, logits, -jnp.inf), axis=-1, keepdims=True)
    gsel = jnp.min(jnp.where(is_group & (logits == mg), lane, big), axis=-1, keepdims=True)
    lo = N_GROUPS + EXPERTS_PER_GROUP * gsel
    in_group = (lane >= lo) & (lane < lo + EXPERTS_PER_GROUP)
    m1 = jnp.max(jnp.where(in_group, logits, -jnp.inf), axis=-1, keepdims=True)
    i1 = jnp.min(jnp.where(in_group & (logits == m1), lane, big), axis=-1, keepdims=True)
    rest = in_group & (lane != i1)
    m2 = jnp.max(jnp.where(rest, logits, -jnp.inf), axis=-1, keepdims=True)
    i2 = jnp.min(jnp.where(rest & (logits == m2), lane, big), axis=-1, keepdims=True)
    e_lo = jnp.minimum(i1, i2) - lo
    e_hi = jnp.maximum(i1, i2) - lo
    pair = jnp.zeros_like(e_lo)
    for idx in range(N_PAIRS):
        a, b = sorted((PAIR_A[idx], PAIR_B[idx]))
        pair = jnp.where((e_lo == a) & (e_hi == b), idx, pair)
    return gsel * N_PAIRS + pair


def _merge_kernel(fmc_ref, fml_ref, nac_ref, nal_ref, ga_ref, gb_ref, xp_ref, xs_ref, mods_ref, g2_ref,
                  wf_ref, wn_ref, wo_ref, wr_ref, br_ref,
                  x1_ref, h2c_ref, info_ref, cnt_ref, carry_ref, x1s_ref):
    s = pl.program_id(0)
    n_ctx = T_CTX // MERGE_TM
    n_all = T_ALL // MERGE_TM
    per_req = DEC_SEQ // MERGE_TM
    row = _mod_row(jnp.minimum(s, n_all - 1), n_ctx, per_req)
    prev_row = _mod_row(jnp.maximum(s - 1, 0), n_ctx, per_req)

    @pl.when(s == 0)
    def _():
        carry_ref[...] = jnp.zeros_like(carry_ref)
        x1s_ref[1] = jnp.zeros((MERGE_TM, D_MODEL), f32)

    def step(tile_refs):
        x1p = x1s_ref[(s + 1) % 2]
        if tile_refs:
            fm_ref, na_ref, x_ref = tile_refs
            a = jnp.dot(fm_ref[...], wf_ref[...], preferred_element_type=f32)

        h2 = _rms(x1p) * g2_ref[...] * (1.0 + _mod_vec(mods_ref, prev_row, 4)) + _mod_vec(mods_ref, prev_row, 3)
        if tile_refs:
            gate_a = jax.nn.sigmoid(ga_ref[...].astype(f32))
            b = jnp.dot(na_ref[...], wn_ref[...], preferred_element_type=f32)

        for k in range(CHUNKS):
            h2c_ref[pl.ds(k, MERGE_TM, stride=CHUNKS), :] = h2[:, k * LANES:(k + 1) * LANES]
        logits = jnp.dot(h2.astype(bf16), wr_ref[...], preferred_element_type=f32) + br_ref[...]
        if tile_refs:
            merged = gate_a * a + jax.nn.sigmoid(gb_ref[...].astype(f32)) * b
            out = jnp.dot(merged.astype(bf16), wo_ref[...], preferred_element_type=f32)

        cls = _class_of(logits)
        lane = lax.broadcasted_iota(jnp.int32, logits.shape, 1)
        onehot = ((lane == cls) & (s > 0)).astype(f32)
        r = lax.broadcasted_iota(jnp.int32, (MERGE_TM, MERGE_TM), 0)
        c = lax.broadcasted_iota(jnp.int32, (MERGE_TM, MERGE_TM), 1)
        before = (c < r).astype(bf16)
        prefix = jnp.dot(before, onehot.astype(bf16), preferred_element_type=f32) + carry_ref[...]
        rank = jnp.sum(onehot * prefix, axis=-1, keepdims=True)
        info_ref[...] = jnp.where(lane == 0, cls.astype(f32), jnp.where(lane == 1, rank, 0.0))
        carry_ref[...] = carry_ref[...] + jnp.sum(onehot, axis=0, keepdims=True)
        cnt_ref[...] = jnp.broadcast_to(carry_ref[...], cnt_ref.shape)
        if tile_refs:
            x1 = x_ref[...] + _mod_vec(mods_ref, row, 2) * out
            x1_ref[...] = x1
            x1s_ref[s % 2] = x1

    @pl.when(s < n_ctx)
    def _():
        step((fmc_ref, nac_ref, xp_ref))

    @pl.when((s >= n_ctx) & (s < n_all))
    def _():
        step((fml_ref, nal_ref, xs_ref))

    @pl.when(s == n_all)
    def _():
        step(None)


def _merge(fm_c, fm_l, na_c, na_l, p_all, xp, xs, mods, g2, wf, wn, wo, wr, br):
    n_ctx = T_CTX // MERGE_TM
    n_lat = T_LAT // MERGE_TM
    n_all = n_ctx + n_lat
    gcol = (FOURIER_WIDTH + 3 * NA_WIDTH) // D_MODEL
    const = lambda i: (0, 0)
    ctx_tile = lambda i: (jnp.minimum(i, n_ctx - 1), 0)
    lat_tile = lambda i: (jnp.clip(i - n_ctx, 0, n_lat - 1), 0)
    this_tile = lambda i: jnp.minimum(i, n_all - 1)
    prev_tile = lambda i: (jnp.maximum(i - 1, 0), 0)
    once = pl.Buffered(1)
    return pl.pallas_call(
        _merge_kernel,
        grid=(n_all + 1,),
        in_specs=[pl.BlockSpec((MERGE_TM, FOURIER_WIDTH), ctx_tile),
                  pl.BlockSpec((MERGE_TM, FOURIER_WIDTH), lat_tile),
                  pl.BlockSpec((MERGE_TM, NA_WIDTH), ctx_tile),
                  pl.BlockSpec((MERGE_TM, NA_WIDTH), lat_tile),
                  pl.BlockSpec((MERGE_TM, D_MODEL), lambda i: (this_tile(i), gcol)),
                  pl.BlockSpec((MERGE_TM, D_MODEL), lambda i: (this_tile(i), gcol + 1)),
                  pl.BlockSpec((MERGE_TM, D_MODEL), ctx_tile),
                  pl.BlockSpec((MERGE_TM, D_MODEL), lat_tile),
                  pl.BlockSpec((MOD_ROWS, N_MOD * D_MODEL), const, pipeline_mode=once),
                  pl.BlockSpec((1, D_MODEL), const, pipeline_mode=once),
                  pl.BlockSpec((FOURIER_WIDTH, D_MODEL), const, pipeline_mode=once),
                  pl.BlockSpec((NA_WIDTH, D_MODEL), const, pipeline_mode=once),
                  pl.BlockSpec((D_MODEL, D_MODEL), const, pipeline_mode=once),
                  pl.BlockSpec((D_MODEL, ROUTE_LANES), const, pipeline_mode=once),
                  pl.BlockSpec((1, ROUTE_LANES), const, pipeline_mode=once)],
        out_specs=[pl.BlockSpec((MERGE_TM, D_MODEL), lambda i: (this_tile(i), 0)),
                   pl.BlockSpec((MERGE_TM * CHUNKS, LANES), prev_tile),
                   pl.BlockSpec((MERGE_TM, ROUTE_LANES), prev_tile),
                   pl.BlockSpec((8, ROUTE_LANES), const)],
        out_shape=[jax.ShapeDtypeStruct((T_ALL, D_MODEL), f32),
                   jax.ShapeDtypeStruct((T_ALL * CHUNKS, LANES), f32),
                   jax.ShapeDtypeStruct((T_ALL, ROUTE_LANES), f32),
                   jax.ShapeDtypeStruct((8, ROUTE_LANES), f32)],
        scratch_shapes=[pltpu.VMEM((1, ROUTE_LANES), f32), pltpu.VMEM((2, MERGE_TM, D_MODEL), f32)],
        compiler_params=_params(1),
        name="merge",
    )(fm_c, fm_l, na_c, na_l, p_all, p_all, xp, xs, mods, g2, wf, wn, wo, wr, br)


def _perm_kernel(pos_ref, zeros_hbm, src_ref):
    pltpu.sync_copy(zeros_hbm, src_ref)

    def place(t, carry):
        src_ref[pos_ref[t]] = t
        return carry

    lax.fori_loop(0, T_ALL, place, 0, unroll=16)


def _inverse_positions(pos):
    smem = pl.BlockSpec(memory_space=pltpu.SMEM)
    return pl.pallas_call(
        _perm_kernel,
        in_specs=[smem, pl.BlockSpec(memory_space=pl.ANY)],
        out_specs=smem,
        out_shape=jax.ShapeDtypeStruct((T_PAD,), jnp.int32),
        name="permutation",
    )(pos, jnp.zeros((T_PAD,), jnp.int32))


def _row_copy(idx_ref, base, src_hbm, buf, sem, slot, k):
    tok = idx_ref[base + k]
    return pltpu.make_async_copy(src_hbm.at[pl.ds(pl.multiple_of(tok * CHUNKS, CHUNKS), CHUNKS), :],
                                 buf.at[slot, pl.ds(pl.multiple_of(k * CHUNKS, CHUNKS), CHUNKS), :],
                                 sem.at[slot])


def _start_gather(idx_ref, base, src_hbm, buf, sem, slot, rows, queues=(0, 1)):
    def body(k8, carry):
        for j in range(8):
            _row_copy(idx_ref, base, src_hbm, buf, sem, slot, k8 * 8 + j).start(priority=queues[j % len(queues)])
        return carry

    lax.fori_loop(0, rows // 8, body, 0)


def _wait_gather(buf, sem, slot):
    pltpu.make_async_copy(buf.at[slot], buf.at[slot], sem.at[slot]).wait()


def _gathered_rows(buf, slot, rows):
    return jnp.concatenate([buf[slot, pl.ds(s, rows, stride=CHUNKS), :] for s in range(CHUNKS)], axis=1)


def _moe_kernel(ea_ref, eb_ref, nt_ref, src_ref, h2c_hbm, wga_ref, wua_ref, wda_ref, wgb_ref, wub_ref, wdb_ref,
                wr_ref, br_ref, o_ref, buf, sem):
    i = pl.program_id(0)
    nt = nt_ref[0]

    @pl.when(i >= nt)
    def _():
        o_ref[...] = jnp.zeros_like(o_ref)

    def fetch(tile):
        _start_gather(src_ref, tile * MOE_TILE, h2c_hbm, buf, sem, tile % MOE_SLOTS, MOE_TILE, queues=(1,))

    @pl.when(i == 0)
    def _():
        for t in range(MOE_AHEAD):
            @pl.when(t < nt)
            def _(t=t):
                fetch(t)

    @pl.when(i + MOE_AHEAD < nt)
    def _():
        fetch(i + MOE_AHEAD)

    @pl.when(i < nt)
    def _():
        slot = i % MOE_SLOTS
        _wait_gather(buf, sem, slot)
        x = _gathered_rows(buf, slot, MOE_TILE).astype(bf16)
        logits = jnp.dot(x, wr_ref[...], preferred_element_type=f32) + br_ref[...]
        lane = lax.broadcasted_iota(jnp.int32, logits.shape, 1)
        ea = ea_ref[i]
        eb = eb_ref[i]

        def pick(col):
            return jnp.sum(jnp.where(lane == col, logits, 0.0), axis=-1, keepdims=True)

        lg = pick(ea // EXPERTS_PER_GROUP)
        p_group = 1.0 / jnp.sum(jnp.where(lane < N_GROUPS, jnp.exp(logits - lg), 0.0), axis=-1, keepdims=True)
        la = pick(N_GROUPS + ea)
        lb = pick(N_GROUPS + eb)
        w_a = p_group / (1.0 + jnp.exp(lb - la))
        w_b = p_group / (1.0 + jnp.exp(la - lb))

        def up(wg_ref, wu_ref):
            return (jnp.dot(x, wg_ref[0], preferred_element_type=f32),
                    jnp.dot(x, wu_ref[0], preferred_element_type=f32))

        def act(g, u):
            return (g * jax.nn.sigmoid(g) * u).astype(bf16)

        g_a, u_a = up(wga_ref, wua_ref)
        g_b, u_b = up(wgb_ref, wub_ref)
        hid_a = act(g_a, u_a)
        hid_b = act(g_b, u_b)
        per_chunk = MOE_OUT_CHUNK // LANES
        for c in range(D_MODEL // MOE_OUT_CHUNK):
            cols = slice(c * MOE_OUT_CHUNK, (c + 1) * MOE_OUT_CHUNK)
            y = (w_a * jnp.dot(hid_a, wda_ref[0, :, cols], preferred_element_type=f32)
                 + w_b * jnp.dot(hid_b, wdb_ref[0, :, cols], preferred_element_type=f32))
            for k in range(per_chunk):
                o_ref[pl.ds(c * per_chunk + k, MOE_TILE, stride=CHUNKS), :] = y[:, k * LANES:(k + 1) * LANES]


def _moe(tile_ea, tile_eb, ntiles, src, h2c, wg, wu, wd, wr, br):
    up = lambda tab: (lambda i, ea, eb, nt, src: (tab(ea, eb)[i], 0, 0))
    pick_a = lambda ea, eb: ea
    pick_b = lambda ea, eb: eb
    const = lambda i, ea, eb, nt, src: (0, 0)
    grid_spec = pltpu.PrefetchScalarGridSpec(
        num_scalar_prefetch=4,
        grid=(MOE_MAX_TILES,),
        in_specs=[pl.BlockSpec(memory_space=pl.ANY),
                  pl.BlockSpec((1, D_MODEL, EXPERT_HIDDEN), up(pick_a)),
                  pl.BlockSpec((1, D_MODEL, EXPERT_HIDDEN), up(pick_a)),
                  pl.BlockSpec((1, EXPERT_HIDDEN, D_MODEL), up(pick_a)),
                  pl.BlockSpec((1, D_MODEL, EXPERT_HIDDEN), up(pick_b)),
                  pl.BlockSpec((1, D_MODEL, EXPERT_HIDDEN), up(pick_b)),
                  pl.BlockSpec((1, EXPERT_HIDDEN, D_MODEL), up(pick_b)),
                  pl.BlockSpec((D_MODEL, ROUTE_LANES), const),
                  pl.BlockSpec((1, ROUTE_LANES), const)],
        out_specs=pl.BlockSpec((MOE_TILE * CHUNKS, LANES), lambda i, ea, eb, nt, src: (i, 0)),
        scratch_shapes=[pltpu.VMEM((MOE_SLOTS, MOE_TILE * CHUNKS, LANES), f32),
                        pltpu.SemaphoreType.DMA((MOE_SLOTS,))],
    )
    return pl.pallas_call(
        _moe_kernel,
        grid_spec=grid_spec,
        out_shape=jax.ShapeDtypeStruct((T_PAD * CHUNKS, LANES), f32),
        compiler_params=_params(1),
        name="moe",
    )(tile_ea, tile_eb, ntiles, src, h2c, wg, wu, wd, wg, wu, wd, wr, br)


def _final_kernel(pos_ref, ys_hbm, x1_ref, mods_ref, g_ref, o_ref, buf, sem, *, tile0, n_tiles, n_ctx_tiles):
    i = pl.program_id(0)

    def fetch(tile):
        _start_gather(pos_ref, (tile0 + tile) * FINAL_TILE, ys_hbm, buf, sem, tile % FINAL_SLOTS, FINAL_TILE)

    @pl.when(i == 0)
    def _():
        for t in range(min(FINAL_AHEAD, n_tiles)):
            fetch(t)

    @pl.when(i + FINAL_AHEAD < n_tiles)
    def _():
        fetch(i + FINAL_AHEAD)

    slot = i % FINAL_SLOTS
    _wait_gather(buf, sem, slot)
    y = _gathered_rows(buf, slot, FINAL_TILE)
    row = _mod_row(tile0 + i, n_ctx_tiles, DEC_SEQ // FINAL_TILE)
    x2 = x1_ref[...] + _mod_vec(mods_ref, row, 5) * y
    o_ref[...] = _rms(x2) * g_ref[...]


def _final(pos, ys, x1, mods, g, tile0, n_tiles):
    body = functools.partial(_final_kernel, tile0=tile0, n_tiles=n_tiles, n_ctx_tiles=T_CTX // FINAL_TILE)
    grid_spec = pltpu.PrefetchScalarGridSpec(
        num_scalar_prefetch=1,
        grid=(n_tiles,),
        in_specs=[pl.BlockSpec(memory_space=pl.ANY),
                  pl.BlockSpec((FINAL_TILE, D_MODEL), lambda i, pos: (tile0 + i, 0)),
                  pl.BlockSpec((MOD_ROWS, N_MOD * D_MODEL), lambda i, pos: (0, 0)),
                  pl.BlockSpec((1, D_MODEL), lambda i, pos: (0, 0))],
        out_specs=pl.BlockSpec((FINAL_TILE, D_MODEL), lambda i, pos: (i, 0)),
        scratch_shapes=[pltpu.VMEM((FINAL_SLOTS, FINAL_TILE * CHUNKS, LANES), f32),
                        pltpu.SemaphoreType.DMA((FINAL_SLOTS,))],
    )
    return pl.pallas_call(
        body,
        grid_spec=grid_spec,
        out_shape=jax.ShapeDtypeStruct((n_tiles * FINAL_TILE, D_MODEL), f32),
        compiler_params=_params(1),
        name=f"final_{tile0}",
    )(pos, ys, x1, mods, g)


def kernel(x_prompt, x_sample, cache_k_ctx, cache_v_ctx, c, c_ctx, norm1_g, norm2_g, w_ada, b_ada, w_in, b_in,
           w_fourier, w_na_o, rpb, w_out, w_router_group, b_router_group, w_router_expert, b_router_expert,
           w_exp_gate, w_exp_up, w_exp_down, final_norm_g):
    xp = x_prompt.reshape(T_CTX, D_MODEL)
    xs = x_sample.reshape(T_LAT, D_MODEL)
    c_all = jnp.concatenate([c_ctx[None, :], c, jnp.zeros((MOD_ROWS - 1 - DEC_BATCH, D_MODEL), f32)], axis=0)
    w_route = jnp.concatenate([w_router_group[0], w_router_expert[0]], axis=1)
    w_route = jnp.pad(w_route, ((0, 0), (0, ROUTE_LANES - w_route.shape[1]))).astype(bf16)
    b_route = jnp.concatenate([b_router_group[0], b_router_expert[0]])
    b_route = jnp.pad(b_route, (0, ROUTE_LANES - b_route.shape[0]))[None, :]

    mods = _ada(c_all, w_ada[0], b_ada[0][None, :])
    h1 = _modulate1(xp, xs, mods, norm1_g)
    p_all, k_new, v_new, (wg, wu, wd, wf, wn, wo) = _project(
        h1, w_in[0], b_in,
        (w_exp_gate.reshape(N_EXPERTS * D_MODEL, EXPERT_HIDDEN), w_exp_up.reshape(N_EXPERTS * D_MODEL, EXPERT_HIDDEN),
         w_exp_down.reshape(N_EXPERTS * EXPERT_HIDDEN, D_MODEL), w_fourier[0], w_na_o[0], w_out[0]))
    wg = wg.reshape(N_EXPERTS, D_MODEL, EXPERT_HIDDEN)
    wu = wu.reshape(N_EXPERTS, D_MODEL, EXPERT_HIDDEN)
    wd = wd.reshape(N_EXPERTS, EXPERT_HIDDEN, D_MODEL)

    tt = _bias_table(rpb[0])
    na_c, fm_c = _ctx_mix(p_all)
    ck = cache_k_ctx.reshape(DEC_BATCH, PAST_LEN * NA_HEADS, HEAD_DIM)
    cv = cache_v_ctx.reshape(DEC_BATCH, PAST_LEN * NA_HEADS, HEAD_DIM)
    na_l = _attn_lat(p_all, ck, cv, tt)
    fm_l = _fourier(p_all, DEC_SEQ, DEC_BATCH, T_CTX // DEC_SEQ)

    x1, h2c, info, counts = _merge(fm_c, fm_l, na_c, na_l, p_all, xp, xs, mods, norm2_g, wf, wn, wo,
                                   w_route, b_route)

    cnt = counts[0, :N_CLASSES].astype(jnp.int32)
    padded = ((cnt + MOE_TILE - 1) // MOE_TILE) * MOE_TILE
    off_end = jnp.cumsum(padded)
    off = off_end - padded
    ntiles = (off_end[-1] // MOE_TILE).astype(jnp.int32)
    tile_start = jnp.arange(MOE_MAX_TILES, dtype=jnp.int32) * MOE_TILE
    last_start = (ntiles - 1) * MOE_TILE
    tile_cls = jnp.sum(jnp.minimum(tile_start, last_start)[:, None] >= off_end[None, :], axis=1)
    pair = tile_cls % N_PAIRS
    group = tile_cls // N_PAIRS
    pair_hot = pair[:, None] == jnp.arange(N_PAIRS)[None, :]
    tile_ea = (group * EXPERTS_PER_GROUP + jnp.sum(pair_hot * jnp.asarray(PAIR_A), axis=1)).astype(jnp.int32)
    tile_eb = (group * EXPERTS_PER_GROUP + jnp.sum(pair_hot * jnp.asarray(PAIR_B), axis=1)).astype(jnp.int32)

    class_hot = info[:, 0:1] == jnp.arange(N_CLASSES, dtype=f32)[None, :]
    pos = (info[:, 1] + jnp.sum(jnp.where(class_hot, off.astype(f32)[None, :], 0.0), axis=1)).astype(jnp.int32)
    src = _inverse_positions(pos)
    ys = _moe(tile_ea, tile_eb, ntiles[None], src, h2c, wg, wu, wd, w_route, b_route)

    fg = final_norm_g[None, :]
    y_prompt = _final(pos, ys, x1, mods, fg, 0, T_CTX // FINAL_TILE)
    y_sample = _final(pos, ys, x1, mods, fg, T_CTX // FINAL_TILE, T_LAT // FINAL_TILE)

    shape_kv = (BATCH, 1, SEQ, NA_HEADS, HEAD_DIM)
    return (y_prompt.reshape(BATCH, SEQ, D_MODEL), y_sample.reshape(DEC_BATCH, DEC_SEQ, D_MODEL),
            k_new.reshape(shape_kv), v_new.reshape(shape_kv))
```
